```python
import math
import jax
import jax.numpy as jnp
from jax import lax
import numpy as np

D_MODEL = 2048
BATCH = 2
SEQ = 4096
DEPTH = 4
DEC_BATCH = 128
DEC_SEQ = 8
PAST_LEN = 8192
PAGE_SIZE = 128

N_HYB = (DEPTH + 1) // 2
N_REC = DEPTH // 2
CONV_W = D_MODEL // 2
CONV_K = 3
Q_LORA = 512
KV_LORA = 512
NOPE_DIM = 128
ROPE_DIM = 64
V_DIM = 128
MLA_HEADS = (D_MODEL // 2) // V_DIM
MLA_SCALE = (NOPE_DIM + ROPE_DIM) ** -0.5
ROPE_THETA = 10000.0
Q_BLOCK = 128
HYB_IN = 3 * CONV_W + Q_LORA + KV_LORA + ROPE_DIM
HYB_MIX = CONV_W + MLA_HEADS * V_DIM
REC_DK = 128
REC_HEADS = D_MODEL // REC_DK
REC_DV = D_MODEL // REC_HEADS
REC_W = REC_HEADS * REC_DK
REC_VW = REC_HEADS * REC_DV
REC_CHUNK = 64
N_GROUPS = 4
EXPERTS_PER_GROUP = 8
N_EXPERTS = N_GROUPS * EXPERTS_PER_GROUP
TOP_K = 2
D_EXPERT = D_MODEL // 2
MOE_BLOCK = 128
DN_ALPHA = (2 * DEPTH) ** 0.25
DN_BETA = (8 * DEPTH) ** -0.25
LN_EPS = 1e-5
RMS_EPS = 1e-6

kernel_name = 'hybrid_conv_mla_hgrn2_hmoe_step'


def _layer_norm(x, g, b):
    xf = x.astype(jnp.float32)
    mu = jnp.mean(xf, -1, keepdims=True)
    var = jnp.mean(jnp.square(xf - mu), -1, keepdims=True)
    y = (xf - mu) * lax.rsqrt(var + LN_EPS) * g.astype(jnp.float32) + b.astype(jnp.float32)
    return y.astype(x.dtype)


def _rms_norm(x, g):
    xf = x.astype(jnp.float32)
    y = xf * lax.rsqrt(jnp.mean(jnp.square(xf), -1, keepdims=True) + RMS_EPS) * g.astype(jnp.float32)
    return y.astype(x.dtype)


def _rope(x, pos):
    half = ROPE_DIM // 2
    inv_freq = ROPE_THETA ** (-jnp.arange(half, dtype=jnp.float32) / half)
    ang = pos.astype(jnp.float32)[:, None] * inv_freq
    ang = ang.reshape((ang.shape[0],) + (1,) * (x.ndim - 3) + (half,))
    cos, sin = jnp.cos(ang), jnp.sin(ang)
    x1 = x[..., :half].astype(jnp.float32)
    x2 = x[..., half:].astype(jnp.float32)
    return jnp.concatenate([x1 * cos - x2 * sin, x1 * sin + x2 * cos], -1).astype(x.dtype)


def _short_conv(u, state, w):
    t = u.shape[1]
    u_ext = jnp.concatenate([state.astype(u.dtype), u], axis=1)
    v = sum(w[j] * u_ext[:, j:j + t] for j in range(CONV_K))
    return v, u_ext[:, t:]


def _mla_prompt(q_nope, q_pe, lat, k_pe, w_uk, w_uv):
    b, s, h, _ = q_nope.shape
    nb = s // Q_BLOCK
    k_nope = jnp.einsum('bsc,chd->bshd', lat, w_uk)
    v = jnp.einsum('bsc,chv->bshv', lat, w_uv)
    k_pos = jnp.arange(s)

    def blocks(a):
        return a.reshape((b, nb, Q_BLOCK) + a.shape[2:]).swapaxes(0, 1)

    def attend(args):
        qn, qp, start = args
        sc = (jnp.einsum('bqhd,bkhd->bhqk', qn, k_nope, preferred_element_type=jnp.float32)
              + jnp.einsum('bqhr,bkr->bhqk', qp, k_pe, preferred_element_type=jnp.float32)) * MLA_SCALE
        q_pos = start + jnp.arange(Q_BLOCK)
        sc = jnp.where(k_pos[None, :] <= q_pos[:, None], sc, -jnp.inf)
        p = jax.nn.softmax(sc, axis=-1).astype(v.dtype)
        return jnp.einsum('bhqk,bkhv->bqhv', p, v)

    o = lax.map(attend, (blocks(q_nope), blocks(q_pe), jnp.arange(nb) * Q_BLOCK))
    return o.swapaxes(0, 1).reshape(b, s, h, V_DIM)


def _mla_sample(q_nope, q_pe, lat, k_pe, lat_past, pe_past, w_uk, w_uv):
    t = q_nope.shape[1]
    past = lat_past.shape[1]
    q_lat = jnp.einsum('bthd,chd->bthc', q_nope, w_uk)
    s_past = (jnp.einsum('bthc,bsc->bhts', q_lat, lat_past, preferred_element_type=jnp.float32)
              + jnp.einsum('bthr,bsr->bhts', q_pe, pe_past, preferred_element_type=jnp.float32))
    s_new = (jnp.einsum('bthc,bsc->bhts', q_lat, lat, preferred_element_type=jnp.float32)
             + jnp.einsum('bthr,bsr->bhts', q_pe, k_pe, preferred_element_type=jnp.float32))
    causal = jnp.tril(jnp.ones((t, t), dtype=bool))
    s_new = jnp.where(causal, s_new, -jnp.inf)
    p = jax.nn.softmax(jnp.concatenate([s_past, s_new], -1) * MLA_SCALE, axis=-1).astype(lat.dtype)
    o_lat = (jnp.einsum('bhts,bsc->bthc', p[..., :past], lat_past)
             + jnp.einsum('bhts,bsc->bthc', p[..., past:], lat))
    return jnp.einsum('bthc,chv->bthv', o_lat, w_uv)


def _hybrid_mixer(x, pos, conv_state, past, w_in, conv_w, q_norm_g, kv_norm_g, w_q_up, w_uk, w_uv, w_out):
    bsz, t, _ = x.shape
    h = x @ w_in
    cuts = (CONV_W, 2 * CONV_W, 3 * CONV_W, 3 * CONV_W + Q_LORA, 3 * CONV_W + Q_LORA + KV_LORA)
    conv_x, gate_b, gate_c, q_c, kv_c, k_pe = jnp.split(h, cuts, axis=-1)
    conv_out, new_conv = _short_conv(gate_c * conv_x, conv_state, conv_w)
    a_out = gate_b * conv_out
    q = jnp.einsum('btc,chd->bthd', _rms_norm(q_c, q_norm_g), w_q_up)
    q_nope = q[..., :NOPE_DIM]
    q_pe = _rope(q[..., NOPE_DIM:], pos)
    lat = _rms_norm(kv_c, kv_norm_g)
    k_pe = _rope(k_pe, pos)
    if past is None:
        o = _mla_prompt(q_nope, q_pe, lat, k_pe, w_uk, w_uv)
    else:
        o = _mla_sample(q_nope, q_pe, lat, k_pe, past[0], past[1], w_uk, w_uv)
    mix = jnp.concatenate([a_out, o.reshape(bsz, t, MLA_HEADS * V_DIM)], -1)
    return mix @ w_out, lat, k_pe, new_conv


def _gla_chunked(q, k, v, log_f, s0, chunk):
    b, t, h, dk = q.shape
    dv = v.shape[-1]
    nc = t // chunk

    def to_chunks(a):
        return a.reshape(b, nc, chunk, h, a.shape[-1]).transpose(1, 0, 3, 2, 4)

    mask = jnp.tril(jnp.ones((chunk, chunk), dtype=bool))

    def step(s, inp):
        qc, kc, vc, gc = inp
        cum = jnp.cumsum(gc, axis=2)
        inter = jnp.einsum('bhtk,bhkv->bhtv', qc * jnp.exp(cum), s)
        diff = cum[:, :, :, None, :] - cum[:, :, None, :, :]
        decay = jnp.exp(jnp.where(mask[:, :, None], diff, -jnp.inf))
        att = jnp.einsum('bhtk,bhsk,bhtsk->bhts', qc, kc, decay)
        intra = jnp.einsum('bhts,bhsv->bhtv', att, vc)
        last = cum[:, :, -1:, :]
        s_new = (jnp.exp(last[:, :, 0, :])[..., None] * s
                 + jnp.einsum('bhsk,bhsv->bhkv', kc * jnp.exp(last - cum), vc))
        return s_new, inter + intra

    s_fin, o = lax.scan(step, s0.astype(jnp.float32),
                        (to_chunks(q), to_chunks(k), to_chunks(v), to_chunks(log_f)))
    return o.transpose(1, 0, 3, 2, 4).reshape(b, t, h, dv), s_fin


def _hgrn2_mixer(x, state, lower_bound, w_in, norm_g, w_out):
    bsz, t, _ = x.shape
    q, f, i, g = jnp.split(x @ w_in, (REC_W, 2 * REC_W, 2 * REC_W + REC_VW), axis=-1)
    q = jax.nn.silu(q.astype(jnp.float32)) * REC_DK ** -0.5
    lb = lower_bound.astype(jnp.float32)
    forget = lb + (1.0 - lb) * jax.nn.sigmoid(f.astype(jnp.float32))
    key_in = 1.0 - forget
    log_f = jnp.log(forget)

    def heads(a, d):
        return a.reshape(bsz, t, REC_HEADS, d)

    o, new_state = _gla_chunked(heads(q, REC_DK), heads(key_in, REC_DK),
                                heads(i.astype(jnp.float32), REC_DV), heads(log_f, REC_DK),
                                state, math.gcd(REC_CHUNK, t))
    o = _rms_norm(o.reshape(bsz, t, REC_VW), norm_g) * jax.nn.silu(g.astype(jnp.float32))
    return o.astype(x.dtype) @ w_out, new_state


def _hier_moe(x, w_rg, b_rg, w_re, b_re, w_gate, w_up, w_down):
    shape = x.shape
    xf = x.reshape(-1, shape[-1])
    n = xf.shape[0]
    g_logits = (xf @ w_rg).astype(jnp.float32) + b_rg.astype(jnp.float32)
    g_prob = jax.nn.softmax(g_logits, -1)
    g_sel = jnp.argmax(g_logits, -1).astype(jnp.int32)
    g_gate = jnp.take_along_axis(g_prob, g_sel[:, None], 1)
    e_logits = ((xf @ w_re).astype(jnp.float32) + b_re.astype(jnp.float32)).reshape(n, N_GROUPS, EXPERTS_PER_GROUP)
    e_in = jnp.take_along_axis(e_logits, g_sel[:, None, None], 1)[:, 0]
    top_p, top_i = lax.top_k(jax.nn.softmax(e_in, -1), TOP_K)
    weights = (g_gate * top_p / jnp.sum(top_p, -1, keepdims=True)).reshape(-1)
    expert_ids = (g_sel[:, None] * EXPERTS_PER_GROUP + top_i).reshape(-1)
    n_assign = n * TOP_K
    n_blocks = -(-n_assign // MOE_BLOCK) + N_EXPERTS
    cap = n_blocks * MOE_BLOCK
    order = jnp.argsort(expert_ids)
    sorted_e = expert_ids[order]
    counts = jnp.bincount(expert_ids, length=N_EXPERTS)
    padded = (counts + MOE_BLOCK - 1) // MOE_BLOCK * MOE_BLOCK
    starts = jnp.cumsum(counts) - counts
    pad_ends = jnp.cumsum(padded)
    pad_starts = pad_ends - padded
    dest = pad_starts[sorted_e] + jnp.arange(n_assign) - starts[sorted_e]
    tok = jnp.repeat(jnp.arange(n, dtype=jnp.int32), TOP_K)[order]
    slot_tok = jnp.full((cap,), n, jnp.int32).at[dest].set(tok)
    slot_w = jnp.zeros((cap,), jnp.float32).at[dest].set(weights[order])
    block_e = jnp.minimum(jnp.searchsorted(pad_ends, jnp.arange(n_blocks) * MOE_BLOCK, side='right'),
                          N_EXPERTS - 1)
    x_pad = jnp.concatenate([xf, jnp.zeros((1, xf.shape[1]), xf.dtype)], 0)
    xb = x_pad[slot_tok].reshape(n_blocks, MOE_BLOCK, xf.shape[1])

    def run(args):
        xe, e = args
        hid = jax.nn.silu(xe @ w_gate[e]) * (xe @ w_up[e])
        return hid @ w_down[e]

    yb = lax.map(run, (xb, block_e)).reshape(cap, xf.shape[1]).astype(jnp.float32)
    out = jnp.zeros((n + 1, xf.shape[1]), jnp.float32).at[slot_tok].add(yb * slot_w[:, None])
    return out[:n].astype(x.dtype).reshape(shape)


def setup_inputs(seed: int = 0) -> dict:
    key = jax.random.key(seed)
    ks = jax.random.split(key, 32)
    f32 = jnp.float32

    def nrm(i, shape, scale):
        return jax.random.normal(ks[i], shape, f32) * scale

    n_pages = PAST_LEN // PAGE_SIZE
    n_phys = (DEC_BATCH * n_pages * 5) // 4
    page_table = jax.random.permutation(ks[6], n_phys)[:DEC_BATCH * n_pages]
    page_table = page_table.reshape(DEC_BATCH, n_pages).astype(jnp.int32)
    return {
        'x_prompt': nrm(0, (BATCH, SEQ, D_MODEL), 1.0),
        'x_sample': nrm(1, (DEC_BATCH, DEC_SEQ, D_MODEL), 1.0),
        'cache_kv_latent': nrm(2, (N_HYB, n_phys, PAGE_SIZE, KV_LORA), 1.0),
        'cache_k_rope': nrm(3, (N_HYB, n_phys, PAGE_SIZE, ROPE_DIM), 1.0),
        'state_conv': nrm(4, (N_HYB, DEC_BATCH, CONV_K - 1, CONV_W), 1.0),
        'state_hgrn': nrm(5, (N_REC, DEC_BATCH, REC_HEADS, REC_DK, REC_DV), 1.0),
        'page_table': page_table,
        'w_in_hyb': nrm(7, (N_HYB, D_MODEL, HYB_IN), D_MODEL ** -0.5),
        'conv_w': nrm(8, (N_HYB, CONV_K, CONV_W), CONV_K ** -0.5),
        'q_norm_g': 1.0 + nrm(9, (N_HYB, Q_LORA), 0.02),
        'kv_norm_g': 1.0 + nrm(10, (N_HYB, KV_LORA), 0.02),
        'w_q_up': nrm(11, (N_HYB, Q_LORA, MLA_HEADS, NOPE_DIM + ROPE_DIM), Q_LORA ** -0.5),
        'w_uk': nrm(12, (N_HYB, KV_LORA, MLA_HEADS, NOPE_DIM), KV_LORA ** -0.5),
        'w_uv': nrm(13, (N_HYB, KV_LORA, MLA_HEADS, V_DIM), KV_LORA ** -0.5),
        'w_out_hyb': nrm(14, (N_HYB, HYB_MIX, D_MODEL), HYB_MIX ** -0.5 * DN_BETA),
        'w_in_rec': nrm(15, (N_REC, D_MODEL, 2 * REC_W + 2 * REC_VW), D_MODEL ** -0.5),
        'lb_logits': nrm(16, (DEPTH, REC_W), 0.5),
        'rec_norm_g': 1.0 + nrm(17, (N_REC, REC_VW), 0.02),
        'w_out_rec': nrm(18, (N_REC, REC_VW, D_MODEL), REC_VW ** -0.5 * DN_BETA),
        'ln1_g': 1.0 + nrm(19, (DEPTH, D_MODEL), 0.02),
        'ln1_b': nrm(20, (DEPTH, D_MODEL), 0.02),
        'ln2_g': 1.0 + nrm(21, (DEPTH, D_MODEL), 0.02),
        'ln2_b': nrm(22, (DEPTH, D_MODEL), 0.02),
        'w_router_group': nrm(23, (DEPTH, D_MODEL, N_GROUPS), D_MODEL ** -0.5),
        'b_router_group': nrm(24, (DEPTH, N_GROUPS), 0.01),
        'w_router_expert': nrm(25, (DEPTH, D_MODEL, N_EXPERTS), D_MODEL ** -0.5),
        'b_router_expert': nrm(26, (DEPTH, N_EXPERTS), 0.01),
        'w_gate': nrm(27, (DEPTH, N_EXPERTS, D_MODEL, D_EXPERT), D_MODEL ** -0.5),
        'w_up': nrm(28, (DEPTH, N_EXPERTS, D_MODEL, D_EXPERT), D_MODEL ** -0.5),
        'w_down': nrm(29, (DEPTH, N_EXPERTS, D_EXPERT, D_MODEL), D_EXPERT ** -0.5 * DN_BETA),
    }


def reference(x_prompt, x_sample, cache_kv_latent, cache_k_rope, state_conv, state_hgrn, page_table,
              w_in_hyb, conv_w, q_norm_g, kv_norm_g, w_q_up, w_uk, w_uv, w_out_hyb,
              w_in_rec, lb_logits, rec_norm_g, w_out_rec,
              ln1_g, ln1_b, ln2_g, ln2_b,
              w_router_group, b_router_group, w_router_expert, b_router_expert,
              w_gate, w_up, w_down):
    bp, tp = x_prompt.shape[0], x_prompt.shape[1]
    bs, ts = x_sample.shape[0], x_sample.shape[1]
    pos_p = jnp.arange(tp)
    pos_s = PAST_LEN + jnp.arange(ts)
    lower_bounds = jnp.cumsum(jax.nn.softmax(lb_logits.astype(jnp.float32), axis=0), axis=0)
    lower_bounds = lower_bounds - lower_bounds[0]

    yp, ys = x_prompt, x_sample
    lat_p, rope_p, lat_s, rope_s, conv_p, conv_s, rec_p, rec_s = [], [], [], [], [], [], [], []
    for layer in range(DEPTH):
        j = layer // 2
        if layer % 2 == 0:
            hyb_w = (w_in_hyb[j], conv_w[j], q_norm_g[j], kv_norm_g[j], w_q_up[j], w_uk[j], w_uv[j], w_out_hyb[j])
            zero_conv = jnp.zeros((bp, CONV_K - 1, CONV_W), x_prompt.dtype)
            mp, la, kr, cv = _hybrid_mixer(yp, pos_p, zero_conv, None, *hyb_w)
            past = (cache_kv_latent[j, page_table].reshape(bs, -1, KV_LORA),
                    cache_k_rope[j, page_table].reshape(bs, -1, ROPE_DIM))
            ms, la_s, kr_s, cv_s = _hybrid_mixer(ys, pos_s, state_conv[j], past, *hyb_w)
            lat_p.append(la)
            rope_p.append(kr)
            conv_p.append(cv)
            lat_s.append(la_s)
            rope_s.append(kr_s)
            conv_s.append(cv_s)
        else:
            rec_w = (lower_bounds[layer], w_in_rec[j], rec_norm_g[j], w_out_rec[j])
            zero_rec = jnp.zeros((bp, REC_HEADS, REC_DK, REC_DV), jnp.float32)
            mp, sp = _hgrn2_mixer(yp, zero_rec, *rec_w)
            ms, ss = _hgrn2_mixer(ys, state_hgrn[j], *rec_w)
            rec_p.append(sp)
            rec_s.append(ss)
        yp = _layer_norm(DN_ALPHA * yp + mp, ln1_g[layer], ln1_b[layer])
        ys = _layer_norm(DN_ALPHA * ys + ms, ln1_g[layer], ln1_b[layer])
        moe_w = (w_router_group[layer], b_router_group[layer], w_router_expert[layer], b_router_expert[layer],
                 w_gate[layer], w_up[layer], w_down[layer])
        yp = _layer_norm(DN_ALPHA * yp + _hier_moe(yp, *moe_w), ln2_g[layer], ln2_b[layer])
        ys = _layer_norm(DN_ALPHA * ys + _hier_moe(ys, *moe_w), ln2_g[layer], ln2_b[layer])

    lat_prompt = jnp.stack(lat_p)
    rope_prompt = jnp.stack(rope_p)
    lat_sample = jnp.stack(lat_s)
    rope_sample = jnp.stack(rope_s)
    conv_prompt = jnp.stack(conv_p)
    conv_sample = jnp.stack(conv_s)
    hgrn_prompt = jnp.stack(rec_p)
    hgrn_sample = jnp.stack(rec_s)
    return (yp, ys, lat_prompt, rope_prompt, lat_sample, rope_sample, conv_prompt, conv_sample, hgrn_prompt, hgrn_sample)
```

```python
import functools
import math

import numpy as np
import jax
import jax.numpy as jnp
from jax import lax
from jax.experimental import pallas as pl
from jax.experimental.pallas import tpu as pltpu

F32 = jnp.float32
BF16 = jnp.bfloat16

CONV_K = 3
Q_LORA = 512
KV_LORA = 512
NOPE_DIM = 128
ROPE_DIM = 64
V_DIM = 128
REC_DK = 128
N_GROUPS = 4
EXPERTS_PER_GROUP = 8
N_EXPERTS = N_GROUPS * EXPERTS_PER_GROUP
TOP_K = 2
ROPE_THETA = 10000.0
LN_EPS = 1e-5
RMS_EPS = 1e-6
MLA_SCALE = (NOPE_DIM + ROPE_DIM) ** -0.5

LANES = 128
SUBLANES = 8
VMEM_LIMIT_BYTES = 52 * 1024 * 1024

HEAD_PAD = 2 * LANES
GLA_CHUNK = 64
MOE_TILE = 256
ATTN_TQ = 512
PAGES_PER_STEP = 16


def _cparams(*sem):
    return pltpu.CompilerParams(dimension_semantics=sem, vmem_limit_bytes=VMEM_LIMIT_BYTES)


def _tile(n, pref):
    if n <= pref:
        return n
    for t in range(pref, 7, -1):
        if n % t == 0 and t % 8 == 0:
            return t
    return n


def _dot(a, b):
    return jnp.dot(a, b, preferred_element_type=F32)


def _dot_nt(a, b):
    return lax.dot_general(a, b, (((1,), (1,)), ((), ())), preferred_element_type=F32)


def _dot_tn(a, b):
    return lax.dot_general(a, b, (((0,), (0,)), ((), ())), preferred_element_type=F32)


def _mm_kernel(x_ref, w_ref, o_ref):
    o_ref[...] = _dot(x_ref[...].astype(BF16), w_ref[...]).astype(o_ref.dtype)


def _mm(x, w, *, tm=1024, tn=512, out_dtype=F32, name="mm"):
    m, k = x.shape
    n = w.shape[1]
    tm = _tile(m, tm)
    tn = _tile(n, tn) if n % LANES == 0 else n
    return pl.pallas_call(
        _mm_kernel,
        grid=(m // tm, n // tn),
        in_specs=[pl.BlockSpec((tm, k), lambda i, j: (i, 0)),
                  pl.BlockSpec((k, tn), lambda i, j: (0, j))],
        out_specs=pl.BlockSpec((tm, tn), lambda i, j: (i, j)),
        out_shape=jax.ShapeDtypeStruct((m, n), out_dtype),
        compiler_params=_cparams("parallel", "arbitrary"),
        name=name,
    )(x, w)


def _layer_norm_rows(z, g, b):
    mu = jnp.mean(z, axis=-1, keepdims=True)
    zc = z - mu
    var = jnp.mean(zc * zc, axis=-1, keepdims=True)
    return zc * lax.rsqrt(var + LN_EPS) * g + b


def _rms_norm_rows(z, g):
    return z * lax.rsqrt(jnp.mean(z * z, axis=-1, keepdims=True) + RMS_EPS) * g


def _conv_prompt_kernel(cx_ref, gb_ref, gc_ref, cxp_ref, gcp_ref, w_ref, a_ref, st_ref, *, tt):
    i = pl.program_id(1)
    u = gc_ref[...] * cx_ref[...]
    prev = jnp.where(i > 0, gcp_ref[...] * cxp_ref[...], 0.0)
    row = lax.broadcasted_iota(jnp.int32, u.shape, 0)
    u1 = jnp.where(row == 0, prev[7:8], pltpu.roll(u, 1, 0))
    u2 = jnp.where(row == 0, prev[6:7], jnp.where(row == 1, prev[7:8], pltpu.roll(u, 2, 0)))
    v = w_ref[0:1, :] * u2 + w_ref[1:2, :] * u1 + w_ref[2:3, :] * u
    a_ref[...] = gb_ref[...] * v

    @pl.when(i == pl.num_programs(1) - 1)
    def _():
        st_ref[...] = u[tt - (CONV_K - 1):tt]


def _conv_prompt(h, conv_w, bsz, seq, cw):
    tt = _tile(seq, 512)
    nt = seq // tt
    sub = tt // SUBLANES

    def main(c):
        return pl.BlockSpec((tt, cw), lambda b, i: (b * nt + i, c))

    def halo(c):
        return pl.BlockSpec((SUBLANES, cw), lambda b, i: (jnp.maximum((b * nt + i) * sub - 1, 0), c))

    return pl.pallas_call(
        functools.partial(_conv_prompt_kernel, tt=tt),
        grid=(bsz, nt),
        in_specs=[main(0), main(1), main(2), halo(0), halo(2),
                  pl.BlockSpec((CONV_K, cw), lambda b, i: (0, 0))],
        out_specs=[pl.BlockSpec((tt, cw), lambda b, i: (b * nt + i, 0)),
                   pl.BlockSpec((None, CONV_K - 1, cw), lambda b, i: (b, 0, 0))],
        out_shape=[jax.ShapeDtypeStruct((bsz * seq, cw), F32),
                   jax.ShapeDtypeStruct((bsz, CONV_K - 1, cw), F32)],
        compiler_params=_cparams("parallel", "arbitrary"),
        name="conv_prompt",
    )(h, h, h, h, h, conv_w)


def _conv_sample_kernel(cx_ref, gb_ref, gc_ref, st_ref, w_ref, a_ref, sto_ref, *, nb, ts):
    cw = cx_ref.shape[-1]
    u = (gc_ref[...] * cx_ref[...]).reshape(nb, ts, cw)
    st = st_ref[...]
    t = lax.broadcasted_iota(jnp.int32, u.shape, 1)
    s0 = st[:, 0:1, :]
    s1 = st[:, 1:2, :]
    u1 = jnp.where(t == 0, s1, pltpu.roll(u, 1, 1))
    u2 = jnp.where(t == 0, s0, jnp.where(t == 1, s1, pltpu.roll(u, 2, 1)))
    w = w_ref[...]
    v = w[0:1, :][None] * u2 + w[1:2, :][None] * u1 + w[2:3, :][None] * u
    a_ref[...] = gb_ref[...] * v.reshape(nb * ts, cw)
    sto_ref[...] = u[:, ts - (CONV_K - 1):ts, :]


def _conv_sample(h, state, conv_w, row0, bsz, ts, cw):
    nb = _tile(bsz, 64)
    rows = nb * ts
    off = row0 // rows

    def main(c):
        return pl.BlockSpec((rows, cw), lambda i: (off + i, c))

    return pl.pallas_call(
        functools.partial(_conv_sample_kernel, nb=nb, ts=ts),
        grid=(bsz // nb,),
        in_specs=[main(0), main(1), main(2),
                  pl.BlockSpec((nb, CONV_K - 1, cw), lambda i: (i, 0, 0)),
                  pl.BlockSpec((CONV_K, cw), lambda i: (0, 0))],
        out_specs=[pl.BlockSpec((rows, cw), lambda i: (i, 0)),
                   pl.BlockSpec((nb, CONV_K - 1, cw), lambda i: (i, 0, 0))],
        out_shape=[jax.ShapeDtypeStruct((bsz * ts, cw), F32),
                   jax.ShapeDtypeStruct((bsz, CONV_K - 1, cw), F32)],
        compiler_params=_cparams("parallel"),
        name="conv_sample",
    )(h, h, h, state, conv_w)


def _rope_lanes(x, c, sa, sb):
    half = ROPE_DIM // 2
    return x * c + pltpu.roll(x, LANES - half, 1) * sa + pltpu.roll(x, half, 1) * sb


def _q_proj_kernel(qc_ref, g_ref, w_ref, c_ref, sa_ref, sb_ref, o_ref, *, heads):
    xn = _rms_norm_rows(qc_ref[...], g_ref[...]).astype(BF16)
    q = _dot(xn, w_ref[...])
    c, sa, sb = c_ref[...], sa_ref[...], sb_ref[...]
    for hd in range(heads):
        lo = hd * HEAD_PAD
        o_ref[:, lo:lo + LANES] = q[:, lo:lo + LANES].astype(o_ref.dtype)
        pe = _rope_lanes(q[:, lo + LANES:lo + HEAD_PAD], c, sa, sb)
        o_ref[:, lo + LANES:lo + HEAD_PAD] = pe.astype(o_ref.dtype)


def _q_proj(h, col_block, g, w_cat, rope_c, rope_sa, rope_sb, heads):
    m = h.shape[0]
    tm = _tile(m, 512)
    n = heads * HEAD_PAD
    row = lambda i: (i, 0)
    return pl.pallas_call(
        functools.partial(_q_proj_kernel, heads=heads),
        grid=(m // tm,),
        in_specs=[pl.BlockSpec((tm, Q_LORA), lambda i: (i, col_block)),
                  pl.BlockSpec((1, Q_LORA), lambda i: (0, 0)),
                  pl.BlockSpec((Q_LORA, n), lambda i: (0, 0)),
                  pl.BlockSpec((tm, LANES), row), pl.BlockSpec((tm, LANES), row),
                  pl.BlockSpec((tm, LANES), row)],
        out_specs=pl.BlockSpec((tm, n), row),
        out_shape=jax.ShapeDtypeStruct((m, n), BF16),
        compiler_params=_cparams("parallel"),
        name="q_proj",
    )(h, g, w_cat, rope_c, rope_sa, rope_sb)


def _kv_norm_kernel(kv_ref, pe_ref, g_ref, c_ref, sa_ref, sb_ref, lat_ref, kpe_ref):
    lat_ref[...] = _rms_norm_rows(kv_ref[...], g_ref[...])
    kpe_ref[...] = _rope_lanes(pe_ref[...], c_ref[...], sa_ref[...], sb_ref[...])


def _kv_norm(h, col_block, h_pe, g, rope_c, rope_sa, rope_sb):
    m = h.shape[0]
    tm = _tile(m, 1024)
    row = lambda i: (i, 0)
    return pl.pallas_call(
        _kv_norm_kernel,
        grid=(m // tm,),
        in_specs=[pl.BlockSpec((tm, KV_LORA), lambda i: (i, col_block)),
                  pl.BlockSpec((tm, LANES), row),
                  pl.BlockSpec((1, KV_LORA), lambda i: (0, 0)),
                  pl.BlockSpec((tm, LANES), row), pl.BlockSpec((tm, LANES), row),
                  pl.BlockSpec((tm, LANES), row)],
        out_specs=[pl.BlockSpec((tm, KV_LORA), row), pl.BlockSpec((tm, LANES), row)],
        out_shape=[jax.ShapeDtypeStruct((m, KV_LORA), F32),
                   jax.ShapeDtypeStruct((m, LANES), F32)],
        compiler_params=_cparams("parallel"),
        name="kv_norm",
    )(h, h_pe, g, rope_c, rope_sa, rope_sb)


def _kv_expand_kernel(lat_ref, kpe_ref, wk_ref, wv_ref, k_ref, v_ref, *, heads):
    lat = lat_ref[...].astype(BF16)
    kn = _dot(lat, wk_ref[...])
    kpe = kpe_ref[...].astype(k_ref.dtype)
    for hd in range(heads):
        lo = hd * HEAD_PAD
        k_ref[:, lo:lo + LANES] = kn[:, hd * NOPE_DIM:(hd + 1) * NOPE_DIM].astype(k_ref.dtype)
        k_ref[:, lo + LANES:lo + HEAD_PAD] = kpe
    v_ref[...] = _dot(lat, wv_ref[...]).astype(v_ref.dtype)


def _kv_expand(lat, kpe, w_uk, w_uv, rows, heads):
    tm = _tile(rows, 512)
    row = lambda i: (i, 0)
    fixed = lambda i: (0, 0)
    return pl.pallas_call(
        functools.partial(_kv_expand_kernel, heads=heads),
        grid=(rows // tm,),
        in_specs=[pl.BlockSpec((tm, KV_LORA), row), pl.BlockSpec((tm, LANES), row),
                  pl.BlockSpec((KV_LORA, heads * NOPE_DIM), fixed),
                  pl.BlockSpec((KV_LORA, heads * V_DIM), fixed)],
        out_specs=[pl.BlockSpec((tm, heads * HEAD_PAD), row),
                   pl.BlockSpec((tm, heads * V_DIM), row)],
        out_shape=[jax.ShapeDtypeStruct((rows, heads * HEAD_PAD), BF16),
                   jax.ShapeDtypeStruct((rows, heads * V_DIM), BF16)],
        compiler_params=_cparams("parallel"),
        name="kv_expand",
    )(lat, kpe, w_uk, w_uv)


def _attn_prompt_kernel(q_ref, k_ref, v_ref, o_ref, m_ref, l_ref, acc_ref, *, tq):
    i = pl.program_id(2)
    m_ref[...] = jnp.full(m_ref.shape, -jnp.inf, F32)
    l_ref[...] = jnp.zeros(l_ref.shape, F32)
    acc_ref[...] = jnp.zeros(acc_ref.shape, F32)
    q = q_ref[...]

    def chunk(c, masked):
        r0 = pl.multiple_of(c * tq, tq)
        kc = k_ref[pl.ds(r0, tq), :]
        vc = v_ref[pl.ds(r0, tq), :]
        s = _dot_nt(q, kc) * MLA_SCALE
        if masked:
            row = lax.broadcasted_iota(jnp.int32, s.shape, 0)
            col = lax.broadcasted_iota(jnp.int32, s.shape, 1)
            s = jnp.where(col <= row, s, -jnp.inf)
        m_old = m_ref[...]
        m_new = jnp.maximum(m_old, jnp.max(s, axis=-1, keepdims=True))
        alpha = jnp.exp(m_old - m_new)
        p = jnp.exp(s - m_new)
        l_ref[...] = alpha * l_ref[...] + jnp.sum(p, axis=-1, keepdims=True)
        acc_ref[...] = alpha * acc_ref[...] + _dot(p.astype(BF16), vc)
        m_ref[...] = m_new

    def body(c, carry):
        chunk(c, False)
        return carry

    lax.fori_loop(0, i, body, 0)
    chunk(i, True)
    o_ref[...] = (acc_ref[...] / l_ref[...]).astype(o_ref.dtype)


def _attn_prompt(q_cat, k_cat, v, bsz, seq, heads):
    tq = _tile(seq, ATTN_TQ)
    nq = seq // tq
    return pl.pallas_call(
        functools.partial(_attn_prompt_kernel, tq=tq),
        grid=(bsz, heads, nq),
        in_specs=[pl.BlockSpec((tq, HEAD_PAD), lambda b, h, i: (b * nq + i, h)),
                  pl.BlockSpec((seq, HEAD_PAD), lambda b, h, i: (b, h)),
                  pl.BlockSpec((seq, V_DIM), lambda b, h, i: (b, h))],
        out_specs=pl.BlockSpec((tq, V_DIM), lambda b, h, i: (b * nq + i, h)),
        out_shape=jax.ShapeDtypeStruct((bsz * seq, heads * V_DIM), F32),
        scratch_shapes=[pltpu.VMEM((tq, 1), F32), pltpu.VMEM((tq, 1), F32),
                        pltpu.VMEM((tq, V_DIM), F32)],
        compiler_params=_cparams("parallel", "parallel", "arbitrary"),
        name="attn_prompt",
    )(q_cat, k_cat, v)


def _head_mm_kernel(x_ref, w_ref, o_ref, *, nt):
    x = x_ref[...].astype(BF16)
    o_ref[...] = (_dot_nt(x, w_ref[...]) if nt else _dot(x, w_ref[...])).astype(o_ref.dtype)


def _q_latent(q_cat, w_uk, row0, rows, heads):
    off = row0 // rows
    return pl.pallas_call(
        functools.partial(_head_mm_kernel, nt=True),
        grid=(heads,),
        in_specs=[pl.BlockSpec((rows, NOPE_DIM), lambda h: (off, 2 * h)),
                  pl.BlockSpec((KV_LORA, NOPE_DIM), lambda h: (0, h))],
        out_specs=pl.BlockSpec((None, rows, KV_LORA), lambda h: (h, 0, 0)),
        out_shape=jax.ShapeDtypeStruct((heads, rows, KV_LORA), BF16),
        compiler_params=_cparams("parallel"),
        name="q_latent",
    )(q_cat, w_uk)


def _o_from_latent(o_lat, w_uv, heads):
    rows = o_lat.shape[1]
    return pl.pallas_call(
        functools.partial(_head_mm_kernel, nt=False),
        grid=(heads,),
        in_specs=[pl.BlockSpec((None, rows, KV_LORA), lambda h: (h, 0, 0)),
                  pl.BlockSpec((KV_LORA, V_DIM), lambda h: (0, h))],
        out_specs=pl.BlockSpec((rows, V_DIM), lambda h: (0, h)),
        out_shape=jax.ShapeDtypeStruct((rows, heads * V_DIM), F32),
        compiler_params=_cparams("parallel"),
        name="o_from_latent",
    )(o_lat, w_uv)


def _attn_sample_kernel(pt_ref, ql_ref, qp_ref, *refs, npg, ts):
    lat_pages = refs[:npg]
    pe_pages = refs[npg:2 * npg]
    nl_ref, np_ref, o_ref, m_ref, l_ref, acc_ref = refs[2 * npg:]
    s_idx = pl.program_id(1)

    @pl.when(s_idx == 0)
    def _():
        m_ref[...] = jnp.full(m_ref.shape, -jnp.inf, F32)
        l_ref[...] = jnp.zeros(l_ref.shape, F32)
        acc_ref[...] = jnp.zeros(acc_ref.shape, F32)

    ql = ql_ref[...]
    qp = qp_ref[...]

    def update(s, values):
        m_old = m_ref[...]
        m_new = jnp.maximum(m_old, jnp.max(s, axis=-1, keepdims=True))
        alpha = jnp.exp(m_old - m_new)
        p = jnp.exp(s - m_new)
        l_ref[...] = alpha * l_ref[...] + jnp.sum(p, axis=-1, keepdims=True)
        pv = None
        for j, val in enumerate(values):
            w = val.shape[0]
            term = _dot(p[:, j * w:(j + 1) * w].astype(BF16), val)
            pv = term if pv is None else pv + term
        acc_ref[...] = alpha * acc_ref[...] + pv
        m_ref[...] = m_new

    keys = [r[...].astype(BF16) for r in lat_pages]
    scores = [(_dot_nt(ql, kk) + _dot_nt(qp, r[...].astype(BF16))) * MLA_SCALE
              for kk, r in zip(keys, pe_pages)]
    update(jnp.concatenate(scores, axis=-1), keys)

    @pl.when(s_idx == pl.num_programs(1) - 1)
    def _():
        kn = nl_ref[...].astype(BF16)
        s = (_dot_nt(ql, kn) + _dot_nt(qp, np_ref[...].astype(BF16))) * MLA_SCALE
        t_q = lax.broadcasted_iota(jnp.int32, s.shape, 0) % ts
        t_k = lax.broadcasted_iota(jnp.int32, s.shape, 1)
        update(jnp.where(t_k <= t_q, s, -jnp.inf), [kn])
        o_ref[...] = (acc_ref[...] / l_ref[...]).astype(o_ref.dtype)


def _attn_sample(page_table, q_lat, q_pe, cache_lat, cache_pe, layer, new_lat, new_pe, ts):
    bsz, qrows, _ = q_lat.shape
    n_pages = page_table.shape[1]
    page = cache_lat.shape[2]
    npg = math.gcd(PAGES_PER_STEP, n_pages)
    steps = n_pages // npg
    pt_flat = page_table.reshape(-1)

    def page_spec(width, j):
        return pl.BlockSpec((None, None, page, width),
                            lambda b, s, pt: (layer, pt[b * n_pages + s * npg + j], 0, 0))

    per_b = lambda b, s, pt: (b, 0, 0)
    grid_spec = pltpu.PrefetchScalarGridSpec(
        num_scalar_prefetch=1,
        grid=(bsz, steps),
        in_specs=([pl.BlockSpec((None, qrows, KV_LORA), per_b),
                   pl.BlockSpec((None, qrows, ROPE_DIM), per_b)]
                  + [page_spec(KV_LORA, j) for j in range(npg)]
                  + [page_spec(ROPE_DIM, j) for j in range(npg)]
                  + [pl.BlockSpec((None, 2 * ts, KV_LORA), per_b),
                     pl.BlockSpec((None, 2 * ts, ROPE_DIM), per_b)]),
        out_specs=pl.BlockSpec((None, qrows, KV_LORA), per_b),
        scratch_shapes=[pltpu.VMEM((qrows, 1), F32), pltpu.VMEM((qrows, 1), F32),
                        pltpu.VMEM((qrows, KV_LORA), F32)],
    )
    return pl.pallas_call(
        functools.partial(_attn_sample_kernel, npg=npg, ts=ts),
        grid_spec=grid_spec,
        out_shape=jax.ShapeDtypeStruct((bsz, qrows, KV_LORA), BF16),
        compiler_params=_cparams("parallel", "arbitrary"),
        name="attn_sample",
    )(pt_flat, q_lat, q_pe, *([cache_lat] * npg), *([cache_pe] * npg), new_lat, new_pe)


def _hyb_out_kernel(a_ref, o_ref, w_ref, x_ref, g_ref, b_ref, y_ref, *, alpha):
    ka = a_ref.shape[1]
    mix = _dot(a_ref[...].astype(BF16), w_ref[0:ka, :]) + _dot(o_ref[...].astype(BF16), w_ref[ka:, :])
    y_ref[...] = _layer_norm_rows(alpha * x_ref[...] + mix, g_ref[...], b_ref[...])


def _hyb_out(a, o, w_out, x, g, b, alpha):
    m, d = x.shape
    tm = _tile(m, 256)
    row = lambda i: (i, 0)
    fixed = lambda i: (0, 0)
    return pl.pallas_call(
        functools.partial(_hyb_out_kernel, alpha=alpha),
        grid=(m // tm,),
        in_specs=[pl.BlockSpec((tm, a.shape[1]), row), pl.BlockSpec((tm, o.shape[1]), row),
                  pl.BlockSpec(w_out.shape, fixed), pl.BlockSpec((tm, d), row),
                  pl.BlockSpec((1, d), fixed), pl.BlockSpec((1, d), fixed)],
        out_specs=pl.BlockSpec((tm, d), row),
        out_shape=jax.ShapeDtypeStruct((m, d), F32),
        compiler_params=_cparams("parallel"),
        name="hyb_out",
    )(a, o, w_out, x, g, b)


def _rec_out_kernel(o_ref, gate_ref, ng_ref, w_ref, x_ref, g_ref, b_ref, y_ref, *, alpha):
    gate = gate_ref[...]
    on = _rms_norm_rows(o_ref[...], ng_ref[...]) * (gate * jax.nn.sigmoid(gate))
    mix = _dot(on.astype(BF16), w_ref[...])
    y_ref[...] = _layer_norm_rows(alpha * x_ref[...] + mix, g_ref[...], b_ref[...])


def _rec_out(o, h, gate_col_block, norm_g, w_out, x, g, b, alpha):
    m, d = x.shape
    vw = o.shape[1]
    tm = _tile(m, 256)
    row = lambda i: (i, 0)
    fixed = lambda i: (0, 0)
    return pl.pallas_call(
        functools.partial(_rec_out_kernel, alpha=alpha),
        grid=(m // tm,),
        in_specs=[pl.BlockSpec((tm, vw), row),
                  pl.BlockSpec((tm, vw), lambda i: (i, gate_col_block)),
                  pl.BlockSpec((1, vw), fixed), pl.BlockSpec(w_out.shape, fixed),
                  pl.BlockSpec((tm, d), row), pl.BlockSpec((1, d), fixed),
                  pl.BlockSpec((1, d), fixed)],
        out_specs=pl.BlockSpec((tm, d), row),
        out_shape=jax.ShapeDtypeStruct((m, d), F32),
        compiler_params=_cparams("parallel"),
        name="rec_out",
    )(o, h, norm_g, w_out, x, g, b)


def _gla_levels(chunk):
    m = chunk // 2
    out = []
    while m >= 1:
        out.append(m)
        m //= 2
    return tuple(out)


def _gla_prefix_matrix(chunk):
    t = np.arange(chunk)
    j = np.arange(chunk)
    blocks = [(j[None, :] <= t[:, None])]
    for m in _gla_levels(chunk):
        ref = (t // (2 * m)) * (2 * m) + m
        blocks.append(j[None, :] <= ref[:, None])
    blocks.append(np.ones((chunk, chunk), bool))
    return np.concatenate(blocks, 0).astype(np.float32)


def _gla_gates(q_raw, f_raw, lb):
    q = q_raw * jax.nn.sigmoid(q_raw) * (REC_DK ** -0.5)
    forget = lb + (1.0 - lb) * jax.nn.sigmoid(f_raw)
    return q, 1.0 - forget, jnp.log(forget)


def _split3(x):
    hi = x.astype(BF16)
    r1 = x - hi.astype(F32)
    mid = r1.astype(BF16)
    lo = (r1 - mid.astype(F32)).astype(BF16)
    return hi, mid, lo


def _gla_prompt_kernel(q_ref, f_ref, v_ref, lb_ref, pm_ref, o_ref, st_ref, s_ref, *, tt):
    c = GLA_CHUNK
    i = pl.program_id(2)

    @pl.when(i == 0)
    def _():
        s_ref[...] = jnp.zeros(s_ref.shape, F32)

    lb = lb_ref[...]
    pm = pm_ref[...]
    levels = _gla_levels(c)
    row = lax.broadcasted_iota(jnp.int32, (c, REC_DK), 0)
    ti = lax.broadcasted_iota(jnp.int32, (c, c), 0)
    si = lax.broadcasted_iota(jnp.int32, (c, c), 1)

    def body(ci, carry):
        r0 = pl.multiple_of(ci * c, c)
        q, k, g = _gla_gates(q_ref[pl.ds(r0, c), :], f_ref[pl.ds(r0, c), :], lb)
        v = v_ref[pl.ds(r0, c), :].astype(BF16)
        hi, mid, lo = _split3(g)
        pre = _dot(pm, hi) + _dot(pm, mid) + _dot(pm, lo)
        cum = pre[0:c]
        last = pre[(len(levels) + 1) * c:(len(levels) + 2) * c]
        att = jnp.where(ti == si, _dot_nt(q.astype(BF16), k.astype(BF16)), 0.0)
        for li, m in enumerate(levels):
            ref = pre[(li + 1) * c:(li + 2) * c]
            e = jnp.exp(-jnp.abs(cum - ref))
            later = (row // m) % 2 == 1
            ql = jnp.where(later, q * e, 0.0).astype(BF16)
            kl = jnp.where(later, 0.0, k * e).astype(BF16)
            att = att + jnp.where(ti // (2 * m) == si // (2 * m), _dot_nt(ql, kl), 0.0)
        s_t = s_ref[...]
        inter = _dot_nt((q * jnp.exp(cum)).astype(BF16), s_t.astype(BF16))
        o_ref[pl.ds(r0, c), :] = inter + _dot(att.astype(BF16), v)
        kd = (k * jnp.exp(last - cum)).astype(BF16)
        s_ref[...] = s_t * jnp.exp(last[0:1, :]) + _dot_tn(v, kd)
        return carry

    lax.fori_loop(0, tt // c, body, 0)

    @pl.when(i == pl.num_programs(2) - 1)
    def _():
        st_ref[...] = s_ref[...]


def _gla_prompt(h, lb, bsz, seq, heads):
    tt = _tile(seq, 512)
    assert tt % GLA_CHUNK == 0
    nt = seq // tt
    pm = jnp.asarray(_gla_prefix_matrix(GLA_CHUNK), BF16)

    def col(base):
        return pl.BlockSpec((tt, REC_DK), lambda b, hd, i: (b * nt + i, base + hd))

    return pl.pallas_call(
        functools.partial(_gla_prompt_kernel, tt=tt),
        grid=(bsz, heads, nt),
        in_specs=[col(0), col(heads), col(2 * heads),
                  pl.BlockSpec((1, REC_DK), lambda b, hd, i: (0, hd)),
                  pl.BlockSpec(pm.shape, lambda b, hd, i: (0, 0))],
        out_specs=[pl.BlockSpec((tt, REC_DK), lambda b, hd, i: (b * nt + i, hd)),
                   pl.BlockSpec((None, None, REC_DK, REC_DK), lambda b, hd, i: (b, hd, 0, 0))],
        out_shape=[jax.ShapeDtypeStruct((bsz * seq, heads * REC_DK), F32),
                   jax.ShapeDtypeStruct((bsz, heads, REC_DK, REC_DK), F32)],
        scratch_shapes=[pltpu.VMEM((REC_DK, REC_DK), F32)],
        compiler_params=_cparams("parallel", "parallel", "arbitrary"),
        name="gla_prompt",
    )(h, h, h, lb, pm)


def _gla_sample_kernel(q_ref, f_ref, v_ref, lb_ref, hs_ref, he_ref, st_ref, o_ref, sto_ref, *, ts, heads):
    dk = REC_DK
    q, k, g = _gla_gates(q_ref[...], f_ref[...], lb_ref[...])
    v = v_ref[...]
    t = lax.broadcasted_iota(jnp.int32, q.shape, 0)
    cum = g
    sh = 1
    while sh < ts:
        cum = cum + jnp.where(t >= sh, pltpu.roll(cum, sh, 0), 0.0)
        sh *= 2
    last = cum[ts - 1:ts, :]
    xs = []
    for s in range(ts):
        dec = jnp.exp(jnp.where(t >= s, cum - cum[s:s + 1, :], -jnp.inf))
        xs.append(q * k[s:s + 1, :] * dec)
    x = jnp.concatenate(xs, axis=0)
    x_hi = x.astype(BF16)
    x_lo = (x - x_hi.astype(F32)).astype(BF16)
    att = _dot(x_hi, hs_ref[...]) + _dot(x_lo, hs_ref[...])
    att_e = _dot(att.astype(BF16), he_ref[...])
    intra = jnp.zeros(q.shape, F32)
    for s in range(ts):
        intra = intra + att_e[s * ts:(s + 1) * ts, :] * v[s:s + 1, :]
    qe = (q * jnp.exp(cum)).astype(BF16)
    kd = (k * jnp.exp(last - cum)).astype(BF16)
    dec_last = jnp.exp(last)
    eye = (lax.broadcasted_iota(jnp.int32, (dk, dk), 0)
           == lax.broadcasted_iota(jnp.int32, (dk, dk), 1))
    vb = v.astype(BF16)
    for hd in range(heads):
        sl = slice(hd * dk, (hd + 1) * dk)
        s_h = st_ref[hd]
        o_ref[:, sl] = _dot(qe[:, sl], s_h.astype(BF16)) + intra[:, sl]
        d_col = jnp.sum(jnp.where(eye, dec_last[:, sl], 0.0), axis=1, keepdims=True)
        sto_ref[hd] = s_h * d_col + _dot_tn(kd[:, sl], vb[:, sl])


def _gla_sample(h, lb, state, j, row0, bsz, ts, heads):
    width = heads * REC_DK
    off = row0 // ts
    head_of = np.arange(width) // REC_DK
    hsum = (head_of[:, None] == np.arange(LANES)[None, :]).astype(np.float32)
    hs = jnp.asarray(hsum, BF16)
    he = jnp.asarray(hsum.T, BF16)

    def col(cb):
        return pl.BlockSpec((ts, width), lambda b: (off + b, cb))

    fixed = lambda b: (0, 0)
    st_in = pl.BlockSpec((None, None, heads, REC_DK, REC_DK), lambda b: (j, b, 0, 0, 0))
    st_out = pl.BlockSpec((None, heads, REC_DK, REC_DK), lambda b: (b, 0, 0, 0))
    return pl.pallas_call(
        functools.partial(_gla_sample_kernel, ts=ts, heads=heads),
        grid=(bsz,),
        in_specs=[col(0), col(1), col(2), pl.BlockSpec((1, width), fixed),
                  pl.BlockSpec(hs.shape, fixed), pl.BlockSpec(he.shape, fixed), st_in],
        out_specs=[pl.BlockSpec((ts, width), lambda b: (b, 0)), st_out],
        out_shape=[jax.ShapeDtypeStruct((bsz * ts, width), F32),
                   jax.ShapeDtypeStruct(state.shape[1:], F32)],
        compiler_params=_cparams("parallel"),
        name="gla_sample",
    )(h, h, h, lb, hs, he, state)


def _router_kernel(x_ref, wh_ref, wl_ref, b_ref, o_ref):
    x = x_ref[...]
    x_hi = x.astype(BF16)
    x_lo = (x - x_hi.astype(F32)).astype(BF16)
    wh = wh_ref[...]
    logits = _dot(x_hi, wh) + _dot(x_lo, wh) + _dot(x_hi, wl_ref[...]) + b_ref[...]
    lane = lax.broadcasted_iota(jnp.int32, logits.shape, 1).astype(F32)
    big = float(LANES)
    neg = -jnp.inf
    gl = jnp.where(lane < N_GROUPS, logits, neg)
    g_max = jnp.max(gl, axis=-1, keepdims=True)
    g_sel = jnp.min(jnp.where(gl == g_max, lane, big), axis=-1, keepdims=True)
    g_gate = 1.0 / jnp.sum(jnp.exp(gl - g_max), axis=-1, keepdims=True)
    lo = N_GROUPS + g_sel * EXPERTS_PER_GROUP
    el = jnp.where((lane >= lo) & (lane < lo + EXPERTS_PER_GROUP), logits, neg)
    e_max = jnp.max(el, axis=-1, keepdims=True)
    i1 = jnp.min(jnp.where(el == e_max, lane, big), axis=-1, keepdims=True)
    denom = jnp.sum(jnp.exp(el - e_max), axis=-1, keepdims=True)
    el2 = jnp.where(lane == i1, neg, el)
    m2 = jnp.max(el2, axis=-1, keepdims=True)
    i2 = jnp.min(jnp.where(el2 == m2, lane, big), axis=-1, keepdims=True)
    p1 = 1.0 / denom
    p2 = jnp.exp(m2 - e_max) / denom
    w1 = g_gate * p1 / (p1 + p2)
    w2 = g_gate * p2 / (p1 + p2)
    e1 = i1 - N_GROUPS
    e2 = i2 - N_GROUPS
    o_ref[...] = jnp.where(lane == 0, e1, jnp.where(lane == 1, e2,
                           jnp.where(lane == 2, w1, jnp.where(lane == 3, w2, 0.0))))


def _router(x, w_hi, w_lo, bias):
    m, d = x.shape
    tm = _tile(m, 512)
    fixed = lambda i: (0, 0)
    return pl.pallas_call(
        _router_kernel,
        grid=(m // tm,),
        in_specs=[pl.BlockSpec((tm, d), lambda i: (i, 0)), pl.BlockSpec((d, LANES), fixed),
                  pl.BlockSpec((d, LANES), fixed), pl.BlockSpec((1, LANES), fixed)],
        out_specs=pl.BlockSpec((tm, LANES), lambda i: (i, 0)),
        out_shape=jax.ShapeDtypeStruct((m, LANES), F32),
        compiler_params=_cparams("parallel"),
        name="router",
    )(x, w_hi, w_lo, bias)


def _expert_changed(te_ref, t):
    return jnp.logical_or(t == 0, te_ref[t] != te_ref[jnp.maximum(t - 1, 0)])


def _moe_up_kernel(te_ref, nu_ref, x_ref, wg_ref, wu_ref, h_ref, wgb_ref, wub_ref):
    t = pl.program_id(1)

    @pl.when(t < nu_ref[0])
    def _():
        @pl.when(_expert_changed(te_ref, t))
        def _():
            wgb_ref[...] = wg_ref[...].astype(BF16)
            wub_ref[...] = wu_ref[...].astype(BF16)

        x = x_ref[...]
        gate = _dot(x, wgb_ref[...])
        up = _dot(x, wub_ref[...])
        h_ref[...] = (gate * jax.nn.sigmoid(gate) * up).astype(h_ref.dtype)


def _moe_down_kernel(te_ref, nu_ref, h_ref, wd_ref, sw_ref, y_ref, wdb_ref):
    t = pl.program_id(1)

    @pl.when(t < nu_ref[0])
    def _():
        @pl.when(_expert_changed(te_ref, t))
        def _():
            wdb_ref[...] = wd_ref[...].astype(BF16)

        y_ref[...] = _dot(h_ref[...], wdb_ref[...]) * sw_ref[...]


def _moe_experts(xs, slot_w, tile_e, n_used, w_gate, w_up, w_down, layer):
    cap, d = xs.shape
    de = w_gate.shape[-1]
    n_tiles = cap // MOE_TILE
    fa = _tile(de, 512)
    nb = _tile(d, 1024)

    def live(t, nu):
        return jnp.minimum(t, nu[0] - 1)

    hid = pl.pallas_call(
        _moe_up_kernel,
        grid_spec=pltpu.PrefetchScalarGridSpec(
            num_scalar_prefetch=2,
            grid=(de // fa, n_tiles),
            in_specs=[pl.BlockSpec((MOE_TILE, d), lambda f, t, te, nu: (live(t, nu), 0)),
                      pl.BlockSpec((None, None, d, fa), lambda f, t, te, nu: (layer, te[t], 0, f)),
                      pl.BlockSpec((None, None, d, fa), lambda f, t, te, nu: (layer, te[t], 0, f))],
            out_specs=pl.BlockSpec((MOE_TILE, fa), lambda f, t, te, nu: (live(t, nu), f)),
            scratch_shapes=[pltpu.VMEM((d, fa), BF16), pltpu.VMEM((d, fa), BF16)]),
        out_shape=jax.ShapeDtypeStruct((cap, de), BF16),
        compiler_params=_cparams("arbitrary", "arbitrary"),
        name="moe_up",
    )(tile_e, n_used, xs, w_gate, w_up)

    return pl.pallas_call(
        _moe_down_kernel,
        grid_spec=pltpu.PrefetchScalarGridSpec(
            num_scalar_prefetch=2,
            grid=(d // nb, n_tiles),
            in_specs=[pl.BlockSpec((MOE_TILE, de), lambda n, t, te, nu: (live(t, nu), 0)),
                      pl.BlockSpec((None, None, de, nb), lambda n, t, te, nu: (layer, te[t], 0, n)),
                      pl.BlockSpec((MOE_TILE, 1), lambda n, t, te, nu: (live(t, nu), 0))],
            out_specs=pl.BlockSpec((MOE_TILE, nb), lambda n, t, te, nu: (live(t, nu), n)),
            scratch_shapes=[pltpu.VMEM((de, nb), BF16)]),
        out_shape=jax.ShapeDtypeStruct((cap, d), F32),
        compiler_params=_cparams("arbitrary", "arbitrary"),
        name="moe_down",
    )(tile_e, n_used, hid, w_down, slot_w)


def _moe_ln_kernel(x_ref, y0_ref, y1_ref, g_ref, b_ref, o_ref, *, alpha):
    z = alpha * x_ref[...] + (y0_ref[...] + y1_ref[...])
    o_ref[...] = _layer_norm_rows(z, g_ref[...], b_ref[...])


def _moe_ln(x, y0, y1, g, b, alpha):
    m, d = x.shape
    tm = _tile(m, 512)
    row = lambda i: (i, 0)
    fixed = lambda i: (0, 0)
    return pl.pallas_call(
        functools.partial(_moe_ln_kernel, alpha=alpha),
        grid=(m // tm,),
        in_specs=[pl.BlockSpec((tm, d), row), pl.BlockSpec((tm, d), row), pl.BlockSpec((tm, d), row),
                  pl.BlockSpec((1, d), fixed), pl.BlockSpec((1, d), fixed)],
        out_specs=pl.BlockSpec((tm, d), row),
        out_shape=jax.ShapeDtypeStruct((m, d), F32),
        compiler_params=_cparams("parallel"),
        name="moe_ln",
    )(x, y0, y1, g, b)


def _moe_layer(x, layer, w_rg, b_rg, w_re, b_re, w_gate, w_up, w_down, ln_g, ln_b, alpha):
    m, d = x.shape
    wr = jnp.concatenate([w_rg, w_re], axis=1)
    wr = jnp.pad(wr, ((0, 0), (0, LANES - wr.shape[1])))
    wr_hi = wr.astype(BF16)
    wr_lo = (wr - wr_hi.astype(F32)).astype(BF16)
    br = jnp.pad(jnp.concatenate([b_rg, b_re]), (0, LANES - N_GROUPS - N_EXPERTS))[None, :]
    routed = _router(x, wr_hi, wr_lo, br)
    ids = routed[:, 0:TOP_K].astype(jnp.int32)
    wts = routed[:, TOP_K:2 * TOP_K]

    n_assign = m * TOP_K
    n_tiles = -(-n_assign // MOE_TILE) + N_EXPERTS
    cap = n_tiles * MOE_TILE
    flat_e = ids.reshape(-1)
    onehot = (flat_e[:, None] == jnp.arange(N_EXPERTS, dtype=jnp.int32)[None, :]).astype(jnp.int32)
    csum = jnp.cumsum(onehot, axis=0)
    counts = csum[-1]
    rank = jnp.take_along_axis(csum, flat_e[:, None], axis=1)[:, 0] - 1
    tiles_per = (counts + MOE_TILE - 1) // MOE_TILE
    tile_ends = jnp.cumsum(tiles_per)
    pad_starts = (tile_ends - tiles_per) * MOE_TILE
    dest = pad_starts[flat_e] + rank
    n_used = tile_ends[-1:].astype(jnp.int32)
    tile_e = jnp.searchsorted(tile_ends, jnp.arange(n_tiles, dtype=jnp.int32), side='right')
    last_e = jnp.max(jnp.where(counts > 0, jnp.arange(N_EXPERTS), 0))
    tile_e = jnp.minimum(tile_e, last_e).astype(jnp.int32)
    tok = jnp.arange(n_assign, dtype=jnp.int32) // TOP_K
    slot_tok = jnp.zeros((cap,), jnp.int32).at[dest].set(tok)
    slot_w = jnp.zeros((cap,), F32).at[dest].set(wts.reshape(-1))

    xs = x.astype(BF16)[slot_tok]
    y = _moe_experts(xs, slot_w[:, None], tile_e, n_used, w_gate, w_up, w_down, layer)
    dest2 = dest.reshape(m, TOP_K)
    return _moe_ln(x, y[dest2[:, 0]], y[dest2[:, 1]], ln_g[None, :], ln_b[None, :], alpha)


def _rope_tables(pos):
    half = ROPE_DIM // 2
    inv_freq = ROPE_THETA ** (-jnp.arange(half, dtype=F32) / half)
    ang = pos.astype(F32)[:, None] * inv_freq
    cos, sin = jnp.cos(ang), jnp.sin(ang)
    z = jnp.zeros_like(cos)
    pad = jnp.zeros((pos.shape[0], LANES - ROPE_DIM), F32)
    c = jnp.concatenate([cos, cos, pad], axis=1)
    sa = jnp.concatenate([-sin, z, pad], axis=1)
    sb = jnp.concatenate([z, sin, pad], axis=1)
    return c, sa, sb


def _hybrid_layer(x, dims, rope, conv_state, cache_lat, cache_pe, page_table, j,
                  w_in, conv_w, q_g, kv_g, w_q_up, w_uk, w_uv, w_out, ln_g, ln_b, alpha):
    bp, sp, bs, ts = dims
    n_p = bp * sp
    cw = conv_w.shape[1]
    heads = w_q_up.shape[1]
    n_main = 3 * cw + Q_LORA + KV_LORA

    w_main = w_in[:, :n_main].astype(BF16)
    w_pe = jnp.pad(w_in[:, n_main:], ((0, 0), (0, LANES - ROPE_DIM))).astype(BF16)
    h = _mm(x, w_main, name="hyb_in")
    h_pe = _mm(x, w_pe, name="hyb_in_pe")

    a_p, conv_p = _conv_prompt(h, conv_w, bp, sp, cw)
    a_s, conv_s = _conv_sample(h, conv_state, conv_w, n_p, bs, ts, cw)

    w_q_cat = jnp.pad(w_q_up, ((0, 0), (0, 0), (0, HEAD_PAD - NOPE_DIM - ROPE_DIM)))
    w_q_cat = w_q_cat.reshape(Q_LORA, heads * HEAD_PAD).astype(BF16)
    q_cat = _q_proj(h, 3 * cw // Q_LORA, q_g[None, :], w_q_cat, *rope, heads)
    lat, kpe = _kv_norm(h, (3 * cw + Q_LORA) // KV_LORA, h_pe, kv_g[None, :], *rope)
    w_uk2 = w_uk.reshape(KV_LORA, heads * NOPE_DIM).astype(BF16)
    w_uv2 = w_uv.reshape(KV_LORA, heads * V_DIM).astype(BF16)

    k_cat, v = _kv_expand(lat, kpe, w_uk2, w_uv2, n_p, heads)
    o_p = _attn_prompt(q_cat, k_cat, v, bp, sp, heads)

    n_s = bs * ts
    q_lat = _q_latent(q_cat, w_uk2, n_p, n_s, heads)
    q_lat = q_lat.reshape(heads, bs, ts, KV_LORA).transpose(1, 0, 2, 3).reshape(bs, heads * ts, KV_LORA)
    q_pe = q_cat[n_p:].reshape(bs, ts, heads, HEAD_PAD)[..., NOPE_DIM:NOPE_DIM + ROPE_DIM]
    q_pe = q_pe.transpose(0, 2, 1, 3).reshape(bs, heads * ts, ROPE_DIM)
    lat_s = lat[n_p:].reshape(bs, ts, KV_LORA)
    kpe_s = kpe[n_p:, :ROPE_DIM].reshape(bs, ts, ROPE_DIM)
    new_lat = jnp.pad(lat_s, ((0, 0), (0, ts), (0, 0)))
    new_pe = jnp.pad(kpe_s, ((0, 0), (0, ts), (0, 0)))
    o_lat = _attn_sample(page_table, q_lat, q_pe, cache_lat, cache_pe, j, new_lat, new_pe, ts)
    o_lat = o_lat.reshape(bs, heads, ts, KV_LORA).transpose(1, 0, 2, 3).reshape(heads, n_s, KV_LORA)
    o_s = _o_from_latent(o_lat, w_uv2, heads)

    a = jnp.concatenate([a_p, a_s], axis=0)
    o = jnp.concatenate([o_p, o_s], axis=0)
    y = _hyb_out(a, o, w_out.astype(BF16), x, ln_g[None, :], ln_b[None, :], alpha)
    outs = (lat[:n_p].reshape(bp, sp, KV_LORA), kpe[:n_p, :ROPE_DIM].reshape(bp, sp, ROPE_DIM),
            lat_s, kpe_s, conv_p, conv_s)
    return y, outs


def _rec_layer(x, dims, state, j, lb, w_in, norm_g, w_out, ln_g, ln_b, alpha):
    bp, sp, bs, ts = dims
    n_p = bp * sp
    width = w_in.shape[1] // 4
    heads = width // REC_DK
    h = _mm(x, w_in.astype(BF16), name="rec_in")
    lb2 = lb[None, :]
    o_p, st_p = _gla_prompt(h, lb2, bp, sp, heads)
    o_s, st_s = _gla_sample(h, lb2, state, j, n_p, bs, ts, heads)
    o = jnp.concatenate([o_p, o_s], axis=0)
    y = _rec_out(o, h, 3, norm_g[None, :], w_out.astype(BF16), x, ln_g[None, :], ln_b[None, :], alpha)
    return y, (jnp.swapaxes(st_p, -1, -2), st_s)


def kernel(x_prompt, x_sample, cache_kv_latent, cache_k_rope, state_conv, state_hgrn, page_table,
           w_in_hyb, conv_w, q_norm_g, kv_norm_g, w_q_up, w_uk, w_uv, w_out_hyb,
           w_in_rec, lb_logits, rec_norm_g, w_out_rec,
           ln1_g, ln1_b, ln2_g, ln2_b,
           w_router_group, b_router_group, w_router_expert, b_router_expert,
           w_gate, w_up, w_down):
    bp, sp, d = x_prompt.shape
    bs, ts, _ = x_sample.shape
    depth = ln1_g.shape[0]
    past_len = page_table.shape[1] * cache_kv_latent.shape[2]
    dims = (bp, sp, bs, ts)
    n_p = bp * sp
    alpha = (2 * depth) ** 0.25

    lower = jnp.cumsum(jax.nn.softmax(lb_logits.astype(F32), axis=0), axis=0)
    lower = lower - lower[0]

    pos = jnp.concatenate([jnp.tile(jnp.arange(sp), bp), jnp.tile(past_len + jnp.arange(ts), bs)])
    rope = _rope_tables(pos)

    x = jnp.concatenate([x_prompt.reshape(n_p, d), x_sample.reshape(bs * ts, d)], axis=0)
    hyb_outs, rec_outs = [], []
    for layer in range(depth):
        j = layer // 2
        if layer % 2 == 0:
            x, outs = _hybrid_layer(x, dims, rope, state_conv[j], cache_kv_latent, cache_k_rope, page_table, j,
                                    w_in_hyb[j], conv_w[j], q_norm_g[j], kv_norm_g[j], w_q_up[j], w_uk[j],
                                    w_uv[j], w_out_hyb[j], ln1_g[layer], ln1_b[layer], alpha)
            hyb_outs.append(outs)
        else:
            x, outs = _rec_layer(x, dims, state_hgrn, j, lower[layer], w_in_rec[j], rec_norm_g[j],
                                 w_out_rec[j], ln1_g[layer], ln1_b[layer], alpha)
            rec_outs.append(outs)
        x = _moe_layer(x, layer, w_router_group[layer], b_router_group[layer], w_router_expert[layer],
                       b_router_expert[layer], w_gate, w_up, w_down, ln2_g[layer], ln2_b[layer], alpha)

    stack = lambda parts, i: jnp.stack([p[i] for p in parts])
    return (x[:n_p].reshape(bp, sp, d), x[n_p:].reshape(bs, ts, d),
            stack(hyb_outs, 0), stack(hyb_outs, 1), stack(hyb_outs, 2), stack(hyb_outs, 3),
            stack(hyb_outs, 4), stack(hyb_outs, 5), stack(rec_outs, 0), stack(rec_outs, 1))
```

```python
import functools
import math

import numpy as np
import jax
import jax.numpy as jnp
from jax import lax
from jax.experimental import pallas as pl
from jax.experimental.pallas import tpu as pltpu

F32 = jnp.float32
BF16 = jnp.bfloat16

CONV_K = 3
Q_LORA = 512
KV_LORA = 512
NOPE_DIM = 128
ROPE_DIM = 64
V_DIM = 128
REC_DK = 128
N_GROUPS = 4
EXPERTS_PER_GROUP = 8
N_EXPERTS = N_GROUPS * EXPERTS_PER_GROUP
TOP_K = 2
ROPE_THETA = 10000.0
LN_EPS = 1e-5
RMS_EPS = 1e-6
MLA_SCALE = (NOPE_DIM + ROPE_DIM) ** -0.5

LANES = 128
SUBLANES = 8
VMEM_LIMIT_BYTES = 52 * 1024 * 1024

HEAD_PAD = 2 * LANES
GLA_CHUNK = 64
GLA_HEADS_PER_STEP = 4
MOE_TILE = 256
ATTN_TQ = 512
PAGES_PER_STEP = 16


def _cparams(*sem):
    return pltpu.CompilerParams(dimension_semantics=sem, vmem_limit_bytes=VMEM_LIMIT_BYTES)


def _tile(n, pref):
    if n <= pref:
        return n
    for t in range(pref, 7, -1):
        if n % t == 0 and t % 8 == 0:
            return t
    return n


def _dot(a, b):
    return jnp.dot(a, b, preferred_element_type=F32)


def _dot_nt(a, b):
    return lax.dot_general(a, b, (((1,), (1,)), ((), ())), preferred_element_type=F32)


def _dot_tn(a, b):
    return lax.dot_general(a, b, (((0,), (0,)), ((), ())), preferred_element_type=F32)


def _mm_kernel(x_ref, w_ref, o_ref):
    o_ref[...] = _dot(x_ref[...].astype(BF16), w_ref[...].astype(BF16)).astype(o_ref.dtype)


def _mm(x, w, *, layer=None, n=None, tm=1024, tn=512, out_dtype=F32, name="mm"):
    m, k = x.shape
    n = w.shape[-1] if n is None else n
    tm = _tile(m, tm)
    tn = _tile(n, tn)
    if layer is None:
        w_spec = pl.BlockSpec((k, tn), lambda i, j: (0, j))
    else:
        w_spec = pl.BlockSpec((None, k, tn), lambda i, j: (layer, 0, j))
    return pl.pallas_call(
        _mm_kernel,
        grid=(m // tm, n // tn),
        in_specs=[pl.BlockSpec((tm, k), lambda i, j: (i, 0)), w_spec],
        out_specs=pl.BlockSpec((tm, tn), lambda i, j: (i, j)),
        out_shape=jax.ShapeDtypeStruct((m, n), out_dtype),
        compiler_params=_cparams("parallel", "arbitrary"),
        name=name,
    )(x, w)


def _layer_norm_rows(z, g, b):
    mu = jnp.mean(z, axis=-1, keepdims=True)
    zc = z - mu
    var = jnp.mean(zc * zc, axis=-1, keepdims=True)
    return zc * lax.rsqrt(var + LN_EPS) * g + b


def _rms_norm_rows(z, g):
    return z * lax.rsqrt(jnp.mean(z * z, axis=-1, keepdims=True) + RMS_EPS) * g


def _conv_prompt_kernel(cx_ref, gb_ref, gc_ref, cxp_ref, gcp_ref, w_ref, a_ref, st_ref, *, tt):
    i = pl.program_id(1)
    u = gc_ref[...] * cx_ref[...]
    prev = jnp.where(i > 0, gcp_ref[...] * cxp_ref[...], 0.0)
    row = lax.broadcasted_iota(jnp.int32, u.shape, 0)
    u1 = jnp.where(row == 0, prev[7:8], pltpu.roll(u, 1, 0))
    u2 = jnp.where(row == 0, prev[6:7], jnp.where(row == 1, prev[7:8], pltpu.roll(u, 2, 0)))
    v = w_ref[0:1, :] * u2 + w_ref[1:2, :] * u1 + w_ref[2:3, :] * u
    a_ref[...] = gb_ref[...] * v

    @pl.when(i == pl.num_programs(1) - 1)
    def _():
        st_ref[...] = u[tt - (CONV_K - 1):tt]


def _conv_prompt(h, conv_w, bsz, seq, cw):
    tt = _tile(seq, 512)
    nt = seq // tt
    sub = tt // SUBLANES

    def main(c):
        return pl.BlockSpec((tt, cw), lambda b, i: (b * nt + i, c))

    def halo(c):
        return pl.BlockSpec((SUBLANES, cw), lambda b, i: (jnp.maximum((b * nt + i) * sub - 1, 0), c))

    return pl.pallas_call(
        functools.partial(_conv_prompt_kernel, tt=tt),
        grid=(bsz, nt),
        in_specs=[main(0), main(1), main(2), halo(0), halo(2),
                  pl.BlockSpec((CONV_K, cw), lambda b, i: (0, 0))],
        out_specs=[pl.BlockSpec((tt, cw), lambda b, i: (b * nt + i, 0)),
                   pl.BlockSpec((None, CONV_K - 1, cw), lambda b, i: (b, 0, 0))],
        out_shape=[jax.ShapeDtypeStruct((bsz * seq, cw), F32),
                   jax.ShapeDtypeStruct((bsz, CONV_K - 1, cw), F32)],
        compiler_params=_cparams("parallel", "arbitrary"),
        name="conv_prompt",
    )(h, h, h, h, h, conv_w)


def _conv_sample_kernel(cx_ref, gb_ref, gc_ref, st_ref, w_ref, a_ref, sto_ref, *, nb, ts):
    cw = cx_ref.shape[-1]
    u = (gc_ref[...] * cx_ref[...]).reshape(nb, ts, cw)
    st = st_ref[...]
    t = lax.broadcasted_iota(jnp.int32, u.shape, 1)
    s0 = st[:, 0:1, :]
    s1 = st[:, 1:2, :]
    u1 = jnp.where(t == 0, s1, pltpu.roll(u, 1, 1))
    u2 = jnp.where(t == 0, s0, jnp.where(t == 1, s1, pltpu.roll(u, 2, 1)))
    w = w_ref[...]
    v = w[0:1, :][None] * u2 + w[1:2, :][None] * u1 + w[2:3, :][None] * u
    a_ref[...] = gb_ref[...] * v.reshape(nb * ts, cw)
    sto_ref[...] = u[:, ts - (CONV_K - 1):ts, :]


def _conv_sample(h, state, conv_w, row0, bsz, ts, cw):
    nb = _tile(bsz, 64)
    rows = nb * ts
    off = row0 // rows

    def main(c):
        return pl.BlockSpec((rows, cw), lambda i: (off + i, c))

    return pl.pallas_call(
        functools.partial(_conv_sample_kernel, nb=nb, ts=ts),
        grid=(bsz // nb,),
        in_specs=[main(0), main(1), main(2),
                  pl.BlockSpec((nb, CONV_K - 1, cw), lambda i: (i, 0, 0)),
                  pl.BlockSpec((CONV_K, cw), lambda i: (0, 0))],
        out_specs=[pl.BlockSpec((rows, cw), lambda i: (i, 0)),
                   pl.BlockSpec((nb, CONV_K - 1, cw), lambda i: (i, 0, 0))],
        out_shape=[jax.ShapeDtypeStruct((bsz * ts, cw), F32),
                   jax.ShapeDtypeStruct((bsz, CONV_K - 1, cw), F32)],
        compiler_params=_cparams("parallel"),
        name="conv_sample",
    )(h, h, h, state, conv_w)


def _rope_lanes(x, c, sa, sb):
    half = ROPE_DIM // 2
    return x * c + pltpu.roll(x, LANES - half, 1) * sa + pltpu.roll(x, half, 1) * sb


def _q_proj_kernel(qc_ref, g_ref, w_ref, c_ref, sa_ref, sb_ref, o_ref, *, heads):
    xn = _rms_norm_rows(qc_ref[...], g_ref[...]).astype(BF16)
    q = _dot(xn, w_ref[...]) * MLA_SCALE
    c, sa, sb = c_ref[...], sa_ref[...], sb_ref[...]
    for hd in range(heads):
        lo = hd * HEAD_PAD
        o_ref[:, lo:lo + LANES] = q[:, lo:lo + LANES].astype(o_ref.dtype)
        pe = _rope_lanes(q[:, lo + LANES:lo + HEAD_PAD], c, sa, sb)
        o_ref[:, lo + LANES:lo + HEAD_PAD] = pe.astype(o_ref.dtype)


def _q_proj(h, col_block, g, w_cat, rope_c, rope_sa, rope_sb, heads):
    m = h.shape[0]
    tm = _tile(m, 512)
    n = heads * HEAD_PAD
    row = lambda i: (i, 0)
    return pl.pallas_call(
        functools.partial(_q_proj_kernel, heads=heads),
        grid=(m // tm,),
        in_specs=[pl.BlockSpec((tm, Q_LORA), lambda i: (i, col_block)),
                  pl.BlockSpec((1, Q_LORA), lambda i: (0, 0)),
                  pl.BlockSpec((Q_LORA, n), lambda i: (0, 0)),
                  pl.BlockSpec((tm, LANES), row), pl.BlockSpec((tm, LANES), row),
                  pl.BlockSpec((tm, LANES), row)],
        out_specs=pl.BlockSpec((tm, n), row),
        out_shape=jax.ShapeDtypeStruct((m, n), BF16),
        compiler_params=_cparams("parallel"),
        name="q_proj",
    )(h, g, w_cat, rope_c, rope_sa, rope_sb)


def _kv_norm_kernel(kv_ref, pe_ref, g_ref, c_ref, sa_ref, sb_ref, lat_ref, kpe_ref):
    lat_ref[...] = _rms_norm_rows(kv_ref[...], g_ref[...])
    kpe_ref[...] = _rope_lanes(pe_ref[...], c_ref[...], sa_ref[...], sb_ref[...])


def _kv_norm(h, col_block, h_pe, g, rope_c, rope_sa, rope_sb):
    m = h.shape[0]
    tm = _tile(m, 1024)
    row = lambda i: (i, 0)
    return pl.pallas_call(
        _kv_norm_kernel,
        grid=(m // tm,),
        in_specs=[pl.BlockSpec((tm, KV_LORA), lambda i: (i, col_block)),
                  pl.BlockSpec((tm, LANES), row),
                  pl.BlockSpec((1, KV_LORA), lambda i: (0, 0)),
                  pl.BlockSpec((tm, LANES), row), pl.BlockSpec((tm, LANES), row),
                  pl.BlockSpec((tm, LANES), row)],
        out_specs=[pl.BlockSpec((tm, KV_LORA), row), pl.BlockSpec((tm, LANES), row)],
        out_shape=[jax.ShapeDtypeStruct((m, KV_LORA), F32),
                   jax.ShapeDtypeStruct((m, LANES), F32)],
        compiler_params=_cparams("parallel"),
        name="kv_norm",
    )(h, h_pe, g, rope_c, rope_sa, rope_sb)


def _kv_expand_kernel(lat_ref, kpe_ref, wk_ref, wvt_ref, k_ref, vt_ref, *, heads):
    lat = lat_ref[...].astype(BF16)
    kn = _dot(lat, wk_ref[...])
    kpe = kpe_ref[...].astype(k_ref.dtype)
    for hd in range(heads):
        lo = hd * HEAD_PAD
        k_ref[:, lo:lo + LANES] = kn[:, hd * NOPE_DIM:(hd + 1) * NOPE_DIM].astype(k_ref.dtype)
        k_ref[:, lo + LANES:lo + HEAD_PAD] = kpe
    vt_ref[...] = _dot_nt(wvt_ref[...], lat).astype(vt_ref.dtype)


def _kv_expand(lat, kpe, w_uk, w_uv_t, rows, heads):
    tm = _tile(rows, 512)
    row = lambda i: (i, 0)
    fixed = lambda i: (0, 0)
    return pl.pallas_call(
        functools.partial(_kv_expand_kernel, heads=heads),
        grid=(rows // tm,),
        in_specs=[pl.BlockSpec((tm, KV_LORA), row), pl.BlockSpec((tm, LANES), row),
                  pl.BlockSpec((KV_LORA, heads * NOPE_DIM), fixed),
                  pl.BlockSpec((heads * V_DIM, KV_LORA), fixed)],
        out_specs=[pl.BlockSpec((tm, heads * HEAD_PAD), row),
                   pl.BlockSpec((heads * V_DIM, tm), lambda i: (0, i))],
        out_shape=[jax.ShapeDtypeStruct((rows, heads * HEAD_PAD), BF16),
                   jax.ShapeDtypeStruct((heads * V_DIM, rows), BF16)],
        compiler_params=_cparams("parallel"),
        name="kv_expand",
    )(lat, kpe, w_uk, w_uv_t)


def _attn_prompt_kernel(q_ref, k_ref, vt_ref, o_ref, m_ref, l_ref, acc_ref, *, tq):
    i = pl.program_id(2)
    m_ref[...] = jnp.full(m_ref.shape, -jnp.inf, F32)
    l_ref[...] = jnp.zeros(l_ref.shape, F32)
    acc_ref[...] = jnp.zeros(acc_ref.shape, F32)
    q = q_ref[...]

    def chunk(c, masked):
        r0 = pl.multiple_of(c * tq, tq)
        s = _dot_nt(k_ref[pl.ds(r0, tq), :], q)
        if masked:
            key = lax.broadcasted_iota(jnp.int32, s.shape, 0)
            qry = lax.broadcasted_iota(jnp.int32, s.shape, 1)
            s = jnp.where(key <= qry, s, -jnp.inf)
        m_old = m_ref[...]
        m_new = jnp.maximum(m_old, jnp.max(s, axis=0, keepdims=True))
        alpha = jnp.exp(m_old - m_new)
        p = jnp.exp(s - m_new)
        l_ref[...] = alpha * l_ref[...] + jnp.sum(p, axis=0, keepdims=True)
        acc_ref[...] = alpha * acc_ref[...] + _dot(vt_ref[:, pl.ds(r0, tq)], p.astype(BF16))
        m_ref[...] = m_new

    def body(c, carry):
        chunk(c, False)
        return carry

    lax.fori_loop(0, i, body, 0)
    chunk(i, True)
    o_ref[...] = (acc_ref[...] / l_ref[...]).T.astype(o_ref.dtype)


def _attn_prompt(q_cat, k_cat, v_t, bsz, seq, heads):
    tq = _tile(seq, ATTN_TQ)
    nq = seq // tq
    return pl.pallas_call(
        functools.partial(_attn_prompt_kernel, tq=tq),
        grid=(bsz, heads, nq),
        in_specs=[pl.BlockSpec((tq, HEAD_PAD), lambda b, h, i: (b * nq + i, h)),
                  pl.BlockSpec((seq, HEAD_PAD), lambda b, h, i: (b, h)),
                  pl.BlockSpec((V_DIM, seq), lambda b, h, i: (h, b))],
        out_specs=pl.BlockSpec((tq, V_DIM), lambda b, h, i: (b * nq + i, h)),
        out_shape=jax.ShapeDtypeStruct((bsz * seq, heads * V_DIM), F32),
        scratch_shapes=[pltpu.VMEM((1, tq), F32), pltpu.VMEM((1, tq), F32),
                        pltpu.VMEM((V_DIM, tq), F32)],
        compiler_params=_cparams("parallel", "parallel", "arbitrary"),
        name="attn_prompt",
    )(q_cat, k_cat, v_t)


def _head_mm_kernel(x_ref, w_ref, o_ref, *, nt):
    x = x_ref[...].astype(BF16)
    o_ref[...] = (_dot_nt(x, w_ref[...]) if nt else _dot(x, w_ref[...])).astype(o_ref.dtype)


def _q_latent(q_cat, w_uk, row0, rows, heads):
    off = row0 // rows
    return pl.pallas_call(
        functools.partial(_head_mm_kernel, nt=True),
        grid=(heads,),
        in_specs=[pl.BlockSpec((rows, NOPE_DIM), lambda h: (off, 2 * h)),
                  pl.BlockSpec((KV_LORA, NOPE_DIM), lambda h: (0, h))],
        out_specs=pl.BlockSpec((None, rows, KV_LORA), lambda h: (h, 0, 0)),
        out_shape=jax.ShapeDtypeStruct((heads, rows, KV_LORA), BF16),
        compiler_params=_cparams("parallel"),
        name="q_latent",
    )(q_cat, w_uk)


def _o_from_latent(o_lat, w_uv, heads):
    rows = o_lat.shape[1]
    return pl.pallas_call(
        functools.partial(_head_mm_kernel, nt=False),
        grid=(heads,),
        in_specs=[pl.BlockSpec((None, rows, KV_LORA), lambda h: (h, 0, 0)),
                  pl.BlockSpec((KV_LORA, V_DIM), lambda h: (0, h))],
        out_specs=pl.BlockSpec((rows, V_DIM), lambda h: (0, h)),
        out_shape=jax.ShapeDtypeStruct((rows, heads * V_DIM), F32),
        compiler_params=_cparams("parallel"),
        name="o_from_latent",
    )(o_lat, w_uv)


def _lane_chunks(s):
    w = s.shape[-1]
    if w % LANES or w == LANES:
        return [s]
    return [s[:, c * LANES:(c + 1) * LANES] for c in range(w // LANES)]


def _row_max(s):
    parts = _lane_chunks(s)
    return jnp.max(functools.reduce(jnp.maximum, parts), axis=-1, keepdims=True)


def _row_sum(s):
    parts = _lane_chunks(s)
    return jnp.sum(functools.reduce(jnp.add, parts), axis=-1, keepdims=True)


def _attn_sample_kernel(pt_ref, ql_ref, qp_ref, *refs, npg, ts, page):
    lat_pages = refs[:npg]
    pe_pages = refs[npg:2 * npg]
    nl_ref, np_ref, o_ref, k_ref, pe_ref, m_ref, l_ref, acc_ref = refs[2 * npg:]
    s_idx = pl.program_id(1)

    @pl.when(s_idx == 0)
    def _():
        m_ref[...] = jnp.full(m_ref.shape, -jnp.inf, F32)
        l_ref[...] = jnp.zeros(l_ref.shape, F32)
        acc_ref[...] = jnp.zeros(acc_ref.shape, F32)

    ql = ql_ref[...]
    qp = qp_ref[...]

    def update(s, values):
        m_old = m_ref[...]
        m_new = jnp.maximum(m_old, _row_max(s))
        alpha = jnp.exp(m_old - m_new)
        p = jnp.exp(s - m_new)
        l_ref[...] = alpha * l_ref[...] + _row_sum(p)
        acc_ref[...] = alpha * acc_ref[...] + _dot(p.astype(BF16), values)
        m_ref[...] = m_new

    for j in range(npg):
        k_ref[j * page:(j + 1) * page, :] = lat_pages[j][...].astype(BF16)
        pe_ref[:, j * page:(j + 1) * page] = pe_pages[j][...].astype(BF16)
    keys = k_ref[...]
    update(_dot_nt(ql, keys) + _dot(qp, pe_ref[...]), keys)

    @pl.when(s_idx == pl.num_programs(1) - 1)
    def _():
        kn = nl_ref[...].astype(BF16)
        s = _dot_nt(ql, kn) + _dot_nt(qp, np_ref[...].astype(BF16))
        t_q = lax.broadcasted_iota(jnp.int32, s.shape, 0) % ts
        t_k = lax.broadcasted_iota(jnp.int32, s.shape, 1)
        update(jnp.where(t_k <= t_q, s, -jnp.inf), kn)
        o_ref[...] = (acc_ref[...] / l_ref[...]).astype(o_ref.dtype)


def _attn_sample(page_table, q_lat, q_pe, cache_lat, cache_pe_t, layer, new_lat, new_pe, ts):
    bsz, qrows, _ = q_lat.shape
    n_pages = page_table.shape[1]
    page = cache_lat.shape[2]
    npg = math.gcd(PAGES_PER_STEP, n_pages)
    steps = n_pages // npg
    pt_flat = page_table.reshape(-1)

    def page_spec(shape, j):
        return pl.BlockSpec((None, None) + shape,
                            lambda b, s, pt: (layer, pt[b * n_pages + s * npg + j], 0, 0))

    per_b = lambda b, s, pt: (b, 0, 0)
    grid_spec = pltpu.PrefetchScalarGridSpec(
        num_scalar_prefetch=1,
        grid=(bsz, steps),
        in_specs=([pl.BlockSpec((None, qrows, KV_LORA), per_b),
                   pl.BlockSpec((None, qrows, ROPE_DIM), per_b)]
                  + [page_spec((page, KV_LORA), j) for j in range(npg)]
                  + [page_spec((ROPE_DIM, page), j) for j in range(npg)]
                  + [pl.BlockSpec((None, 2 * ts, KV_LORA), per_b),
                     pl.BlockSpec((None, 2 * ts, ROPE_DIM), per_b)]),
        out_specs=pl.BlockSpec((None, qrows, KV_LORA), per_b),
        scratch_shapes=[pltpu.VMEM((npg * page, KV_LORA), BF16),
                        pltpu.VMEM((ROPE_DIM, npg * page), BF16),
                        pltpu.VMEM((qrows, 1), F32), pltpu.VMEM((qrows, 1), F32),
                        pltpu.VMEM((qrows, KV_LORA), F32)],
    )
    return pl.pallas_call(
        functools.partial(_attn_sample_kernel, npg=npg, ts=ts, page=page),
        grid_spec=grid_spec,
        out_shape=jax.ShapeDtypeStruct((bsz, qrows, KV_LORA), BF16),
        compiler_params=_cparams("parallel", "arbitrary"),
        name="attn_sample",
    )(pt_flat, q_lat, q_pe, *([cache_lat] * npg), *([cache_pe_t] * npg), new_lat, new_pe)


def _hyb_out_kernel(a_ref, o_ref, w_ref, x_ref, g_ref, b_ref, y_ref, *, alpha):
    ka = a_ref.shape[1]
    mix = _dot(a_ref[...].astype(BF16), w_ref[0:ka, :]) + _dot(o_ref[...].astype(BF16), w_ref[ka:, :])
    y_ref[...] = _layer_norm_rows(alpha * x_ref[...] + mix, g_ref[...], b_ref[...])


def _hyb_out(a, o, w_out, x, g, b, alpha):
    m, d = x.shape
    tm = _tile(m, 256)
    row = lambda i: (i, 0)
    fixed = lambda i: (0, 0)
    return pl.pallas_call(
        functools.partial(_hyb_out_kernel, alpha=alpha),
        grid=(m // tm,),
        in_specs=[pl.BlockSpec((tm, a.shape[1]), row), pl.BlockSpec((tm, o.shape[1]), row),
                  pl.BlockSpec(w_out.shape, fixed), pl.BlockSpec((tm, d), row),
                  pl.BlockSpec((1, d), fixed), pl.BlockSpec((1, d), fixed)],
        out_specs=pl.BlockSpec((tm, d), row),
        out_shape=jax.ShapeDtypeStruct((m, d), F32),
        compiler_params=_cparams("parallel"),
        name="hyb_out",
    )(a, o, w_out, x, g, b)


def _rec_out_kernel(o_ref, gate_ref, ng_ref, w_ref, x_ref, g_ref, b_ref, y_ref, *, alpha):
    gate = gate_ref[...]
    on = _rms_norm_rows(o_ref[...], ng_ref[...]) * (gate * jax.nn.sigmoid(gate))
    mix = _dot(on.astype(BF16), w_ref[...])
    y_ref[...] = _layer_norm_rows(alpha * x_ref[...] + mix, g_ref[...], b_ref[...])


def _rec_out(o, h, gate_col_block, norm_g, w_out, x, g, b, alpha):
    m, d = x.shape
    vw = o.shape[1]
    tm = _tile(m, 256)
    row = lambda i: (i, 0)
    fixed = lambda i: (0, 0)
    return pl.pallas_call(
        functools.partial(_rec_out_kernel, alpha=alpha),
        grid=(m // tm,),
        in_specs=[pl.BlockSpec((tm, vw), row),
                  pl.BlockSpec((tm, vw), lambda i: (i, gate_col_block)),
                  pl.BlockSpec((1, vw), fixed), pl.BlockSpec(w_out.shape, fixed),
                  pl.BlockSpec((tm, d), row), pl.BlockSpec((1, d), fixed),
                  pl.BlockSpec((1, d), fixed)],
        out_specs=pl.BlockSpec((tm, d), row),
        out_shape=jax.ShapeDtypeStruct((m, d), F32),
        compiler_params=_cparams("parallel"),
        name="rec_out",
    )(o, h, norm_g, w_out, x, g, b)


def _gla_levels(chunk):
    m = chunk // 2
    out = []
    while m >= 1:
        out.append(m)
        m //= 2
    return tuple(out)


def _gla_prefix_matrix(chunk):
    t = np.arange(chunk)
    j = np.arange(chunk)
    blocks = [(j[None, :] <= t[:, None])]
    for m in _gla_levels(chunk):
        ref = (t // (2 * m)) * (2 * m) + m
        blocks.append(j[None, :] <= ref[:, None])
    blocks.append(np.ones((chunk, chunk), bool))
    return np.concatenate(blocks, 0).astype(np.float32)


def _gla_gates(q_raw, f_raw, lb):
    q = q_raw * jax.nn.sigmoid(q_raw) * (REC_DK ** -0.5)
    forget = lb + (1.0 - lb) * jax.nn.sigmoid(f_raw)
    return q, 1.0 - forget, jnp.log(forget)


def _split3(x):
    hi = x.astype(BF16)
    r1 = x - hi.astype(F32)
    mid = r1.astype(BF16)
    lo = (r1 - mid.astype(F32)).astype(BF16)
    return hi, mid, lo


def _gla_prompt_kernel(q_ref, f_ref, v_ref, lb_ref, pm_ref, o_ref, st_ref, s_ref, *, tt, gh):
    c = GLA_CHUNK
    dk = REC_DK
    i = pl.program_id(2)

    @pl.when(i == 0)
    def _():
        s_ref[...] = jnp.zeros(s_ref.shape, F32)

    lb = lb_ref[...]
    pm = pm_ref[...]
    levels = _gla_levels(c)
    row = lax.broadcasted_iota(jnp.int32, (c, dk), 0)
    ti = lax.broadcasted_iota(jnp.int32, (c, c), 0)
    si = lax.broadcasted_iota(jnp.int32, (c, c), 1)

    def body(ci, carry):
        r0 = pl.multiple_of(ci * c, c)
        q_all, k_all, g_all = _gla_gates(q_ref[pl.ds(r0, c), :], f_ref[pl.ds(r0, c), :], lb)
        v_all = v_ref[pl.ds(r0, c), :].astype(BF16)
        hi, mid, lo = _split3(g_all)
        pre = _dot(pm, hi) + _dot(pm, mid) + _dot(pm, lo)
        outs, states = [], []
        for hd in range(gh):
            sl = slice(hd * dk, (hd + 1) * dk)
            q, k, v = q_all[:, sl], k_all[:, sl], v_all[:, sl]
            cum = pre[0:c, sl]
            last = pre[(len(levels) + 1) * c:(len(levels) + 2) * c, sl]
            att = jnp.where(ti == si, _dot_nt(q.astype(BF16), k.astype(BF16)), 0.0)
            for li, m in enumerate(levels):
                ref = pre[(li + 1) * c:(li + 2) * c, sl]
                e = jnp.exp(-jnp.abs(cum - ref))
                later = (row // m) % 2 == 1
                ql = jnp.where(later, q * e, 0.0).astype(BF16)
                kl = jnp.where(later, 0.0, k * e).astype(BF16)
                att = att + jnp.where(ti // (2 * m) == si // (2 * m), _dot_nt(ql, kl), 0.0)
            s_t = s_ref[hd * dk:(hd + 1) * dk, :]
            inter = _dot_nt((q * jnp.exp(cum)).astype(BF16), s_t.astype(BF16))
            outs.append(inter + _dot(att.astype(BF16), v))
            kd = (k * jnp.exp(last - cum)).astype(BF16)
            states.append(s_t * jnp.exp(last[0:1, :]) + _dot_tn(v, kd))
        o_ref[pl.ds(r0, c), :] = jnp.concatenate(outs, axis=1)
        s_ref[...] = jnp.concatenate(states, axis=0)
        return carry

    lax.fori_loop(0, tt // c, body, 0)

    @pl.when(i == pl.num_programs(2) - 1)
    def _():
        for hd in range(gh):
            st_ref[hd] = s_ref[hd * dk:(hd + 1) * dk, :]


def _gla_prompt(h, lb, bsz, seq, heads):
    tt = _tile(seq, 512)
    assert tt % GLA_CHUNK == 0
    nt = seq // tt
    gh = math.gcd(GLA_HEADS_PER_STEP, heads)
    ng = heads // gh
    wide = gh * REC_DK
    pm = jnp.asarray(_gla_prefix_matrix(GLA_CHUNK), BF16)

    def col(base):
        return pl.BlockSpec((tt, wide), lambda b, hg, i: (b * nt + i, base + hg))

    return pl.pallas_call(
        functools.partial(_gla_prompt_kernel, tt=tt, gh=gh),
        grid=(bsz, ng, nt),
        in_specs=[col(0), col(ng), col(2 * ng),
                  pl.BlockSpec((1, wide), lambda b, hg, i: (0, hg)),
                  pl.BlockSpec(pm.shape, lambda b, hg, i: (0, 0))],
        out_specs=[pl.BlockSpec((tt, wide), lambda b, hg, i: (b * nt + i, hg)),
                   pl.BlockSpec((None, gh, REC_DK, REC_DK), lambda b, hg, i: (b, hg, 0, 0))],
        out_shape=[jax.ShapeDtypeStruct((bsz * seq, heads * REC_DK), F32),
                   jax.ShapeDtypeStruct((bsz, heads, REC_DK, REC_DK), F32)],
        scratch_shapes=[pltpu.VMEM((wide, REC_DK), F32)],
        compiler_params=_cparams("parallel", "parallel", "arbitrary"),
        name="gla_prompt",
    )(h, h, h, lb, pm)


def _gla_sample_kernel(q_ref, f_ref, v_ref, lb_ref, hs_ref, he_ref, st_ref, o_ref, sto_ref, *, ts, heads):
    dk = REC_DK
    q, k, g = _gla_gates(q_ref[...], f_ref[...], lb_ref[...])
    v = v_ref[...]
    t = lax.broadcasted_iota(jnp.int32, q.shape, 0)
    cum = g
    sh = 1
    while sh < ts:
        cum = cum + jnp.where(t >= sh, pltpu.roll(cum, sh, 0), 0.0)
        sh *= 2
    last = cum[ts - 1:ts, :]
    xs = []
    for s in range(ts):
        dec = jnp.exp(jnp.where(t >= s, cum - cum[s:s + 1, :], -jnp.inf))
        xs.append(q * k[s:s + 1, :] * dec)
    x = jnp.concatenate(xs, axis=0)
    x_hi = x.astype(BF16)
    x_lo = (x - x_hi.astype(F32)).astype(BF16)
    att = _dot(x_hi, hs_ref[...]) + _dot(x_lo, hs_ref[...])
    att_e = _dot(att.astype(BF16), he_ref[...])
    intra = jnp.zeros(q.shape, F32)
    for s in range(ts):
        intra = intra + att_e[s * ts:(s + 1) * ts, :] * v[s:s + 1, :]
    qe = (q * jnp.exp(cum)).astype(BF16)
    kd = (k * jnp.exp(last - cum)).astype(BF16)
    dec_last = jnp.exp(last)
    eye = (lax.broadcasted_iota(jnp.int32, (dk, dk), 0)
           == lax.broadcasted_iota(jnp.int32, (dk, dk), 1))
    vb = v.astype(BF16)
    for hd in range(heads):
        sl = slice(hd * dk, (hd + 1) * dk)
        s_h = st_ref[hd]
        o_ref[:, sl] = _dot(qe[:, sl], s_h.astype(BF16)) + intra[:, sl]
        d_col = jnp.sum(jnp.where(eye, dec_last[:, sl], 0.0), axis=1, keepdims=True)
        sto_ref[hd] = s_h * d_col + _dot_tn(kd[:, sl], vb[:, sl])


def _gla_sample(h, lb, state, j, row0, bsz, ts, heads):
    width = heads * REC_DK
    off = row0 // ts
    head_of = np.arange(width) // REC_DK
    hsum = (head_of[:, None] == np.arange(LANES)[None, :]).astype(np.float32)
    hs = jnp.asarray(hsum, BF16)
    he = jnp.asarray(hsum.T, BF16)

    def col(cb):
        return pl.BlockSpec((ts, width), lambda b: (off + b, cb))

    fixed = lambda b: (0, 0)
    st_in = pl.BlockSpec((None, None, heads, REC_DK, REC_DK), lambda b: (j, b, 0, 0, 0))
    st_out = pl.BlockSpec((None, heads, REC_DK, REC_DK), lambda b: (b, 0, 0, 0))
    return pl.pallas_call(
        functools.partial(_gla_sample_kernel, ts=ts, heads=heads),
        grid=(bsz,),
        in_specs=[col(0), col(1), col(2), pl.BlockSpec((1, width), fixed),
                  pl.BlockSpec(hs.shape, fixed), pl.BlockSpec(he.shape, fixed), st_in],
        out_specs=[pl.BlockSpec((ts, width), lambda b: (b, 0)), st_out],
        out_shape=[jax.ShapeDtypeStruct((bsz * ts, width), F32),
                   jax.ShapeDtypeStruct(state.shape[1:], F32)],
        compiler_params=_cparams("parallel"),
        name="gla_sample",
    )(h, h, h, lb, hs, he, state)


def _router_kernel(x_ref, wh_ref, wl_ref, b_ref, o_ref):
    x = x_ref[...]
    x_hi = x.astype(BF16)
    x_lo = (x - x_hi.astype(F32)).astype(BF16)
    wh = wh_ref[...]
    logits = _dot(x_hi, wh) + _dot(x_lo, wh) + _dot(x_hi, wl_ref[...]) + b_ref[...]
    lane = lax.broadcasted_iota(jnp.int32, logits.shape, 1).astype(F32)
    big = float(LANES)
    neg = -jnp.inf
    gl = jnp.where(lane < N_GROUPS, logits, neg)
    g_max = jnp.max(gl, axis=-1, keepdims=True)
    g_sel = jnp.min(jnp.where(gl == g_max, lane, big), axis=-1, keepdims=True)
    g_gate = 1.0 / jnp.sum(jnp.exp(gl - g_max), axis=-1, keepdims=True)
    lo = N_GROUPS + g_sel * EXPERTS_PER_GROUP
    el = jnp.where((lane >= lo) & (lane < lo + EXPERTS_PER_GROUP), logits, neg)
    e_max = jnp.max(el, axis=-1, keepdims=True)
    i1 = jnp.min(jnp.where(el == e_max, lane, big), axis=-1, keepdims=True)
    denom = jnp.sum(jnp.exp(el - e_max), axis=-1, keepdims=True)
    el2 = jnp.where(lane == i1, neg, el)
    m2 = jnp.max(el2, axis=-1, keepdims=True)
    i2 = jnp.min(jnp.where(el2 == m2, lane, big), axis=-1, keepdims=True)
    p1 = 1.0 / denom
    p2 = jnp.exp(m2 - e_max) / denom
    w1 = g_gate * p1 / (p1 + p2)
    w2 = g_gate * p2 / (p1 + p2)
    e1 = i1 - N_GROUPS
    e2 = i2 - N_GROUPS
    o_ref[...] = jnp.where(lane == 0, e1, jnp.where(lane == 1, e2,
                           jnp.where(lane == 2, w1, jnp.where(lane == 3, w2, 0.0))))


def _router(x, w_hi, w_lo, bias):
    m, d = x.shape
    tm = _tile(m, 512)
    fixed = lambda i: (0, 0)
    return pl.pallas_call(
        _router_kernel,
        grid=(m // tm,),
        in_specs=[pl.BlockSpec((tm, d), lambda i: (i, 0)), pl.BlockSpec((d, LANES), fixed),
                  pl.BlockSpec((d, LANES), fixed), pl.BlockSpec((1, LANES), fixed)],
        out_specs=pl.BlockSpec((tm, LANES), lambda i: (i, 0)),
        out_shape=jax.ShapeDtypeStruct((m, LANES), F32),
        compiler_params=_cparams("parallel"),
        name="router",
    )(x, w_hi, w_lo, bias)


def _expert_changed(te_ref, t):
    return jnp.logical_or(t == 0, te_ref[t] != te_ref[jnp.maximum(t - 1, 0)])


def _moe_up_kernel(te_ref, nu_ref, x_ref, wg_ref, wu_ref, h_ref, wgb_ref, wub_ref):
    t = pl.program_id(1)

    @pl.when(t < nu_ref[0])
    def _():
        @pl.when(_expert_changed(te_ref, t))
        def _():
            wgb_ref[...] = wg_ref[...].astype(BF16)
            wub_ref[...] = wu_ref[...].astype(BF16)

        x = x_ref[...].astype(BF16)
        gate = _dot(x, wgb_ref[...])
        up = _dot(x, wub_ref[...])
        h_ref[...] = (gate * jax.nn.sigmoid(gate) * up).astype(h_ref.dtype)


def _moe_down_kernel(te_ref, nu_ref, h_ref, wd_ref, y_ref, wdb_ref):
    t = pl.program_id(1)

    @pl.when(t < nu_ref[0])
    def _():
        @pl.when(_expert_changed(te_ref, t))
        def _():
            wdb_ref[...] = wd_ref[...].astype(BF16)

        y_ref[...] = _dot(h_ref[...], wdb_ref[...])


def _moe_experts(xs, tile_e, n_used, w_gate, w_up, w_down, layer):
    cap, d = xs.shape
    de = w_gate.shape[-1]
    n_tiles = cap // MOE_TILE
    fa = _tile(de, 512)
    nb = _tile(d, 1024)

    def live(t, nu):
        return jnp.minimum(t, nu[0] - 1)

    hid = pl.pallas_call(
        _moe_up_kernel,
        grid_spec=pltpu.PrefetchScalarGridSpec(
            num_scalar_prefetch=2,
            grid=(de // fa, n_tiles),
            in_specs=[pl.BlockSpec((MOE_TILE, d), lambda f, t, te, nu: (live(t, nu), 0)),
                      pl.BlockSpec((None, None, d, fa), lambda f, t, te, nu: (layer, te[t], 0, f)),
                      pl.BlockSpec((None, None, d, fa), lambda f, t, te, nu: (layer, te[t], 0, f))],
            out_specs=pl.BlockSpec((MOE_TILE, fa), lambda f, t, te, nu: (live(t, nu), f)),
            scratch_shapes=[pltpu.VMEM((d, fa), BF16), pltpu.VMEM((d, fa), BF16)]),
        out_shape=jax.ShapeDtypeStruct((cap, de), BF16),
        compiler_params=_cparams("arbitrary", "arbitrary"),
        name="moe_up",
    )(tile_e, n_used, xs, w_gate, w_up)

    return pl.pallas_call(
        _moe_down_kernel,
        grid_spec=pltpu.PrefetchScalarGridSpec(
            num_scalar_prefetch=2,
            grid=(d // nb, n_tiles),
            in_specs=[pl.BlockSpec((MOE_TILE, de), lambda n, t, te, nu: (live(t, nu), 0)),
                      pl.BlockSpec((None, None, de, nb), lambda n, t, te, nu: (layer, te[t], 0, n))],
            out_specs=pl.BlockSpec((MOE_TILE, nb), lambda n, t, te, nu: (live(t, nu), n)),
            scratch_shapes=[pltpu.VMEM((de, nb), BF16)]),
        out_shape=jax.ShapeDtypeStruct((cap, d), F32),
        compiler_params=_cparams("arbitrary", "arbitrary"),
        name="moe_down",
    )(tile_e, n_used, hid, w_down)


def _moe_ln_kernel(x_ref, y0_ref, y1_ref, r_ref, g_ref, b_ref, o_ref, *, alpha):
    r = r_ref[...]
    moe = r[:, TOP_K:TOP_K + 1] * y0_ref[...] + r[:, TOP_K + 1:TOP_K + 2] * y1_ref[...]
    o_ref[...] = _layer_norm_rows(alpha * x_ref[...] + moe, g_ref[...], b_ref[...])


def _moe_ln(x, y2, routed, g, b, alpha):
    m, d = x.shape
    tm = _tile(m, 512)
    nt = m // tm
    row = lambda i: (i, 0)
    fixed = lambda i: (0, 0)
    return pl.pallas_call(
        functools.partial(_moe_ln_kernel, alpha=alpha),
        grid=(nt,),
        in_specs=[pl.BlockSpec((tm, d), row), pl.BlockSpec((tm, d), row),
                  pl.BlockSpec((tm, d), lambda i: (nt + i, 0)), pl.BlockSpec((tm, LANES), row),
                  pl.BlockSpec((1, d), fixed), pl.BlockSpec((1, d), fixed)],
        out_specs=pl.BlockSpec((tm, d), row),
        out_shape=jax.ShapeDtypeStruct((m, d), F32),
        compiler_params=_cparams("parallel"),
        name="moe_ln",
    )(x, y2, y2, routed, g, b)


def _moe_layer(x, layer, w_rg, b_rg, w_re, b_re, w_gate, w_up, w_down, ln_g, ln_b, alpha):
    m, d = x.shape
    wr = jnp.concatenate([w_rg, w_re], axis=1)
    wr = jnp.pad(wr, ((0, 0), (0, LANES - wr.shape[1])))
    wr_hi = wr.astype(BF16)
    wr_lo = (wr - wr_hi.astype(F32)).astype(BF16)
    br = jnp.pad(jnp.concatenate([b_rg, b_re]), (0, LANES - N_GROUPS - N_EXPERTS))[None, :]
    routed = _router(x, wr_hi, wr_lo, br)
    ids = routed[:, 0:TOP_K].astype(jnp.int32)

    n_assign = m * TOP_K
    n_tiles = -(-n_assign // MOE_TILE) + N_EXPERTS
    cap = n_tiles * MOE_TILE
    flat_e = ids.reshape(-1)
    onehot = (flat_e[:, None] == jnp.arange(N_EXPERTS, dtype=jnp.int32)[None, :]).astype(jnp.int32)
    csum = jnp.cumsum(onehot, axis=0)
    counts = csum[-1]
    rank = jnp.take_along_axis(csum, flat_e[:, None], axis=1)[:, 0] - 1
    tiles_per = (counts + MOE_TILE - 1) // MOE_TILE
    tile_ends = jnp.cumsum(tiles_per)
    pad_starts = (tile_ends - tiles_per) * MOE_TILE
    dest = pad_starts[flat_e] + rank
    n_used = tile_ends[-1:].astype(jnp.int32)
    tile_idx = jnp.arange(n_tiles, dtype=jnp.int32)
    tile_e = jnp.sum((tile_ends[None, :] <= tile_idx[:, None]).astype(jnp.int32), axis=1)
    last_e = jnp.max(jnp.where(counts > 0, jnp.arange(N_EXPERTS), 0))
    tile_e = jnp.minimum(tile_e, last_e).astype(jnp.int32)
    tok = jnp.arange(n_assign, dtype=jnp.int32) // TOP_K
    slot_tok = jnp.zeros((cap,), jnp.int32).at[dest].set(tok)

    xs = x[slot_tok]
    y = _moe_experts(xs, tile_e, n_used, w_gate, w_up, w_down, layer)
    y2 = y[dest.reshape(m, TOP_K).T.reshape(-1)]
    return _moe_ln(x, y2, routed, ln_g[None, :], ln_b[None, :], alpha)


def _rope_tables(pos):
    half = ROPE_DIM // 2
    inv_freq = ROPE_THETA ** (-jnp.arange(half, dtype=F32) / half)
    ang = pos.astype(F32)[:, None] * inv_freq
    cos, sin = jnp.cos(ang), jnp.sin(ang)
    z = jnp.zeros_like(cos)
    pad = jnp.zeros((pos.shape[0], LANES - ROPE_DIM), F32)
    c = jnp.concatenate([cos, cos, pad], axis=1)
    sa = jnp.concatenate([-sin, z, pad], axis=1)
    sb = jnp.concatenate([z, sin, pad], axis=1)
    return c, sa, sb


def _hybrid_layer(x, dims, rope, conv_state, cache_lat, cache_pe, page_table, j,
                  w_in, conv_w, q_g, kv_g, w_q_up, w_uk, w_uv, w_out, ln_g, ln_b, alpha):
    bp, sp, bs, ts = dims
    n_p = bp * sp
    cw = conv_w.shape[1]
    heads = w_q_up.shape[1]
    n_main = 3 * cw + Q_LORA + KV_LORA

    w_pe = jnp.pad(w_in[j, :, n_main:], ((0, 0), (0, LANES - ROPE_DIM))).astype(BF16)
    h = _mm(x, w_in, layer=j, n=n_main, name="hyb_in")
    h_pe = _mm(x, w_pe, name="hyb_in_pe")

    a_p, conv_p = _conv_prompt(h, conv_w, bp, sp, cw)
    a_s, conv_s = _conv_sample(h, conv_state, conv_w, n_p, bs, ts, cw)

    w_q_cat = jnp.pad(w_q_up, ((0, 0), (0, 0), (0, HEAD_PAD - NOPE_DIM - ROPE_DIM)))
    w_q_cat = w_q_cat.reshape(Q_LORA, heads * HEAD_PAD).astype(BF16)
    q_cat = _q_proj(h, 3 * cw // Q_LORA, q_g[None, :], w_q_cat, *rope, heads)
    lat, kpe = _kv_norm(h, (3 * cw + Q_LORA) // KV_LORA, h_pe, kv_g[None, :], *rope)
    w_uk2 = w_uk.reshape(KV_LORA, heads * NOPE_DIM).astype(BF16)
    w_uv2 = w_uv.reshape(KV_LORA, heads * V_DIM).astype(BF16)
    w_uv_t = w_uv2.T

    k_cat, v_t = _kv_expand(lat, kpe, w_uk2, w_uv_t, n_p, heads)
    o_p = _attn_prompt(q_cat, k_cat, v_t, bp, sp, heads)

    n_s = bs * ts
    q_lat = _q_latent(q_cat, w_uk2, n_p, n_s, heads)
    q_lat = q_lat.reshape(heads, bs, ts, KV_LORA).transpose(1, 0, 2, 3).reshape(bs, heads * ts, KV_LORA)
    q_pe = q_cat[n_p:].reshape(bs, ts, heads, HEAD_PAD)[..., NOPE_DIM:NOPE_DIM + ROPE_DIM]
    q_pe = q_pe.transpose(0, 2, 1, 3).reshape(bs, heads * ts, ROPE_DIM)
    lat_s = lat[n_p:].reshape(bs, ts, KV_LORA)
    kpe_s = kpe[n_p:, :ROPE_DIM].reshape(bs, ts, ROPE_DIM)
    new_lat = jnp.pad(lat_s, ((0, 0), (0, ts), (0, 0)))
    new_pe = jnp.pad(kpe_s, ((0, 0), (0, ts), (0, 0)))
    o_lat = _attn_sample(page_table, q_lat, q_pe, cache_lat, cache_pe, j, new_lat, new_pe, ts)
    o_lat = o_lat.reshape(bs, heads, ts, KV_LORA).transpose(1, 0, 2, 3).reshape(heads, n_s, KV_LORA)
    o_s = _o_from_latent(o_lat, w_uv2, heads)

    a = jnp.concatenate([a_p, a_s], axis=0)
    o = jnp.concatenate([o_p, o_s], axis=0)
    y = _hyb_out(a, o, w_out.astype(BF16), x, ln_g[None, :], ln_b[None, :], alpha)
    outs = (lat[:n_p].reshape(bp, sp, KV_LORA), kpe[:n_p, :ROPE_DIM].reshape(bp, sp, ROPE_DIM),
            lat_s, kpe_s, conv_p, conv_s)
    return y, outs


def _rec_layer(x, dims, state, j, lb, w_in, norm_g, w_out, ln_g, ln_b, alpha):
    bp, sp, bs, ts = dims
    n_p = bp * sp
    width = w_in.shape[-1] // 4
    heads = width // REC_DK
    h = _mm(x, w_in, layer=j, name="rec_in")
    lb2 = lb[None, :]
    o_p, st_p = _gla_prompt(h, lb2, bp, sp, heads)
    o_s, st_s = _gla_sample(h, lb2, state, j, n_p, bs, ts, heads)
    o = jnp.concatenate([o_p, o_s], axis=0)
    y = _rec_out(o, h, 3, norm_g[None, :], w_out.astype(BF16), x, ln_g[None, :], ln_b[None, :], alpha)
    return y, (jnp.swapaxes(st_p, -1, -2), st_s)


def kernel(x_prompt, x_sample, cache_kv_latent, cache_k_rope, state_conv, state_hgrn, page_table,
           w_in_hyb, conv_w, q_norm_g, kv_norm_g, w_q_up, w_uk, w_uv, w_out_hyb,
           w_in_rec, lb_logits, rec_norm_g, w_out_rec,
           ln1_g, ln1_b, ln2_g, ln2_b,
           w_router_group, b_router_group, w_router_expert, b_router_expert,
           w_gate, w_up, w_down):
    bp, sp, d = x_prompt.shape
    bs, ts, _ = x_sample.shape
    depth = ln1_g.shape[0]
    past_len = page_table.shape[1] * cache_kv_latent.shape[2]
    dims = (bp, sp, bs, ts)
    n_p = bp * sp
    alpha = (2 * depth) ** 0.25

    lower = jnp.cumsum(jax.nn.softmax(lb_logits.astype(F32), axis=0), axis=0)
    lower = lower - lower[0]

    pos = jnp.concatenate([jnp.tile(jnp.arange(sp), bp), jnp.tile(past_len + jnp.arange(ts), bs)])
    rope = _rope_tables(pos)

    x = jnp.concatenate([x_prompt.reshape(n_p, d), x_sample.reshape(bs * ts, d)], axis=0)
    cache_pe_t = jnp.swapaxes(cache_k_rope, 2, 3)
    hyb_outs, rec_outs = [], []
    for layer in range(depth):
        j = layer // 2
        if layer % 2 == 0:
            x, outs = _hybrid_layer(x, dims, rope, state_conv[j], cache_kv_latent, cache_pe_t, page_table, j,
                                    w_in_hyb, conv_w[j], q_norm_g[j], kv_norm_g[j], w_q_up[j], w_uk[j],
                                    w_uv[j], w_out_hyb[j], ln1_g[layer], ln1_b[layer], alpha)
            hyb_outs.append(outs)
        else:
            x, outs = _rec_layer(x, dims, state_hgrn, j, lower[layer], w_in_rec, rec_norm_g[j],
                                 w_out_rec[j], ln1_g[layer], ln1_b[layer], alpha)
            rec_outs.append(outs)
        x = _moe_layer(x, layer, w_router_group[layer], b_router_group[layer], w_router_expert[layer],
                       b_router_expert[layer], w_gate, w_up, w_down, ln2_g[layer], ln2_b[layer], alpha)

    stack = lambda parts, i: jnp.stack([p[i] for p in parts])
    return (x[:n_p].reshape(bp, sp, d), x[n_p:].reshape(bs, ts, d),
            stack(hyb_outs, 0), stack(hyb_outs, 1), stack(hyb_outs, 2), stack(hyb_outs, 3),
            stack(hyb_outs, 4), stack(hyb_outs, 5), stack(rec_outs, 0), stack(rec_outs, 1))
```

```python
import functools
import math

import numpy as np
import jax
import jax.numpy as jnp
from jax import lax
from jax.experimental import pallas as pl
from jax.experimental.pallas import tpu as pltpu

F32 = jnp.float32
BF16 = jnp.bfloat16

CONV_K = 3
Q_LORA = 512
KV_LORA = 512
NOPE_DIM = 128
ROPE_DIM = 64
V_DIM = 128
REC_DK = 128
N_GROUPS = 4
EXPERTS_PER_GROUP = 8
N_EXPERTS = N_GROUPS * EXPERTS_PER_GROUP
TOP_K = 2
ROPE_THETA = 10000.0
LN_EPS = 1e-5
RMS_EPS = 1e-6
MLA_SCALE = (NOPE_DIM + ROPE_DIM) ** -0.5

LANES = 128
SUBLANES = 8
VMEM_LIMIT_BYTES = 52 * 1024 * 1024

HEAD_PAD = 2 * LANES
GLA_CHUNK = 64
GLA_HEADS_PER_STEP = 4
MOE_TILE = 256
ATTN_TQ = 512
PAGES_PER_STEP = 16


def _cparams(*sem):
    return pltpu.CompilerParams(dimension_semantics=sem, vmem_limit_bytes=VMEM_LIMIT_BYTES)


def _tile(n, pref):
    if n <= pref:
        return n
    for t in range(pref, 7, -1):
        if n % t == 0 and t % 8 == 0:
            return t
    return n


def _dot(a, b):
    return jnp.dot(a, b, preferred_element_type=F32)


def _dot_nt(a, b):
    return lax.dot_general(a, b, (((1,), (1,)), ((), ())), preferred_element_type=F32)


def _dot_tn(a, b):
    return lax.dot_general(a, b, (((0,), (0,)), ((), ())), preferred_element_type=F32)


def _mm_kernel(x_ref, w_ref, o_ref, xb_ref):
    @pl.when(pl.program_id(1) == 0)
    def _():
        xb_ref[...] = x_ref[...].astype(BF16)

    o_ref[...] = _dot(xb_ref[...], w_ref[...].astype(BF16)).astype(o_ref.dtype)


def _mm(x, w, *, layer=None, n=None, tm=1024, tn=512, out_dtype=F32, name="mm"):
    m, k = x.shape
    n = w.shape[-1] if n is None else n
    tm = _tile(m, tm)
    tn = _tile(n, tn)
    if layer is None:
        w_spec = pl.BlockSpec((k, tn), lambda i, j: (0, j))
    else:
        w_spec = pl.BlockSpec((None, k, tn), lambda i, j: (layer, 0, j))
    return pl.pallas_call(
        _mm_kernel,
        grid=(m // tm, n // tn),
        in_specs=[pl.BlockSpec((tm, k), lambda i, j: (i, 0)), w_spec],
        out_specs=pl.BlockSpec((tm, tn), lambda i, j: (i, j)),
        out_shape=jax.ShapeDtypeStruct((m, n), out_dtype),
        scratch_shapes=[pltpu.VMEM((tm, k), BF16)],
        compiler_params=_cparams("parallel", "arbitrary"),
        name=name,
    )(x, w)


def _layer_norm_rows(z, g, b):
    mu = jnp.mean(z, axis=-1, keepdims=True)
    zc = z - mu
    var = jnp.mean(zc * zc, axis=-1, keepdims=True)
    return zc * lax.rsqrt(var + LN_EPS) * g + b


def _rms_norm_rows(z, g):
    return z * lax.rsqrt(jnp.mean(z * z, axis=-1, keepdims=True) + RMS_EPS) * g


def _conv_prompt_kernel(cx_ref, gb_ref, gc_ref, cxp_ref, gcp_ref, w_ref, a_ref, st_ref, *, tt):
    i = pl.program_id(1)
    u = gc_ref[...] * cx_ref[...]
    prev = jnp.where(i > 0, gcp_ref[...] * cxp_ref[...], 0.0)
    row = lax.broadcasted_iota(jnp.int32, u.shape, 0)
    u1 = jnp.where(row == 0, prev[7:8], pltpu.roll(u, 1, 0))
    u2 = jnp.where(row == 0, prev[6:7], jnp.where(row == 1, prev[7:8], pltpu.roll(u, 2, 0)))
    v = w_ref[0:1, :] * u2 + w_ref[1:2, :] * u1 + w_ref[2:3, :] * u
    a_ref[...] = gb_ref[...] * v

    @pl.when(i == pl.num_programs(1) - 1)
    def _():
        st_ref[...] = u[tt - (CONV_K - 1):tt]


def _conv_prompt(h, conv_w, bsz, seq, cw):
    tt = _tile(seq, 512)
    nt = seq // tt
    sub = tt // SUBLANES

    def main(c):
        return pl.BlockSpec((tt, cw), lambda b, i: (b * nt + i, c))

    def halo(c):
        return pl.BlockSpec((SUBLANES, cw), lambda b, i: (jnp.maximum((b * nt + i) * sub - 1, 0), c))

    return pl.pallas_call(
        functools.partial(_conv_prompt_kernel, tt=tt),
        grid=(bsz, nt),
        in_specs=[main(0), main(1), main(2), halo(0), halo(2),
                  pl.BlockSpec((CONV_K, cw), lambda b, i: (0, 0))],
        out_specs=[pl.BlockSpec((tt, cw), lambda b, i: (b * nt + i, 0)),
                   pl.BlockSpec((None, CONV_K - 1, cw), lambda b, i: (b, 0, 0))],
        out_shape=[jax.ShapeDtypeStruct((bsz * seq, cw), F32),
                   jax.ShapeDtypeStruct((bsz, CONV_K - 1, cw), F32)],
        compiler_params=_cparams("parallel", "arbitrary"),
        name="conv_prompt",
    )(h, h, h, h, h, conv_w)


def _conv_sample_kernel(cx_ref, gb_ref, gc_ref, st_ref, w_ref, a_ref, sto_ref, *, nb, ts):
    cw = cx_ref.shape[-1]
    u = (gc_ref[...] * cx_ref[...]).reshape(nb, ts, cw)
    st = st_ref[...]
    t = lax.broadcasted_iota(jnp.int32, u.shape, 1)
    s0 = st[:, 0:1, :]
    s1 = st[:, 1:2, :]
    u1 = jnp.where(t == 0, s1, pltpu.roll(u, 1, 1))
    u2 = jnp.where(t == 0, s0, jnp.where(t == 1, s1, pltpu.roll(u, 2, 1)))
    w = w_ref[...]
    v = w[0:1, :][None] * u2 + w[1:2, :][None] * u1 + w[2:3, :][None] * u
    a_ref[...] = gb_ref[...] * v.reshape(nb * ts, cw)
    sto_ref[...] = u[:, ts - (CONV_K - 1):ts, :]


def _conv_sample(h, state, conv_w, row0, bsz, ts, cw):
    nb = _tile(bsz, 64)
    rows = nb * ts
    off = row0 // rows

    def main(c):
        return pl.BlockSpec((rows, cw), lambda i: (off + i, c))

    return pl.pallas_call(
        functools.partial(_conv_sample_kernel, nb=nb, ts=ts),
        grid=(bsz // nb,),
        in_specs=[main(0), main(1), main(2),
                  pl.BlockSpec((nb, CONV_K - 1, cw), lambda i: (i, 0, 0)),
                  pl.BlockSpec((CONV_K, cw), lambda i: (0, 0))],
        out_specs=[pl.BlockSpec((rows, cw), lambda i: (i, 0)),
                   pl.BlockSpec((nb, CONV_K - 1, cw), lambda i: (i, 0, 0))],
        out_shape=[jax.ShapeDtypeStruct((bsz * ts, cw), F32),
                   jax.ShapeDtypeStruct((bsz, CONV_K - 1, cw), F32)],
        compiler_params=_cparams("parallel"),
        name="conv_sample",
    )(h, h, h, state, conv_w)


def _rope_lanes(x, c, sa, sb):
    half = ROPE_DIM // 2
    return x * c + pltpu.roll(x, LANES - half, 1) * sa + pltpu.roll(x, half, 1) * sb


def _q_proj_kernel(qc_ref, g_ref, w_ref, c_ref, sa_ref, sb_ref, o_ref, *, heads):
    xn = _rms_norm_rows(qc_ref[...], g_ref[...]).astype(BF16)
    q = _dot(xn, w_ref[...]) * MLA_SCALE
    c, sa, sb = c_ref[...], sa_ref[...], sb_ref[...]
    for hd in range(heads):
        lo = hd * HEAD_PAD
        o_ref[:, lo:lo + LANES] = q[:, lo:lo + LANES].astype(o_ref.dtype)
        pe = _rope_lanes(q[:, lo + LANES:lo + HEAD_PAD], c, sa, sb)
        o_ref[:, lo + LANES:lo + HEAD_PAD] = pe.astype(o_ref.dtype)


def _q_proj(h, col_block, g, w_cat, rope_c, rope_sa, rope_sb, heads):
    m = h.shape[0]
    tm = _tile(m, 512)
    n = heads * HEAD_PAD
    row = lambda i: (i, 0)
    return pl.pallas_call(
        functools.partial(_q_proj_kernel, heads=heads),
        grid=(m // tm,),
        in_specs=[pl.BlockSpec((tm, Q_LORA), lambda i: (i, col_block)),
                  pl.BlockSpec((1, Q_LORA), lambda i: (0, 0)),
                  pl.BlockSpec((Q_LORA, n), lambda i: (0, 0)),
                  pl.BlockSpec((tm, LANES), row), pl.BlockSpec((tm, LANES), row),
                  pl.BlockSpec((tm, LANES), row)],
        out_specs=pl.BlockSpec((tm, n), row),
        out_shape=jax.ShapeDtypeStruct((m, n), BF16),
        compiler_params=_cparams("parallel"),
        name="q_proj",
    )(h, g, w_cat, rope_c, rope_sa, rope_sb)


def _kv_norm_kernel(kv_ref, pe_ref, g_ref, c_ref, sa_ref, sb_ref, lat_ref, kpe_ref):
    lat_ref[...] = _rms_norm_rows(kv_ref[...], g_ref[...])
    kpe_ref[...] = _rope_lanes(pe_ref[...], c_ref[...], sa_ref[...], sb_ref[...])


def _kv_norm(h, col_block, h_pe, g, rope_c, rope_sa, rope_sb):
    m = h.shape[0]
    tm = _tile(m, 1024)
    row = lambda i: (i, 0)
    return pl.pallas_call(
        _kv_norm_kernel,
        grid=(m // tm,),
        in_specs=[pl.BlockSpec((tm, KV_LORA), lambda i: (i, col_block)),
                  pl.BlockSpec((tm, LANES), row),
                  pl.BlockSpec((1, KV_LORA), lambda i: (0, 0)),
                  pl.BlockSpec((tm, LANES), row), pl.BlockSpec((tm, LANES), row),
                  pl.BlockSpec((tm, LANES), row)],
        out_specs=[pl.BlockSpec((tm, KV_LORA), row), pl.BlockSpec((tm, LANES), row)],
        out_shape=[jax.ShapeDtypeStruct((m, KV_LORA), F32),
                   jax.ShapeDtypeStruct((m, LANES), F32)],
        compiler_params=_cparams("parallel"),
        name="kv_norm",
    )(h, h_pe, g, rope_c, rope_sa, rope_sb)


def _kv_expand_kernel(lat_ref, kpe_ref, wk_ref, wvt_ref, k_ref, vt_ref, *, heads):
    lat = lat_ref[...].astype(BF16)
    kn = _dot(lat, wk_ref[...])
    kpe = kpe_ref[...].astype(k_ref.dtype)
    for hd in range(heads):
        lo = hd * HEAD_PAD
        k_ref[:, lo:lo + LANES] = kn[:, hd * NOPE_DIM:(hd + 1) * NOPE_DIM].astype(k_ref.dtype)
        k_ref[:, lo + LANES:lo + HEAD_PAD] = kpe
    vt_ref[...] = _dot_nt(wvt_ref[...], lat).astype(vt_ref.dtype)


def _kv_expand(lat, kpe, w_uk, w_uv_t, rows, heads):
    tm = _tile(rows, 512)
    row = lambda i: (i, 0)
    fixed = lambda i: (0, 0)
    return pl.pallas_call(
        functools.partial(_kv_expand_kernel, heads=heads),
        grid=(rows // tm,),
        in_specs=[pl.BlockSpec((tm, KV_LORA), row), pl.BlockSpec((tm, LANES), row),
                  pl.BlockSpec((KV_LORA, heads * NOPE_DIM), fixed),
                  pl.BlockSpec((heads * V_DIM, KV_LORA), fixed)],
        out_specs=[pl.BlockSpec((tm, heads * HEAD_PAD), row),
                   pl.BlockSpec((heads * V_DIM, tm), lambda i: (0, i))],
        out_shape=[jax.ShapeDtypeStruct((rows, heads * HEAD_PAD), BF16),
                   jax.ShapeDtypeStruct((heads * V_DIM, rows), BF16)],
        compiler_params=_cparams("parallel"),
        name="kv_expand",
    )(lat, kpe, w_uk, w_uv_t)


def _attn_prompt_kernel(q_ref, k_ref, vt_ref, o_ref, m_ref, l_ref, acc_ref, *, tq):
    i = pl.program_id(2)
    m_ref[...] = jnp.full(m_ref.shape, -jnp.inf, F32)
    l_ref[...] = jnp.zeros(l_ref.shape, F32)
    acc_ref[...] = jnp.zeros(acc_ref.shape, F32)
    q = q_ref[...]

    def chunk(c, masked):
        r0 = pl.multiple_of(c * tq, tq)
        s = _dot_nt(k_ref[pl.ds(r0, tq), :], q)
        if masked:
            key = lax.broadcasted_iota(jnp.int32, s.shape, 0)
            qry = lax.broadcasted_iota(jnp.int32, s.shape, 1)
            s = jnp.where(key <= qry, s, -jnp.inf)
        m_old = m_ref[...]
        m_new = jnp.maximum(m_old, jnp.max(s, axis=0, keepdims=True))
        alpha = jnp.exp(m_old - m_new)
        p = jnp.exp(s - m_new)
        l_ref[...] = alpha * l_ref[...] + jnp.sum(p, axis=0, keepdims=True)
        acc_ref[...] = alpha * acc_ref[...] + _dot(vt_ref[:, pl.ds(r0, tq)], p.astype(BF16))
        m_ref[...] = m_new

    def body(c, carry):
        chunk(c, False)
        return carry

    lax.fori_loop(0, i, body, 0)
    chunk(i, True)
    o_ref[...] = (acc_ref[...] / l_ref[...]).T.astype(o_ref.dtype)


def _attn_prompt(q_cat, k_cat, v_t, bsz, seq, heads):
    tq = _tile(seq, ATTN_TQ)
    nq = seq // tq
    return pl.pallas_call(
        functools.partial(_attn_prompt_kernel, tq=tq),
        grid=(bsz, heads, nq),
        in_specs=[pl.BlockSpec((tq, HEAD_PAD), lambda b, h, i: (b * nq + i, h)),
                  pl.BlockSpec((seq, HEAD_PAD), lambda b, h, i: (b, h)),
                  pl.BlockSpec((V_DIM, seq), lambda b, h, i: (h, b))],
        out_specs=pl.BlockSpec((tq, V_DIM), lambda b, h, i: (b * nq + i, h)),
        out_shape=jax.ShapeDtypeStruct((bsz * seq, heads * V_DIM), F32),
        scratch_shapes=[pltpu.VMEM((1, tq), F32), pltpu.VMEM((1, tq), F32),
                        pltpu.VMEM((V_DIM, tq), F32)],
        compiler_params=_cparams("parallel", "parallel", "arbitrary"),
        name="attn_prompt",
    )(q_cat, k_cat, v_t)


def _head_mm_kernel(x_ref, w_ref, o_ref, *, nt):
    x = x_ref[...].astype(BF16)
    o_ref[...] = (_dot_nt(x, w_ref[...]) if nt else _dot(x, w_ref[...])).astype(o_ref.dtype)


def _q_latent(q_cat, w_uk, row0, rows, heads):
    off = row0 // rows
    return pl.pallas_call(
        functools.partial(_head_mm_kernel, nt=True),
        grid=(heads,),
        in_specs=[pl.BlockSpec((rows, NOPE_DIM), lambda h: (off, 2 * h)),
                  pl.BlockSpec((KV_LORA, NOPE_DIM), lambda h: (0, h))],
        out_specs=pl.BlockSpec((None, rows, KV_LORA), lambda h: (h, 0, 0)),
        out_shape=jax.ShapeDtypeStruct((heads, rows, KV_LORA), BF16),
        compiler_params=_cparams("parallel"),
        name="q_latent",
    )(q_cat, w_uk)


def _o_from_latent(o_lat, w_uv, heads):
    rows = o_lat.shape[1]
    return pl.pallas_call(
        functools.partial(_head_mm_kernel, nt=False),
        grid=(heads,),
        in_specs=[pl.BlockSpec((None, rows, KV_LORA), lambda h: (h, 0, 0)),
                  pl.BlockSpec((KV_LORA, V_DIM), lambda h: (0, h))],
        out_specs=pl.BlockSpec((rows, V_DIM), lambda h: (0, h)),
        out_shape=jax.ShapeDtypeStruct((rows, heads * V_DIM), F32),
        compiler_params=_cparams("parallel"),
        name="o_from_latent",
    )(o_lat, w_uv)


def _lane_chunks(s):
    w = s.shape[-1]
    if w % LANES or w == LANES:
        return [s]
    return [s[:, c * LANES:(c + 1) * LANES] for c in range(w // LANES)]


def _row_max(s):
    parts = _lane_chunks(s)
    return jnp.max(functools.reduce(jnp.maximum, parts), axis=-1, keepdims=True)


def _row_sum(s):
    parts = _lane_chunks(s)
    return jnp.sum(functools.reduce(jnp.add, parts), axis=-1, keepdims=True)


def _attn_sample_kernel(pt_ref, ql_ref, qp_ref, lat_hbm, pe_hbm, nl_ref, np_ref, o_ref,
                        lat_buf, pe_buf, lat_sem, pe_sem, k_ref, pe_ref, m_ref, l_ref, acc_ref,
                        *, layer, npg, ts, page):
    s_idx = pl.program_id(1)
    n_groups = pl.num_programs(1)
    lin = pl.program_id(0) * n_groups + s_idx
    slot = lin % 2

    def page_copies(group, slot_, j):
        pid = pt_ref[group * npg + j]
        rows = pl.ds(j * page, page)
        return (pltpu.make_async_copy(lat_hbm.at[layer, pid], lat_buf.at[slot_, rows, :], lat_sem.at[slot_]),
                pltpu.make_async_copy(pe_hbm.at[layer, pid], pe_buf.at[slot_, :, rows], pe_sem.at[slot_]))

    def start_group(group, slot_):
        for j in range(npg):
            for cp in page_copies(group, slot_, j):
                cp.start()

    @pl.when(lin == 0)
    def _():
        start_group(lin, slot)

    @pl.when(lin + 1 < pl.num_programs(0) * n_groups)
    def _():
        start_group(lin + 1, 1 - slot)

    for j in range(npg):
        for cp in page_copies(lin, slot, j):
            cp.wait()

    @pl.when(s_idx == 0)
    def _():
        m_ref[...] = jnp.full(m_ref.shape, -jnp.inf, F32)
        l_ref[...] = jnp.zeros(l_ref.shape, F32)
        acc_ref[...] = jnp.zeros(acc_ref.shape, F32)

    ql = ql_ref[...]
    qp = qp_ref[...]

    def update(s, values):
        m_old = m_ref[...]
        m_new = jnp.maximum(m_old, _row_max(s))
        alpha = jnp.exp(m_old - m_new)
        p = jnp.exp(s - m_new)
        l_ref[...] = alpha * l_ref[...] + _row_sum(p)
        acc_ref[...] = alpha * acc_ref[...] + _dot(p.astype(BF16), values)
        m_ref[...] = m_new

    k_ref[...] = lat_buf[slot].astype(BF16)
    pe_ref[...] = pe_buf[slot].astype(BF16)
    keys = k_ref[...]
    update(_dot_nt(ql, keys) + _dot(qp, pe_ref[...]), keys)

    @pl.when(s_idx == pl.num_programs(1) - 1)
    def _():
        kn = nl_ref[...].astype(BF16)
        s = _dot_nt(ql, kn) + _dot_nt(qp, np_ref[...].astype(BF16))
        t_q = lax.broadcasted_iota(jnp.int32, s.shape, 0) % ts
        t_k = lax.broadcasted_iota(jnp.int32, s.shape, 1)
        update(jnp.where(t_k <= t_q, s, -jnp.inf), kn)
        o_ref[...] = (acc_ref[...] / l_ref[...]).astype(o_ref.dtype)


def _attn_sample(page_table, q_lat, q_pe, cache_lat, cache_pe_t, layer, new_lat, new_pe, ts):
    bsz, qrows, _ = q_lat.shape
    n_pages = page_table.shape[1]
    page = cache_lat.shape[2]
    npg = math.gcd(PAGES_PER_STEP, n_pages)
    steps = n_pages // npg
    pt_flat = page_table.reshape(-1)

    per_b = lambda b, s, pt: (b, 0, 0)
    keys = npg * page
    grid_spec = pltpu.PrefetchScalarGridSpec(
        num_scalar_prefetch=1,
        grid=(bsz, steps),
        in_specs=[pl.BlockSpec((None, qrows, KV_LORA), per_b),
                  pl.BlockSpec((None, qrows, ROPE_DIM), per_b),
                  pl.BlockSpec(memory_space=pl.ANY), pl.BlockSpec(memory_space=pl.ANY),
                  pl.BlockSpec((None, 2 * ts, KV_LORA), per_b),
                  pl.BlockSpec((None, 2 * ts, ROPE_DIM), per_b)],
        out_specs=pl.BlockSpec((None, qrows, KV_LORA), per_b),
        scratch_shapes=[pltpu.VMEM((2, keys, KV_LORA), F32), pltpu.VMEM((2, ROPE_DIM, keys), F32),
                        pltpu.SemaphoreType.DMA((2,)), pltpu.SemaphoreType.DMA((2,)),
                        pltpu.VMEM((keys, KV_LORA), BF16), pltpu.VMEM((ROPE_DIM, keys), BF16),
                        pltpu.VMEM((qrows, 1), F32), pltpu.VMEM((qrows, 1), F32),
                        pltpu.VMEM((qrows, KV_LORA), F32)],
    )
    return pl.pallas_call(
        functools.partial(_attn_sample_kernel, layer=layer, npg=npg, ts=ts, page=page),
        grid_spec=grid_spec,
        out_shape=jax.ShapeDtypeStruct((bsz, qrows, KV_LORA), BF16),
        compiler_params=_cparams("arbitrary", "arbitrary"),
        name="attn_sample",
    )(pt_flat, q_lat, q_pe, cache_lat, cache_pe_t, new_lat, new_pe)


def _two_part_specs(tm, width, nt_p):
    return (pl.BlockSpec((tm, width), lambda i: (jnp.minimum(i, nt_p - 1), 0)),
            pl.BlockSpec((tm, width), lambda i: (jnp.maximum(i - nt_p, 0), 0)))


def _pick_part(nt_p, p_ref, s_ref):
    return jnp.where(pl.program_id(0) < nt_p, p_ref[...], s_ref[...])


def _hyb_out_kernel(ap_ref, as_ref, op_ref, os_ref, w_ref, x_ref, g_ref, b_ref, y_ref, *, alpha, nt_p):
    a = _pick_part(nt_p, ap_ref, as_ref).astype(BF16)
    o = _pick_part(nt_p, op_ref, os_ref).astype(BF16)
    ka = a.shape[1]
    mix = _dot(a, w_ref[0:ka, :]) + _dot(o, w_ref[ka:, :])
    y_ref[...] = _layer_norm_rows(alpha * x_ref[...] + mix, g_ref[...], b_ref[...])


def _hyb_out(a_p, a_s, o_p, o_s, w_out, x, g, b, alpha):
    m, d = x.shape
    n_p, n_s = a_p.shape[0], a_s.shape[0]
    tm = _tile(math.gcd(n_p, n_s), 256)
    nt_p = n_p // tm
    row = lambda i: (i, 0)
    fixed = lambda i: (0, 0)
    return pl.pallas_call(
        functools.partial(_hyb_out_kernel, alpha=alpha, nt_p=nt_p),
        grid=(m // tm,),
        in_specs=[*_two_part_specs(tm, a_p.shape[1], nt_p), *_two_part_specs(tm, o_p.shape[1], nt_p),
                  pl.BlockSpec(w_out.shape, fixed), pl.BlockSpec((tm, d), row),
                  pl.BlockSpec((1, d), fixed), pl.BlockSpec((1, d), fixed)],
        out_specs=pl.BlockSpec((tm, d), row),
        out_shape=jax.ShapeDtypeStruct((m, d), F32),
        compiler_params=_cparams("parallel"),
        name="hyb_out",
    )(a_p, a_s, o_p, o_s, w_out, x, g, b)


def _rec_out_kernel(op_ref, os_ref, gate_ref, ng_ref, w_ref, x_ref, g_ref, b_ref, y_ref, *, alpha, nt_p):
    gate = gate_ref[...]
    o = _pick_part(nt_p, op_ref, os_ref)
    on = _rms_norm_rows(o, ng_ref[...]) * (gate * jax.nn.sigmoid(gate))
    mix = _dot(on.astype(BF16), w_ref[...])
    y_ref[...] = _layer_norm_rows(alpha * x_ref[...] + mix, g_ref[...], b_ref[...])


def _rec_out(o_p, o_s, h, gate_col_block, norm_g, w_out, x, g, b, alpha):
    m, d = x.shape
    vw = o_p.shape[1]
    n_p, n_s = o_p.shape[0], o_s.shape[0]
    tm = _tile(math.gcd(n_p, n_s), 256)
    nt_p = n_p // tm
    row = lambda i: (i, 0)
    fixed = lambda i: (0, 0)
    return pl.pallas_call(
        functools.partial(_rec_out_kernel, alpha=alpha, nt_p=nt_p),
        grid=(m // tm,),
        in_specs=[*_two_part_specs(tm, vw, nt_p),
                  pl.BlockSpec((tm, vw), lambda i: (i, gate_col_block)),
                  pl.BlockSpec((1, vw), fixed), pl.BlockSpec(w_out.shape, fixed),
                  pl.BlockSpec((tm, d), row), pl.BlockSpec((1, d), fixed),
                  pl.BlockSpec((1, d), fixed)],
        out_specs=pl.BlockSpec((tm, d), row),
        out_shape=jax.ShapeDtypeStruct((m, d), F32),
        compiler_params=_cparams("parallel"),
        name="rec_out",
    )(o_p, o_s, h, norm_g, w_out, x, g, b)


def _gla_levels(chunk):
    m = chunk // 2
    out = []
    while m >= 1:
        out.append(m)
        m //= 2
    return tuple(out)


def _gla_prefix_matrix(chunk):
    t = np.arange(chunk)
    j = np.arange(chunk)
    blocks = [(j[None, :] <= t[:, None])]
    for m in _gla_levels(chunk):
        ref = (t // (2 * m)) * (2 * m) + m
        blocks.append(j[None, :] <= ref[:, None])
    blocks.append(np.ones((chunk, chunk), bool))
    return np.concatenate(blocks, 0).astype(np.float32)


def _gla_gates(q_raw, f_raw, lb):
    q = q_raw * jax.nn.sigmoid(q_raw) * (REC_DK ** -0.5)
    forget = lb + (1.0 - lb) * jax.nn.sigmoid(f_raw)
    return q, 1.0 - forget, jnp.log(forget)


def _split3(x):
    hi = x.astype(BF16)
    r1 = x - hi.astype(F32)
    mid = r1.astype(BF16)
    lo = (r1 - mid.astype(F32)).astype(BF16)
    return hi, mid, lo


def _gla_prompt_kernel(q_ref, f_ref, v_ref, lb_ref, pm_ref, o_ref, st_ref, s_ref, *, tt, gh):
    c = GLA_CHUNK
    dk = REC_DK
    i = pl.program_id(2)

    @pl.when(i == 0)
    def _():
        s_ref[...] = jnp.zeros(s_ref.shape, F32)

    lb = lb_ref[...]
    pm = pm_ref[...]
    levels = _gla_levels(c)
    row = lax.broadcasted_iota(jnp.int32, (c, dk), 0)
    ti = lax.broadcasted_iota(jnp.int32, (c, c), 0)
    si = lax.broadcasted_iota(jnp.int32, (c, c), 1)

    def body(ci, carry):
        r0 = pl.multiple_of(ci * c, c)
        q_all, k_all, g_all = _gla_gates(q_ref[pl.ds(r0, c), :], f_ref[pl.ds(r0, c), :], lb)
        v_all = v_ref[pl.ds(r0, c), :].astype(BF16)
        hi, mid, lo = _split3(g_all)
        pre = _dot(pm, hi) + _dot(pm, mid) + _dot(pm, lo)
        outs, states = [], []
        for hd in range(gh):
            sl = slice(hd * dk, (hd + 1) * dk)
            q, k, v = q_all[:, sl], k_all[:, sl], v_all[:, sl]
            cum = pre[0:c, sl]
            last = pre[(len(levels) + 1) * c:(len(levels) + 2) * c, sl]
            att = jnp.where(ti == si, _dot_nt(q.astype(BF16), k.astype(BF16)), 0.0)
            for li, m in enumerate(levels):
                ref = pre[(li + 1) * c:(li + 2) * c, sl]
                e = jnp.exp(-jnp.abs(cum - ref))
                later = (row // m) % 2 == 1
                ql = jnp.where(later, q * e, 0.0).astype(BF16)
                kl = jnp.where(later, 0.0, k * e).astype(BF16)
                att = att + jnp.where(ti // (2 * m) == si // (2 * m), _dot_nt(ql, kl), 0.0)
            s_t = s_ref[hd * dk:(hd + 1) * dk, :]
            inter = _dot_nt((q * jnp.exp(cum)).astype(BF16), s_t.astype(BF16))
            outs.append(inter + _dot(att.astype(BF16), v))
            kd = (k * jnp.exp(last - cum)).astype(BF16)
            states.append(s_t * jnp.exp(last[0:1, :]) + _dot_tn(v, kd))
        o_ref[pl.ds(r0, c), :] = jnp.concatenate(outs, axis=1)
        s_ref[...] = jnp.concatenate(states, axis=0)
        return carry

    lax.fori_loop(0, tt // c, body, 0)

    @pl.when(i == pl.num_programs(2) - 1)
    def _():
        for hd in range(gh):
            st_ref[hd] = s_ref[hd * dk:(hd + 1) * dk, :]


def _gla_prompt(h, lb, bsz, seq, heads):
    tt = _tile(seq, 512)
    assert tt % GLA_CHUNK == 0
    nt = seq // tt
    gh = math.gcd(GLA_HEADS_PER_STEP, heads)
    ng = heads // gh
    wide = gh * REC_DK
    pm = jnp.asarray(_gla_prefix_matrix(GLA_CHUNK), BF16)

    def col(base):
        return pl.BlockSpec((tt, wide), lambda b, hg, i: (b * nt + i, base + hg))

    return pl.pallas_call(
        functools.partial(_gla_prompt_kernel, tt=tt, gh=gh),
        grid=(bsz, ng, nt),
        in_specs=[col(0), col(ng), col(2 * ng),
                  pl.BlockSpec((1, wide), lambda b, hg, i: (0, hg)),
                  pl.BlockSpec(pm.shape, lambda b, hg, i: (0, 0))],
        out_specs=[pl.BlockSpec((tt, wide), lambda b, hg, i: (b * nt + i, hg)),
                   pl.BlockSpec((None, gh, REC_DK, REC_DK), lambda b, hg, i: (b, hg, 0, 0))],
        out_shape=[jax.ShapeDtypeStruct((bsz * seq, heads * REC_DK), F32),
                   jax.ShapeDtypeStruct((bsz, heads, REC_DK, REC_DK), F32)],
        scratch_shapes=[pltpu.VMEM((wide, REC_DK), F32)],
        compiler_params=_cparams("parallel", "parallel", "arbitrary"),
        name="gla_prompt",
    )(h, h, h, lb, pm)


def _gla_sample_kernel(q_ref, f_ref, v_ref, lb_ref, hs_ref, he_ref, st_ref, o_ref, sto_ref, *, ts, heads):
    dk = REC_DK
    q, k, g = _gla_gates(q_ref[...], f_ref[...], lb_ref[...])
    v = v_ref[...]
    t = lax.broadcasted_iota(jnp.int32, q.shape, 0)
    cum = g
    sh = 1
    while sh < ts:
        cum = cum + jnp.where(t >= sh, pltpu.roll(cum, sh, 0), 0.0)
        sh *= 2
    last = cum[ts - 1:ts, :]
    xs = []
    for s in range(ts):
        dec = jnp.exp(jnp.where(t >= s, cum - cum[s:s + 1, :], -jnp.inf))
        xs.append(q * k[s:s + 1, :] * dec)
    x = jnp.concatenate(xs, axis=0)
    x_hi = x.astype(BF16)
    x_lo = (x - x_hi.astype(F32)).astype(BF16)
    att = _dot(x_hi, hs_ref[...]) + _dot(x_lo, hs_ref[...])
    att_e = _dot(att.astype(BF16), he_ref[...])
    intra = jnp.zeros(q.shape, F32)
    for s in range(ts):
        intra = intra + att_e[s * ts:(s + 1) * ts, :] * v[s:s + 1, :]
    qe = (q * jnp.exp(cum)).astype(BF16)
    kd = (k * jnp.exp(last - cum)).astype(BF16)
    dec_last = jnp.exp(last)
    eye = (lax.broadcasted_iota(jnp.int32, (dk, dk), 0)
           == lax.broadcasted_iota(jnp.int32, (dk, dk), 1))
    vb = v.astype(BF16)
    for hd in range(heads):
        sl = slice(hd * dk, (hd + 1) * dk)
        s_h = st_ref[hd]
        o_ref[:, sl] = _dot(qe[:, sl], s_h.astype(BF16)) + intra[:, sl]
        d_col = jnp.sum(jnp.where(eye, dec_last[:, sl], 0.0), axis=1, keepdims=True)
        sto_ref[hd] = s_h * d_col + _dot_tn(kd[:, sl], vb[:, sl])


def _gla_sample(h, lb, state, j, row0, bsz, ts, heads):
    width = heads * REC_DK
    off = row0 // ts
    head_of = np.arange(width) // REC_DK
    hsum = (head_of[:, None] == np.arange(LANES)[None, :]).astype(np.float32)
    hs = jnp.asarray(hsum, BF16)
    he = jnp.asarray(hsum.T, BF16)

    def col(cb):
        return pl.BlockSpec((ts, width), lambda b: (off + b, cb))

    fixed = lambda b: (0, 0)
    st_in = pl.BlockSpec((None, None, heads, REC_DK, REC_DK), lambda b: (j, b, 0, 0, 0))
    st_out = pl.BlockSpec((None, heads, REC_DK, REC_DK), lambda b: (b, 0, 0, 0))
    return pl.pallas_call(
        functools.partial(_gla_sample_kernel, ts=ts, heads=heads),
        grid=(bsz,),
        in_specs=[col(0), col(1), col(2), pl.BlockSpec((1, width), fixed),
                  pl.BlockSpec(hs.shape, fixed), pl.BlockSpec(he.shape, fixed), st_in],
        out_specs=[pl.BlockSpec((ts, width), lambda b: (b, 0)), st_out],
        out_shape=[jax.ShapeDtypeStruct((bsz * ts, width), F32),
                   jax.ShapeDtypeStruct(state.shape[1:], F32)],
        compiler_params=_cparams("parallel"),
        name="gla_sample",
    )(h, h, h, lb, hs, he, state)


def _router_kernel(x_ref, wh_ref, wl_ref, b_ref, o_ref):
    x = x_ref[...]
    x_hi = x.astype(BF16)
    x_lo = (x - x_hi.astype(F32)).astype(BF16)
    wh = wh_ref[...]
    logits = _dot(x_hi, wh) + _dot(x_lo, wh) + _dot(x_hi, wl_ref[...]) + b_ref[...]
    lane = lax.broadcasted_iota(jnp.int32, logits.shape, 1).astype(F32)
    big = float(LANES)
    neg = -jnp.inf
    gl = jnp.where(lane < N_GROUPS, logits, neg)
    g_max = jnp.max(gl, axis=-1, keepdims=True)
    g_sel = jnp.min(jnp.where(gl == g_max, lane, big), axis=-1, keepdims=True)
    g_gate = 1.0 / jnp.sum(jnp.exp(gl - g_max), axis=-1, keepdims=True)
    lo = N_GROUPS + g_sel * EXPERTS_PER_GROUP
    el = jnp.where((lane >= lo) & (lane < lo + EXPERTS_PER_GROUP), logits, neg)
    e_max = jnp.max(el, axis=-1, keepdims=True)
    i1 = jnp.min(jnp.where(el == e_max, lane, big), axis=-1, keepdims=True)
    denom = jnp.sum(jnp.exp(el - e_max), axis=-1, keepdims=True)
    el2 = jnp.where(lane == i1, neg, el)
    m2 = jnp.max(el2, axis=-1, keepdims=True)
    i2 = jnp.min(jnp.where(el2 == m2, lane, big), axis=-1, keepdims=True)
    p1 = 1.0 / denom
    p2 = jnp.exp(m2 - e_max) / denom
    w1 = g_gate * p1 / (p1 + p2)
    w2 = g_gate * p2 / (p1 + p2)
    e1 = i1 - N_GROUPS
    e2 = i2 - N_GROUPS
    o_ref[...] = jnp.where(lane == 0, e1, jnp.where(lane == 1, e2,
                           jnp.where(lane == 2, w1, jnp.where(lane == 3, w2, 0.0))))


def _router(x, w_hi, w_lo, bias):
    m, d = x.shape
    tm = _tile(m, 512)
    fixed = lambda i: (0, 0)
    return pl.pallas_call(
        _router_kernel,
        grid=(m // tm,),
        in_specs=[pl.BlockSpec((tm, d), lambda i: (i, 0)), pl.BlockSpec((d, LANES), fixed),
                  pl.BlockSpec((d, LANES), fixed), pl.BlockSpec((1, LANES), fixed)],
        out_specs=pl.BlockSpec((tm, LANES), lambda i: (i, 0)),
        out_shape=jax.ShapeDtypeStruct((m, LANES), F32),
        compiler_params=_cparams("parallel"),
        name="router",
    )(x, w_hi, w_lo, bias)


def _expert_changed(te_ref, t):
    return jnp.logical_or(t == 0, te_ref[t] != te_ref[jnp.maximum(t - 1, 0)])


def _moe_up_kernel(te_ref, nu_ref, x_ref, wg_ref, wu_ref, h_ref, wgb_ref, wub_ref):
    t = pl.program_id(1)

    @pl.when(t < nu_ref[0])
    def _():
        @pl.when(_expert_changed(te_ref, t))
        def _():
            wgb_ref[...] = wg_ref[...].astype(BF16)
            wub_ref[...] = wu_ref[...].astype(BF16)

        x = x_ref[...].astype(BF16)
        gate = _dot(x, wgb_ref[...])
        up = _dot(x, wub_ref[...])
        h_ref[...] = (gate * jax.nn.sigmoid(gate) * up).astype(h_ref.dtype)

    @pl.when(t >= nu_ref[0])
    def _():
        h_ref[...] = jnp.zeros(h_ref.shape, h_ref.dtype)


def _moe_down_kernel(te_ref, nu_ref, h_ref, wd_ref, y_ref, wdb_ref):
    t = pl.program_id(1)

    @pl.when(t < nu_ref[0])
    def _():
        @pl.when(_expert_changed(te_ref, t))
        def _():
            wdb_ref[...] = wd_ref[...].astype(BF16)

        y_ref[...] = _dot(h_ref[...], wdb_ref[...])

    @pl.when(t >= nu_ref[0])
    def _():
        y_ref[...] = jnp.zeros(y_ref.shape, y_ref.dtype)


def _moe_experts(xs, tile_e, n_used, w_gate, w_up, w_down, layer):
    cap, d = xs.shape
    de = w_gate.shape[-1]
    n_tiles = cap // MOE_TILE
    fa = _tile(de, 512)
    nb = _tile(d, 1024)

    def live(t, nu):
        return jnp.minimum(t, nu[0] - 1)

    hid = pl.pallas_call(
        _moe_up_kernel,
        grid_spec=pltpu.PrefetchScalarGridSpec(
            num_scalar_prefetch=2,
            grid=(de // fa, n_tiles),
            in_specs=[pl.BlockSpec((MOE_TILE, d), lambda f, t, te, nu: (live(t, nu), 0)),
                      pl.BlockSpec((None, None, d, fa), lambda f, t, te, nu: (layer, te[t], 0, f)),
                      pl.BlockSpec((None, None, d, fa), lambda f, t, te, nu: (layer, te[t], 0, f))],
            out_specs=pl.BlockSpec((MOE_TILE, fa), lambda f, t, te, nu: (t, f)),
            scratch_shapes=[pltpu.VMEM((d, fa), BF16), pltpu.VMEM((d, fa), BF16)]),
        out_shape=jax.ShapeDtypeStruct((cap, de), BF16),
        compiler_params=_cparams("arbitrary", "arbitrary"),
        name="moe_up",
    )(tile_e, n_used, xs, w_gate, w_up)

    return pl.pallas_call(
        _moe_down_kernel,
        grid_spec=pltpu.PrefetchScalarGridSpec(
            num_scalar_prefetch=2,
            grid=(d // nb, n_tiles),
            in_specs=[pl.BlockSpec((MOE_TILE, de), lambda n, t, te, nu: (live(t, nu), 0)),
                      pl.BlockSpec((None, None, de, nb), lambda n, t, te, nu: (layer, te[t], 0, n))],
            out_specs=pl.BlockSpec((MOE_TILE, nb), lambda n, t, te, nu: (t, n)),
            scratch_shapes=[pltpu.VMEM((de, nb), BF16)]),
        out_shape=jax.ShapeDtypeStruct((cap, d), F32),
        compiler_params=_cparams("arbitrary", "arbitrary"),
        name="moe_down",
    )(tile_e, n_used, hid, w_down)


def _moe_ln_kernel(x_ref, y0_ref, y1_ref, r_ref, g_ref, b_ref, o_ref, *, alpha):
    r = r_ref[...]
    moe = r[:, TOP_K:TOP_K + 1] * y0_ref[...] + r[:, TOP_K + 1:TOP_K + 2] * y1_ref[...]
    o_ref[...] = _layer_norm_rows(alpha * x_ref[...] + moe, g_ref[...], b_ref[...])


def _moe_ln(x, y2, routed, g, b, alpha):
    m, d = x.shape
    tm = _tile(m, 512)
    nt = m // tm
    row = lambda i: (i, 0)
    fixed = lambda i: (0, 0)
    return pl.pallas_call(
        functools.partial(_moe_ln_kernel, alpha=alpha),
        grid=(nt,),
        in_specs=[pl.BlockSpec((tm, d), row), pl.BlockSpec((tm, d), row),
                  pl.BlockSpec((tm, d), lambda i: (nt + i, 0)), pl.BlockSpec((tm, LANES), row),
                  pl.BlockSpec((1, d), fixed), pl.BlockSpec((1, d), fixed)],
        out_specs=pl.BlockSpec((tm, d), row),
        out_shape=jax.ShapeDtypeStruct((m, d), F32),
        compiler_params=_cparams("parallel"),
        name="moe_ln",
    )(x, y2, y2, routed, g, b)


def _moe_layer(x, layer, w_rg, b_rg, w_re, b_re, w_gate, w_up, w_down, ln_g, ln_b, alpha):
    m, d = x.shape
    wr = jnp.concatenate([w_rg, w_re], axis=1)
    wr = jnp.pad(wr, ((0, 0), (0, LANES - wr.shape[1])))
    wr_hi = wr.astype(BF16)
    wr_lo = (wr - wr_hi.astype(F32)).astype(BF16)
    br = jnp.pad(jnp.concatenate([b_rg, b_re]), (0, LANES - N_GROUPS - N_EXPERTS))[None, :]
    routed = _router(x, wr_hi, wr_lo, br)
    ids = routed[:, 0:TOP_K].astype(jnp.int32)

    n_assign = m * TOP_K
    n_tiles = -(-n_assign // MOE_TILE) + N_EXPERTS
    cap = n_tiles * MOE_TILE
    flat_e = ids.reshape(-1)
    onehot = (flat_e[:, None] == jnp.arange(N_EXPERTS, dtype=jnp.int32)[None, :]).astype(jnp.int32)
    csum = jnp.cumsum(onehot, axis=0)
    counts = csum[-1]
    rank = jnp.take_along_axis(csum, flat_e[:, None], axis=1)[:, 0] - 1
    tiles_per = (counts + MOE_TILE - 1) // MOE_TILE
    tile_ends = jnp.cumsum(tiles_per)
    pad_starts = (tile_ends - tiles_per) * MOE_TILE
    dest = pad_starts[flat_e] + rank
    n_used = tile_ends[-1:].astype(jnp.int32)
    tile_idx = jnp.arange(n_tiles, dtype=jnp.int32)
    tile_e = jnp.sum((tile_ends[None, :] <= tile_idx[:, None]).astype(jnp.int32), axis=1)
    last_e = jnp.max(jnp.where(counts > 0, jnp.arange(N_EXPERTS), 0))
    tile_e = jnp.minimum(tile_e, last_e).astype(jnp.int32)
    tok = jnp.arange(n_assign, dtype=jnp.int32) // TOP_K
    slot_tok = (jnp.arange(cap, dtype=jnp.int32) % m).at[dest].set(tok)

    xs = x[slot_tok]
    y = _moe_experts(xs, tile_e, n_used, w_gate, w_up, w_down, layer)
    y2 = y[dest.reshape(m, TOP_K).T.reshape(-1)]
    return _moe_ln(x, y2, routed, ln_g[None, :], ln_b[None, :], alpha)


def _rope_tables(pos):
    half = ROPE_DIM // 2
    inv_freq = ROPE_THETA ** (-jnp.arange(half, dtype=F32) / half)
    ang = pos.astype(F32)[:, None] * inv_freq
    cos, sin = jnp.cos(ang), jnp.sin(ang)
    z = jnp.zeros_like(cos)
    pad = jnp.zeros((pos.shape[0], LANES - ROPE_DIM), F32)
    c = jnp.concatenate([cos, cos, pad], axis=1)
    sa = jnp.concatenate([-sin, z, pad], axis=1)
    sb = jnp.concatenate([z, sin, pad], axis=1)
    return c, sa, sb


def _hybrid_layer(x, dims, rope, conv_state, cache_lat, cache_pe, page_table, j,
                  w_in, conv_w, q_g, kv_g, w_q_up, w_uk, w_uv, w_out, ln_g, ln_b, alpha):
    bp, sp, bs, ts = dims
    n_p = bp * sp
    cw = conv_w.shape[1]
    heads = w_q_up.shape[1]
    n_main = 3 * cw + Q_LORA + KV_LORA

    w_pe = jnp.pad(w_in[j, :, n_main:], ((0, 0), (0, LANES - ROPE_DIM))).astype(BF16)
    h = _mm(x, w_in, layer=j, n=n_main, name="hyb_in")
    h_pe = _mm(x, w_pe, name="hyb_in_pe")

    a_p, conv_p = _conv_prompt(h, conv_w, bp, sp, cw)
    a_s, conv_s = _conv_sample(h, conv_state, conv_w, n_p, bs, ts, cw)

    w_q_cat = jnp.pad(w_q_up, ((0, 0), (0, 0), (0, HEAD_PAD - NOPE_DIM - ROPE_DIM)))
    w_q_cat = w_q_cat.reshape(Q_LORA, heads * HEAD_PAD).astype(BF16)
    q_cat = _q_proj(h, 3 * cw // Q_LORA, q_g[None, :], w_q_cat, *rope, heads)
    lat, kpe = _kv_norm(h, (3 * cw + Q_LORA) // KV_LORA, h_pe, kv_g[None, :], *rope)
    w_uk2 = w_uk.reshape(KV_LORA, heads * NOPE_DIM).astype(BF16)
    w_uv2 = w_uv.reshape(KV_LORA, heads * V_DIM).astype(BF16)
    w_uv_t = w_uv2.T

    k_cat, v_t = _kv_expand(lat, kpe, w_uk2, w_uv_t, n_p, heads)
    o_p = _attn_prompt(q_cat, k_cat, v_t, bp, sp, heads)

    n_s = bs * ts
    q_lat = _q_latent(q_cat, w_uk2, n_p, n_s, heads)
    q_lat = q_lat.reshape(heads, bs, ts, KV_LORA).transpose(1, 0, 2, 3).reshape(bs, heads * ts, KV_LORA)
    q_pe = q_cat[n_p:].reshape(bs, ts, heads, HEAD_PAD)[..., NOPE_DIM:NOPE_DIM + ROPE_DIM]
    q_pe = q_pe.transpose(0, 2, 1, 3).reshape(bs, heads * ts, ROPE_DIM)
    lat_s = lat[n_p:].reshape(bs, ts, KV_LORA)
    kpe_s = kpe[n_p:, :ROPE_DIM].reshape(bs, ts, ROPE_DIM)
    new_lat = jnp.pad(lat_s, ((0, 0), (0, ts), (0, 0)))
    new_pe = jnp.pad(kpe_s, ((0, 0), (0, ts), (0, 0)))
    o_lat = _attn_sample(page_table, q_lat, q_pe, cache_lat, cache_pe, j, new_lat, new_pe, ts)
    o_lat = o_lat.reshape(bs, heads, ts, KV_LORA).transpose(1, 0, 2, 3).reshape(heads, n_s, KV_LORA)
    o_s = _o_from_latent(o_lat, w_uv2, heads)

    y = _hyb_out(a_p, a_s, o_p, o_s, w_out.astype(BF16), x, ln_g[None, :], ln_b[None, :], alpha)
    outs = (lat[:n_p].reshape(bp, sp, KV_LORA), kpe[:n_p, :ROPE_DIM].reshape(bp, sp, ROPE_DIM),
            lat_s, kpe_s, conv_p, conv_s)
    return y, outs


def _rec_layer(x, dims, state, j, lb, w_in, norm_g, w_out, ln_g, ln_b, alpha):
    bp, sp, bs, ts = dims
    n_p = bp * sp
    width = w_in.shape[-1] // 4
    heads = width // REC_DK
    h = _mm(x, w_in, layer=j, name="rec_in")
    lb2 = lb[None, :]
    o_p, st_p = _gla_prompt(h, lb2, bp, sp, heads)
    o_s, st_s = _gla_sample(h, lb2, state, j, n_p, bs, ts, heads)
    y = _rec_out(o_p, o_s, h, 3, norm_g[None, :], w_out.astype(BF16), x, ln_g[None, :], ln_b[None, :], alpha)
    return y, (jnp.swapaxes(st_p, -1, -2), st_s)


def kernel(x_prompt, x_sample, cache_kv_latent, cache_k_rope, state_conv, state_hgrn, page_table,
           w_in_hyb, conv_w, q_norm_g, kv_norm_g, w_q_up, w_uk, w_uv, w_out_hyb,
           w_in_rec, lb_logits, rec_norm_g, w_out_rec,
           ln1_g, ln1_b, ln2_g, ln2_b,
           w_router_group, b_router_group, w_router_expert, b_router_expert,
           w_gate, w_up, w_down):
    bp, sp, d = x_prompt.shape
    bs, ts, _ = x_sample.shape
    depth = ln1_g.shape[0]
    past_len = page_table.shape[1] * cache_kv_latent.shape[2]
    dims = (bp, sp, bs, ts)
    n_p = bp * sp
    alpha = (2 * depth) ** 0.25

    lower = jnp.cumsum(jax.nn.softmax(lb_logits.astype(F32), axis=0), axis=0)
    lower = lower - lower[0]

    pos = jnp.concatenate([jnp.tile(jnp.arange(sp), bp), jnp.tile(past_len + jnp.arange(ts), bs)])
    rope = _rope_tables(pos)

    x = jnp.concatenate([x_prompt.reshape(n_p, d), x_sample.reshape(bs * ts, d)], axis=0)
    cache_pe_t = jnp.swapaxes(cache_k_rope, 2, 3)
    hyb_outs, rec_outs = [], []
    for layer in range(depth):
        j = layer // 2
        if layer % 2 == 0:
            x, outs = _hybrid_layer(x, dims, rope, state_conv[j], cache_kv_latent, cache_pe_t, page_table, j,
                                    w_in_hyb, conv_w[j], q_norm_g[j], kv_norm_g[j], w_q_up[j], w_uk[j],
                                    w_uv[j], w_out_hyb[j], ln1_g[layer], ln1_b[layer], alpha)
            hyb_outs.append(outs)
        else:
            x, outs = _rec_layer(x, dims, state_hgrn, j, lower[layer], w_in_rec, rec_norm_g[j],
                                 w_out_rec[j], ln1_g[layer], ln1_b[layer], alpha)
            rec_outs.append(outs)
        x = _moe_layer(x, layer, w_router_group[layer], b_router_group[layer], w_router_expert[layer],
                       b_router_expert[layer], w_gate, w_up, w_down, ln2_g[layer], ln2_b[layer], alpha)

    stack = lambda parts, i: jnp.stack([p[i] for p in parts])
    return (x[:n_p].reshape(bp, sp, d), x[n_p:].reshape(bs, ts, d),
            stack(hyb_outs, 0), stack(hyb_outs, 1), stack(hyb_outs, 2), stack(hyb_outs, 3),
            stack(hyb_outs, 4), stack(hyb_outs, 5), stack(rec_outs, 0), stack(rec_outs, 1))
```

```python
import functools
import math

import numpy as np
import jax
import jax.numpy as jnp
from jax import lax
from jax.experimental import pallas as pl
from jax.experimental.pallas import tpu as pltpu

F32 = jnp.float32
BF16 = jnp.bfloat16

CONV_K = 3
Q_LORA = 512
KV_LORA = 512
NOPE_DIM = 128
ROPE_DIM = 64
V_DIM = 128
REC_DK = 128
N_GROUPS = 4
EXPERTS_PER_GROUP = 8
N_EXPERTS = N_GROUPS * EXPERTS_PER_GROUP
TOP_K = 2
ROPE_THETA = 10000.0
LN_EPS = 1e-5
RMS_EPS = 1e-6
MLA_SCALE = (NOPE_DIM + ROPE_DIM) ** -0.5

LANES = 128
SUBLANES = 8
VMEM_LIMIT_BYTES = 52 * 1024 * 1024

HEAD_PAD = 2 * LANES
GLA_CHUNK = 64
GLA_HEADS_PER_STEP = 8
MOE_TILE = 256
ATTN_TQ = 512
PAGES_PER_STEP = 16


def _cparams(*sem):
    return pltpu.CompilerParams(dimension_semantics=sem, vmem_limit_bytes=VMEM_LIMIT_BYTES)


def _tile(n, pref):
    if n <= pref:
        return n
    for t in range(pref, 7, -1):
        if n % t == 0 and t % 8 == 0:
            return t
    return n


def _dot(a, b):
    return jnp.dot(a, b, preferred_element_type=F32)


def _dot_nt(a, b):
    return lax.dot_general(a, b, (((1,), (1,)), ((), ())), preferred_element_type=F32)


def _dot_tn(a, b):
    return lax.dot_general(a, b, (((0,), (0,)), ((), ())), preferred_element_type=F32)


def _mm_kernel(x_ref, w_ref, o_ref, xb_ref):
    @pl.when(pl.program_id(1) == 0)
    def _():
        xb_ref[...] = x_ref[...].astype(BF16)

    o_ref[...] = _dot(xb_ref[...], w_ref[...].astype(BF16)).astype(o_ref.dtype)


def _mm(x, w, *, layer=None, n=None, tm=1024, tn=512, out_dtype=F32, name="mm"):
    m, k = x.shape
    n = w.shape[-1] if n is None else n
    tm = _tile(m, tm)
    tn = _tile(n, tn)
    if layer is None:
        w_spec = pl.BlockSpec((k, tn), lambda i, j: (0, j))
    else:
        w_spec = pl.BlockSpec((None, k, tn), lambda i, j: (layer, 0, j))
    return pl.pallas_call(
        _mm_kernel,
        grid=(m // tm, n // tn),
        in_specs=[pl.BlockSpec((tm, k), lambda i, j: (i, 0)), w_spec],
        out_specs=pl.BlockSpec((tm, tn), lambda i, j: (i, j)),
        out_shape=jax.ShapeDtypeStruct((m, n), out_dtype),
        scratch_shapes=[pltpu.VMEM((tm, k), BF16)],
        compiler_params=_cparams("parallel", "arbitrary"),
        name=name,
    )(x, w)


def _layer_norm_rows(z, g, b):
    mu = jnp.mean(z, axis=-1, keepdims=True)
    zc = z - mu
    var = jnp.mean(zc * zc, axis=-1, keepdims=True)
    return zc * lax.rsqrt(var + LN_EPS) * g + b


def _rms_norm_rows(z, g):
    return z * lax.rsqrt(jnp.mean(z * z, axis=-1, keepdims=True) + RMS_EPS) * g


def _conv_prompt_kernel(cx_ref, gb_ref, gc_ref, cxp_ref, gcp_ref, w_ref, a_ref, st_ref, *, tt):
    i = pl.program_id(1)
    u = gc_ref[...] * cx_ref[...]
    prev = jnp.where(i > 0, gcp_ref[...] * cxp_ref[...], 0.0)
    row = lax.broadcasted_iota(jnp.int32, u.shape, 0)
    u1 = jnp.where(row == 0, prev[7:8], pltpu.roll(u, 1, 0))
    u2 = jnp.where(row == 0, prev[6:7], jnp.where(row == 1, prev[7:8], pltpu.roll(u, 2, 0)))
    v = w_ref[0:1, :] * u2 + w_ref[1:2, :] * u1 + w_ref[2:3, :] * u
    a_ref[...] = gb_ref[...] * v

    @pl.when(i == pl.num_programs(1) - 1)
    def _():
        st_ref[...] = u[tt - (CONV_K - 1):tt]


def _conv_prompt(h, conv_w, bsz, seq, cw):
    tt = _tile(seq, 512)
    nt = seq // tt
    sub = tt // SUBLANES

    def main(c):
        return pl.BlockSpec((tt, cw), lambda b, i: (b * nt + i, c))

    def halo(c):
        return pl.BlockSpec((SUBLANES, cw), lambda b, i: (jnp.maximum((b * nt + i) * sub - 1, 0), c))

    return pl.pallas_call(
        functools.partial(_conv_prompt_kernel, tt=tt),
        grid=(bsz, nt),
        in_specs=[main(0), main(1), main(2), halo(0), halo(2),
                  pl.BlockSpec((CONV_K, cw), lambda b, i: (0, 0))],
        out_specs=[pl.BlockSpec((tt, cw), lambda b, i: (b * nt + i, 0)),
                   pl.BlockSpec((None, CONV_K - 1, cw), lambda b, i: (b, 0, 0))],
        out_shape=[jax.ShapeDtypeStruct((bsz * seq, cw), F32),
                   jax.ShapeDtypeStruct((bsz, CONV_K - 1, cw), F32)],
        compiler_params=_cparams("parallel", "arbitrary"),
        name="conv_prompt",
    )(h, h, h, h, h, conv_w)


def _conv_sample_kernel(cx_ref, gb_ref, gc_ref, st_ref, w_ref, a_ref, sto_ref, *, nb, ts):
    cw = cx_ref.shape[-1]
    u = (gc_ref[...] * cx_ref[...]).reshape(nb, ts, cw)
    st = st_ref[...]
    t = lax.broadcasted_iota(jnp.int32, u.shape, 1)
    s0 = st[:, 0:1, :]
    s1 = st[:, 1:2, :]
    u1 = jnp.where(t == 0, s1, pltpu.roll(u, 1, 1))
    u2 = jnp.where(t == 0, s0, jnp.where(t == 1, s1, pltpu.roll(u, 2, 1)))
    w = w_ref[...]
    v = w[0:1, :][None] * u2 + w[1:2, :][None] * u1 + w[2:3, :][None] * u
    a_ref[...] = gb_ref[...] * v.reshape(nb * ts, cw)
    sto_ref[...] = u[:, ts - (CONV_K - 1):ts, :]


def _conv_sample(h, state, conv_w, row0, bsz, ts, cw):
    nb = _tile(bsz, 64)
    rows = nb * ts
    off = row0 // rows

    def main(c):
        return pl.BlockSpec((rows, cw), lambda i: (off + i, c))

    return pl.pallas_call(
        functools.partial(_conv_sample_kernel, nb=nb, ts=ts),
        grid=(bsz // nb,),
        in_specs=[main(0), main(1), main(2),
                  pl.BlockSpec((nb, CONV_K - 1, cw), lambda i: (i, 0, 0)),
                  pl.BlockSpec((CONV_K, cw), lambda i: (0, 0))],
        out_specs=[pl.BlockSpec((rows, cw), lambda i: (i, 0)),
                   pl.BlockSpec((nb, CONV_K - 1, cw), lambda i: (i, 0, 0))],
        out_shape=[jax.ShapeDtypeStruct((bsz * ts, cw), F32),
                   jax.ShapeDtypeStruct((bsz, CONV_K - 1, cw), F32)],
        compiler_params=_cparams("parallel"),
        name="conv_sample",
    )(h, h, h, state, conv_w)


def _rope_lanes(x, c, sa, sb):
    half = ROPE_DIM // 2
    return x * c + pltpu.roll(x, LANES - half, 1) * sa + pltpu.roll(x, half, 1) * sb


def _q_proj_kernel(qc_ref, g_ref, w_ref, c_ref, sa_ref, sb_ref, o_ref, *, heads):
    xn = _rms_norm_rows(qc_ref[...], g_ref[...]).astype(BF16)
    q = _dot(xn, w_ref[...]) * MLA_SCALE
    c, sa, sb = c_ref[...], sa_ref[...], sb_ref[...]
    for hd in range(heads):
        lo = hd * HEAD_PAD
        o_ref[:, lo:lo + LANES] = q[:, lo:lo + LANES].astype(o_ref.dtype)
        pe = _rope_lanes(q[:, lo + LANES:lo + HEAD_PAD], c, sa, sb)
        o_ref[:, lo + LANES:lo + HEAD_PAD] = pe.astype(o_ref.dtype)


def _q_proj(h, col_block, g, w_cat, rope_c, rope_sa, rope_sb, heads):
    m = h.shape[0]
    tm = _tile(m, 512)
    n = heads * HEAD_PAD
    row = lambda i: (i, 0)
    return pl.pallas_call(
        functools.partial(_q_proj_kernel, heads=heads),
        grid=(m // tm,),
        in_specs=[pl.BlockSpec((tm, Q_LORA), lambda i: (i, col_block)),
                  pl.BlockSpec((1, Q_LORA), lambda i: (0, 0)),
                  pl.BlockSpec((Q_LORA, n), lambda i: (0, 0)),
                  pl.BlockSpec((tm, LANES), row), pl.BlockSpec((tm, LANES), row),
                  pl.BlockSpec((tm, LANES), row)],
        out_specs=pl.BlockSpec((tm, n), row),
        out_shape=jax.ShapeDtypeStruct((m, n), BF16),
        compiler_params=_cparams("parallel"),
        name="q_proj",
    )(h, g, w_cat, rope_c, rope_sa, rope_sb)


def _kv_norm_kernel(kv_ref, pe_ref, g_ref, c_ref, sa_ref, sb_ref, lat_ref, kpe_ref):
    lat_ref[...] = _rms_norm_rows(kv_ref[...], g_ref[...])
    kpe_ref[...] = _rope_lanes(pe_ref[...], c_ref[...], sa_ref[...], sb_ref[...])


def _kv_norm(h, col_block, h_pe, g, rope_c, rope_sa, rope_sb):
    m = h.shape[0]
    tm = _tile(m, 1024)
    row = lambda i: (i, 0)
    return pl.pallas_call(
        _kv_norm_kernel,
        grid=(m // tm,),
        in_specs=[pl.BlockSpec((tm, KV_LORA), lambda i: (i, col_block)),
                  pl.BlockSpec((tm, LANES), row),
                  pl.BlockSpec((1, KV_LORA), lambda i: (0, 0)),
                  pl.BlockSpec((tm, LANES), row), pl.BlockSpec((tm, LANES), row),
                  pl.BlockSpec((tm, LANES), row)],
        out_specs=[pl.BlockSpec((tm, KV_LORA), row), pl.BlockSpec((tm, LANES), row)],
        out_shape=[jax.ShapeDtypeStruct((m, KV_LORA), F32),
                   jax.ShapeDtypeStruct((m, LANES), F32)],
        compiler_params=_cparams("parallel"),
        name="kv_norm",
    )(h, h_pe, g, rope_c, rope_sa, rope_sb)


def _kv_expand_kernel(lat_ref, kpe_ref, wk_ref, wvt_ref, k_ref, vt_ref, *, heads):
    lat = lat_ref[...].astype(BF16)
    kn = _dot(lat, wk_ref[...])
    kpe = kpe_ref[...].astype(k_ref.dtype)
    for hd in range(heads):
        lo = hd * HEAD_PAD
        k_ref[:, lo:lo + LANES] = kn[:, hd * NOPE_DIM:(hd + 1) * NOPE_DIM].astype(k_ref.dtype)
        k_ref[:, lo + LANES:lo + HEAD_PAD] = kpe
    vt_ref[...] = _dot_nt(wvt_ref[...], lat).astype(vt_ref.dtype)


def _kv_expand(lat, kpe, w_uk, w_uv_t, rows, heads):
    tm = _tile(rows, 512)
    row = lambda i: (i, 0)
    fixed = lambda i: (0, 0)
    return pl.pallas_call(
        functools.partial(_kv_expand_kernel, heads=heads),
        grid=(rows // tm,),
        in_specs=[pl.BlockSpec((tm, KV_LORA), row), pl.BlockSpec((tm, LANES), row),
                  pl.BlockSpec((KV_LORA, heads * NOPE_DIM), fixed),
                  pl.BlockSpec((heads * V_DIM, KV_LORA), fixed)],
        out_specs=[pl.BlockSpec((tm, heads * HEAD_PAD), row),
                   pl.BlockSpec((heads * V_DIM, tm), lambda i: (0, i))],
        out_shape=[jax.ShapeDtypeStruct((rows, heads * HEAD_PAD), BF16),
                   jax.ShapeDtypeStruct((heads * V_DIM, rows), BF16)],
        compiler_params=_cparams("parallel"),
        name="kv_expand",
    )(lat, kpe, w_uk, w_uv_t)


def _attn_prompt_kernel(q_ref, k_ref, vt_ref, o_ref, m_ref, l_ref, acc_ref, *, tq):
    i = pl.program_id(2)
    m_ref[...] = jnp.full(m_ref.shape, -jnp.inf, F32)
    l_ref[...] = jnp.zeros(l_ref.shape, F32)
    acc_ref[...] = jnp.zeros(acc_ref.shape, F32)
    q = q_ref[...]

    def chunk(c, masked):
        r0 = pl.multiple_of(c * tq, tq)
        s = _dot_nt(k_ref[pl.ds(r0, tq), :], q)
        if masked:
            key = lax.broadcasted_iota(jnp.int32, s.shape, 0)
            qry = lax.broadcasted_iota(jnp.int32, s.shape, 1)
            s = jnp.where(key <= qry, s, -jnp.inf)
        m_old = m_ref[...]
        m_new = jnp.maximum(m_old, jnp.max(s, axis=0, keepdims=True))
        alpha = jnp.exp(m_old - m_new)
        p = jnp.exp(s - m_new)
        l_ref[...] = alpha * l_ref[...] + jnp.sum(p, axis=0, keepdims=True)
        acc_ref[...] = alpha * acc_ref[...] + _dot(vt_ref[:, pl.ds(r0, tq)], p.astype(BF16))
        m_ref[...] = m_new

    def body(c, carry):
        chunk(c, False)
        return carry

    lax.fori_loop(0, i, body, 0)
    chunk(i, True)
    o_ref[...] = (acc_ref[...] / l_ref[...]).T.astype(o_ref.dtype)


def _attn_prompt(q_cat, k_cat, v_t, bsz, seq, heads):
    tq = _tile(seq, ATTN_TQ)
    nq = seq // tq
    return pl.pallas_call(
        functools.partial(_attn_prompt_kernel, tq=tq),
        grid=(bsz, heads, nq),
        in_specs=[pl.BlockSpec((tq, HEAD_PAD), lambda b, h, i: (b * nq + i, h)),
                  pl.BlockSpec((seq, HEAD_PAD), lambda b, h, i: (b, h)),
                  pl.BlockSpec((V_DIM, seq), lambda b, h, i: (h, b))],
        out_specs=pl.BlockSpec((tq, V_DIM), lambda b, h, i: (b * nq + i, h)),
        out_shape=jax.ShapeDtypeStruct((bsz * seq, heads * V_DIM), F32),
        scratch_shapes=[pltpu.VMEM((1, tq), F32), pltpu.VMEM((1, tq), F32),
                        pltpu.VMEM((V_DIM, tq), F32)],
        compiler_params=_cparams("parallel", "parallel", "arbitrary"),
        name="attn_prompt",
    )(q_cat, k_cat, v_t)


def _head_mm_kernel(x_ref, w_ref, o_ref, *, nt):
    x = x_ref[...].astype(BF16)
    o_ref[...] = (_dot_nt(x, w_ref[...]) if nt else _dot(x, w_ref[...])).astype(o_ref.dtype)


def _q_latent(q_cat, w_uk, row0, rows, heads):
    off = row0 // rows
    return pl.pallas_call(
        functools.partial(_head_mm_kernel, nt=True),
        grid=(heads,),
        in_specs=[pl.BlockSpec((rows, NOPE_DIM), lambda h: (off, 2 * h)),
                  pl.BlockSpec((KV_LORA, NOPE_DIM), lambda h: (0, h))],
        out_specs=pl.BlockSpec((None, rows, KV_LORA), lambda h: (h, 0, 0)),
        out_shape=jax.ShapeDtypeStruct((heads, rows, KV_LORA), BF16),
        compiler_params=_cparams("parallel"),
        name="q_latent",
    )(q_cat, w_uk)


def _o_from_latent(o_lat, w_uv, heads):
    rows = o_lat.shape[1]
    return pl.pallas_call(
        functools.partial(_head_mm_kernel, nt=False),
        grid=(heads,),
        in_specs=[pl.BlockSpec((None, rows, KV_LORA), lambda h: (h, 0, 0)),
                  pl.BlockSpec((KV_LORA, V_DIM), lambda h: (0, h))],
        out_specs=pl.BlockSpec((rows, V_DIM), lambda h: (0, h)),
        out_shape=jax.ShapeDtypeStruct((rows, heads * V_DIM), F32),
        compiler_params=_cparams("parallel"),
        name="o_from_latent",
    )(o_lat, w_uv)


def _lane_chunks(s):
    w = s.shape[-1]
    if w % LANES or w == LANES:
        return [s]
    return [s[:, c * LANES:(c + 1) * LANES] for c in range(w // LANES)]


def _row_max(s):
    parts = _lane_chunks(s)
    return jnp.max(functools.reduce(jnp.maximum, parts), axis=-1, keepdims=True)


def _row_sum(s):
    parts = _lane_chunks(s)
    return jnp.sum(functools.reduce(jnp.add, parts), axis=-1, keepdims=True)


def _attn_sample_kernel(pt_ref, ql_ref, qp_ref, lat_hbm, pe_hbm, nl_ref, np_ref, o_ref,
                        lat_buf, pe_buf, lat_sem, pe_sem, k_ref, pe_ref, m_ref, l_ref, acc_ref,
                        *, layer, npg, ts, page):
    s_idx = pl.program_id(1)
    n_groups = pl.num_programs(1)
    lin = pl.program_id(0) * n_groups + s_idx
    slot = lin % 2

    def page_copies(group, slot_, j):
        pid = pt_ref[group * npg + j]
        rows = pl.ds(j * page, page)
        return (pltpu.make_async_copy(lat_hbm.at[layer, pid], lat_buf.at[slot_, rows, :], lat_sem.at[slot_]),
                pltpu.make_async_copy(pe_hbm.at[layer, pid], pe_buf.at[slot_, j], pe_sem.at[slot_]))

    def start_group(group, slot_):
        for j in range(npg):
            for cp in page_copies(group, slot_, j):
                cp.start()

    @pl.when(lin == 0)
    def _():
        start_group(lin, slot)

    @pl.when(lin + 1 < pl.num_programs(0) * n_groups)
    def _():
        start_group(lin + 1, 1 - slot)

    for j in range(npg):
        for cp in page_copies(lin, slot, j):
            cp.wait()

    @pl.when(s_idx == 0)
    def _():
        m_ref[...] = jnp.full(m_ref.shape, -jnp.inf, F32)
        l_ref[...] = jnp.zeros(l_ref.shape, F32)
        acc_ref[...] = jnp.zeros(acc_ref.shape, F32)

    ql = ql_ref[...]
    qp = qp_ref[...]

    def update(s, values):
        m_old = m_ref[...]
        m_new = jnp.maximum(m_old, _row_max(s))
        alpha = jnp.exp(m_old - m_new)
        p = jnp.exp(s - m_new)
        l_ref[...] = alpha * l_ref[...] + _row_sum(p)
        acc_ref[...] = alpha * acc_ref[...] + _dot(p.astype(BF16), values)
        m_ref[...] = m_new

    k_ref[...] = lat_buf[slot].astype(BF16)
    for j in range(npg):
        pe_ref[:, j * page:(j + 1) * page] = pe_buf[slot, j].astype(BF16)
    keys = k_ref[...]
    update(_dot_nt(ql, keys) + _dot(qp, pe_ref[...]), keys)

    @pl.when(s_idx == pl.num_programs(1) - 1)
    def _():
        kn = nl_ref[...].astype(BF16)
        s = _dot_nt(ql, kn) + _dot_nt(qp, np_ref[...].astype(BF16))
        t_q = lax.broadcasted_iota(jnp.int32, s.shape, 0) % ts
        t_k = lax.broadcasted_iota(jnp.int32, s.shape, 1)
        update(jnp.where(t_k <= t_q, s, -jnp.inf), kn)
        o_ref[...] = (acc_ref[...] / l_ref[...]).astype(o_ref.dtype)


def _attn_sample(page_table, q_lat, q_pe, cache_lat, cache_pe_t, layer, new_lat, new_pe, ts):
    bsz, qrows, _ = q_lat.shape
    n_pages = page_table.shape[1]
    page = cache_lat.shape[2]
    npg = math.gcd(PAGES_PER_STEP, n_pages)
    steps = n_pages // npg
    pt_flat = page_table.reshape(-1)

    per_b = lambda b, s, pt: (b, 0, 0)
    keys = npg * page
    grid_spec = pltpu.PrefetchScalarGridSpec(
        num_scalar_prefetch=1,
        grid=(bsz, steps),
        in_specs=[pl.BlockSpec((None, qrows, KV_LORA), per_b),
                  pl.BlockSpec((None, qrows, ROPE_DIM), per_b),
                  pl.BlockSpec(memory_space=pl.ANY), pl.BlockSpec(memory_space=pl.ANY),
                  pl.BlockSpec((None, 2 * ts, KV_LORA), per_b),
                  pl.BlockSpec((None, 2 * ts, ROPE_DIM), per_b)],
        out_specs=pl.BlockSpec((None, qrows, KV_LORA), per_b),
        scratch_shapes=[pltpu.VMEM((2, keys, KV_LORA), F32), pltpu.VMEM((2, npg, ROPE_DIM, page), F32),
                        pltpu.SemaphoreType.DMA((2,)), pltpu.SemaphoreType.DMA((2,)),
                        pltpu.VMEM((keys, KV_LORA), BF16), pltpu.VMEM((ROPE_DIM, keys), BF16),
                        pltpu.VMEM((qrows, 1), F32), pltpu.VMEM((qrows, 1), F32),
                        pltpu.VMEM((qrows, KV_LORA), F32)],
    )
    return pl.pallas_call(
        functools.partial(_attn_sample_kernel, layer=layer, npg=npg, ts=ts, page=page),
        grid_spec=grid_spec,
        out_shape=jax.ShapeDtypeStruct((bsz, qrows, KV_LORA), BF16),
        compiler_params=_cparams("arbitrary", "arbitrary"),
        name="attn_sample",
    )(pt_flat, q_lat, q_pe, cache_lat, cache_pe_t, new_lat, new_pe)


def _two_part_specs(tm, width, nt_p):
    return (pl.BlockSpec((tm, width), lambda i: (jnp.minimum(i, nt_p - 1), 0)),
            pl.BlockSpec((tm, width), lambda i: (jnp.maximum(i - nt_p, 0), 0)))


def _pick_part(nt_p, p_ref, s_ref):
    return jnp.where(pl.program_id(0) < nt_p, p_ref[...], s_ref[...])


def _hyb_out_kernel(ap_ref, as_ref, op_ref, os_ref, w_ref, x_ref, g_ref, b_ref, y_ref, *, alpha, nt_p):
    a = _pick_part(nt_p, ap_ref, as_ref).astype(BF16)
    o = _pick_part(nt_p, op_ref, os_ref).astype(BF16)
    ka = a.shape[1]
    mix = _dot(a, w_ref[0:ka, :]) + _dot(o, w_ref[ka:, :])
    y_ref[...] = _layer_norm_rows(alpha * x_ref[...] + mix, g_ref[...], b_ref[...])


def _hyb_out(a_p, a_s, o_p, o_s, w_out, x, g, b, alpha):
    m, d = x.shape
    n_p, n_s = a_p.shape[0], a_s.shape[0]
    tm = _tile(math.gcd(n_p, n_s), 256)
    nt_p = n_p // tm
    row = lambda i: (i, 0)
    fixed = lambda i: (0, 0)
    return pl.pallas_call(
        functools.partial(_hyb_out_kernel, alpha=alpha, nt_p=nt_p),
        grid=(m // tm,),
        in_specs=[*_two_part_specs(tm, a_p.shape[1], nt_p), *_two_part_specs(tm, o_p.shape[1], nt_p),
                  pl.BlockSpec(w_out.shape, fixed), pl.BlockSpec((tm, d), row),
                  pl.BlockSpec((1, d), fixed), pl.BlockSpec((1, d), fixed)],
        out_specs=pl.BlockSpec((tm, d), row),
        out_shape=jax.ShapeDtypeStruct((m, d), F32),
        compiler_params=_cparams("parallel"),
        name="hyb_out",
    )(a_p, a_s, o_p, o_s, w_out, x, g, b)


def _rec_out_kernel(op_ref, os_ref, gate_ref, ng_ref, w_ref, x_ref, g_ref, b_ref, y_ref, *, alpha, nt_p):
    gate = gate_ref[...]
    o = _pick_part(nt_p, op_ref, os_ref)
    on = _rms_norm_rows(o, ng_ref[...]) * (gate * jax.nn.sigmoid(gate))
    mix = _dot(on.astype(BF16), w_ref[...])
    y_ref[...] = _layer_norm_rows(alpha * x_ref[...] + mix, g_ref[...], b_ref[...])


def _rec_out(o_p, o_s, h, gate_col_block, norm_g, w_out, x, g, b, alpha):
    m, d = x.shape
    vw = o_p.shape[1]
    n_p, n_s = o_p.shape[0], o_s.shape[0]
    tm = _tile(math.gcd(n_p, n_s), 256)
    nt_p = n_p // tm
    row = lambda i: (i, 0)
    fixed = lambda i: (0, 0)
    return pl.pallas_call(
        functools.partial(_rec_out_kernel, alpha=alpha, nt_p=nt_p),
        grid=(m // tm,),
        in_specs=[*_two_part_specs(tm, vw, nt_p),
                  pl.BlockSpec((tm, vw), lambda i: (i, gate_col_block)),
                  pl.BlockSpec((1, vw), fixed), pl.BlockSpec(w_out.shape, fixed),
                  pl.BlockSpec((tm, d), row), pl.BlockSpec((1, d), fixed),
                  pl.BlockSpec((1, d), fixed)],
        out_specs=pl.BlockSpec((tm, d), row),
        out_shape=jax.ShapeDtypeStruct((m, d), F32),
        compiler_params=_cparams("parallel"),
        name="rec_out",
    )(o_p, o_s, h, norm_g, w_out, x, g, b)


def _gla_levels(chunk):
    m = chunk // 2
    out = []
    while m >= 1:
        out.append(m)
        m //= 2
    return tuple(out)


def _gla_prefix_matrix(chunk):
    t = np.arange(chunk)
    j = np.arange(chunk)
    blocks = [(j[None, :] <= t[:, None])]
    for m in _gla_levels(chunk):
        ref = (t // (2 * m)) * (2 * m) + m
        blocks.append(j[None, :] <= ref[:, None])
    blocks.append(np.ones((chunk, chunk), bool))
    return np.concatenate(blocks, 0).astype(np.float32)


def _gla_gates(q_raw, f_raw, lb):
    q = q_raw * jax.nn.sigmoid(q_raw) * (REC_DK ** -0.5)
    forget = lb + (1.0 - lb) * jax.nn.sigmoid(f_raw)
    return q, 1.0 - forget, jnp.log(forget)


def _split3(x):
    hi = x.astype(BF16)
    r1 = x - hi.astype(F32)
    mid = r1.astype(BF16)
    lo = (r1 - mid.astype(F32)).astype(BF16)
    return hi, mid, lo


def _gla_prompt_kernel(q_ref, f_ref, v_ref, lb_ref, pm_ref, o_ref, st_ref, s_ref, *, tt, gh):
    c = GLA_CHUNK
    dk = REC_DK
    i = pl.program_id(2)

    @pl.when(i == 0)
    def _():
        s_ref[...] = jnp.zeros(s_ref.shape, F32)

    lb = lb_ref[...]
    pm = pm_ref[...]
    levels = _gla_levels(c)
    row = lax.broadcasted_iota(jnp.int32, (c, dk), 0)
    ti = lax.broadcasted_iota(jnp.int32, (c, c), 0)
    si = lax.broadcasted_iota(jnp.int32, (c, c), 1)

    def body(ci, carry):
        r0 = pl.multiple_of(ci * c, c)
        q_all, k_all, g_all = _gla_gates(q_ref[pl.ds(r0, c), :], f_ref[pl.ds(r0, c), :], lb)
        v_all = v_ref[pl.ds(r0, c), :].astype(BF16)
        hi, mid, lo = _split3(g_all)
        pre = _dot(pm, hi) + _dot(pm, mid) + _dot(pm, lo)
        outs, states = [], []
        for hd in range(gh):
            sl = slice(hd * dk, (hd + 1) * dk)
            q, k, v = q_all[:, sl], k_all[:, sl], v_all[:, sl]
            cum = pre[0:c, sl]
            last = pre[(len(levels) + 1) * c:(len(levels) + 2) * c, sl]
            att = jnp.where(ti == si, _dot_nt(q.astype(BF16), k.astype(BF16)), 0.0)
            for li, m in enumerate(levels):
                ref = pre[(li + 1) * c:(li + 2) * c, sl]
                e = jnp.exp(-jnp.abs(cum - ref))
                later = (row // m) % 2 == 1
                ql = jnp.where(later, q * e, 0.0).astype(BF16)
                kl = jnp.where(later, 0.0, k * e).astype(BF16)
                att = att + jnp.where(ti // (2 * m) == si // (2 * m), _dot_nt(ql, kl), 0.0)
            s_t = s_ref[hd * dk:(hd + 1) * dk, :]
            inter = _dot_nt((q * jnp.exp(cum)).astype(BF16), s_t.astype(BF16))
            outs.append(inter + _dot(att.astype(BF16), v))
            kd = (k * jnp.exp(last - cum)).astype(BF16)
            states.append(s_t * jnp.exp(last[0:1, :]) + _dot_tn(v, kd))
        o_ref[pl.ds(r0, c), :] = jnp.concatenate(outs, axis=1)
        s_ref[...] = jnp.concatenate(states, axis=0)
        return carry

    lax.fori_loop(0, tt // c, body, 0)

    @pl.when(i == pl.num_programs(2) - 1)
    def _():
        for hd in range(gh):
            st_ref[hd] = s_ref[hd * dk:(hd + 1) * dk, :]


def _gla_prompt(h, lb, bsz, seq, heads):
    tt = _tile(seq, 512)
    assert tt % GLA_CHUNK == 0
    nt = seq // tt
    gh = math.gcd(GLA_HEADS_PER_STEP, heads)
    ng = heads // gh
    wide = gh * REC_DK
    pm = jnp.asarray(_gla_prefix_matrix(GLA_CHUNK), BF16)

    def col(base):
        return pl.BlockSpec((tt, wide), lambda b, hg, i: (b * nt + i, base + hg))

    return pl.pallas_call(
        functools.partial(_gla_prompt_kernel, tt=tt, gh=gh),
        grid=(bsz, ng, nt),
        in_specs=[col(0), col(ng), col(2 * ng),
                  pl.BlockSpec((1, wide), lambda b, hg, i: (0, hg)),
                  pl.BlockSpec(pm.shape, lambda b, hg, i: (0, 0))],
        out_specs=[pl.BlockSpec((tt, wide), lambda b, hg, i: (b * nt + i, hg)),
                   pl.BlockSpec((None, gh, REC_DK, REC_DK), lambda b, hg, i: (b, hg, 0, 0))],
        out_shape=[jax.ShapeDtypeStruct((bsz * seq, heads * REC_DK), F32),
                   jax.ShapeDtypeStruct((bsz, heads, REC_DK, REC_DK), F32)],
        scratch_shapes=[pltpu.VMEM((wide, REC_DK), F32)],
        compiler_params=_cparams("parallel", "parallel", "arbitrary"),
        name="gla_prompt",
    )(h, h, h, lb, pm)


def _gla_sample_kernel(q_ref, f_ref, v_ref, lb_ref, hs_ref, he_ref, st_ref, o_ref, sto_ref, *, ts, heads):
    dk = REC_DK
    q, k, g = _gla_gates(q_ref[...], f_ref[...], lb_ref[...])
    v = v_ref[...]
    t = lax.broadcasted_iota(jnp.int32, q.shape, 0)
    cum = g
    sh = 1
    while sh < ts:
        cum = cum + jnp.where(t >= sh, pltpu.roll(cum, sh, 0), 0.0)
        sh *= 2
    last = cum[ts - 1:ts, :]
    xs = []
    for s in range(ts):
        dec = jnp.exp(jnp.where(t >= s, cum - cum[s:s + 1, :], -jnp.inf))
        xs.append(q * k[s:s + 1, :] * dec)
    x = jnp.concatenate(xs, axis=0)
    x_hi = x.astype(BF16)
    x_lo = (x - x_hi.astype(F32)).astype(BF16)
    att = _dot(x_hi, hs_ref[...]) + _dot(x_lo, hs_ref[...])
    att_e = _dot(att.astype(BF16), he_ref[...])
    intra = jnp.zeros(q.shape, F32)
    for s in range(ts):
        intra = intra + att_e[s * ts:(s + 1) * ts, :] * v[s:s + 1, :]
    qe = (q * jnp.exp(cum)).astype(BF16)
    kd = (k * jnp.exp(last - cum)).astype(BF16)
    dec_last = jnp.exp(last)
    eye = (lax.broadcasted_iota(jnp.int32, (dk, dk), 0)
           == lax.broadcasted_iota(jnp.int32, (dk, dk), 1))
    vb = v.astype(BF16)
    for hd in range(heads):
        sl = slice(hd * dk, (hd + 1) * dk)
        s_h = st_ref[hd]
        o_ref[:, sl] = _dot(qe[:, sl], s_h.astype(BF16)) + intra[:, sl]
        d_col = jnp.sum(jnp.where(eye, dec_last[:, sl], 0.0), axis=1, keepdims=True)
        sto_ref[hd] = s_h * d_col + _dot_tn(kd[:, sl], vb[:, sl])


def _gla_sample(h, lb, state, j, row0, bsz, ts, heads):
    width = heads * REC_DK
    off = row0 // ts
    head_of = np.arange(width) // REC_DK
    hsum = (head_of[:, None] == np.arange(LANES)[None, :]).astype(np.float32)
    hs = jnp.asarray(hsum, BF16)
    he = jnp.asarray(hsum.T, BF16)

    def col(cb):
        return pl.BlockSpec((ts, width), lambda b: (off + b, cb))

    fixed = lambda b: (0, 0)
    st_in = pl.BlockSpec((None, None, heads, REC_DK, REC_DK), lambda b: (j, b, 0, 0, 0))
    st_out = pl.BlockSpec((None, heads, REC_DK, REC_DK), lambda b: (b, 0, 0, 0))
    return pl.pallas_call(
        functools.partial(_gla_sample_kernel, ts=ts, heads=heads),
        grid=(bsz,),
        in_specs=[col(0), col(1), col(2), pl.BlockSpec((1, width), fixed),
                  pl.BlockSpec(hs.shape, fixed), pl.BlockSpec(he.shape, fixed), st_in],
        out_specs=[pl.BlockSpec((ts, width), lambda b: (b, 0)), st_out],
        out_shape=[jax.ShapeDtypeStruct((bsz * ts, width), F32),
                   jax.ShapeDtypeStruct(state.shape[1:], F32)],
        compiler_params=_cparams("parallel"),
        name="gla_sample",
    )(h, h, h, lb, hs, he, state)


def _router_kernel(x_ref, wh_ref, wl_ref, b_ref, o_ref):
    x = x_ref[...]
    x_hi = x.astype(BF16)
    x_lo = (x - x_hi.astype(F32)).astype(BF16)
    wh = wh_ref[...]
    logits = _dot(x_hi, wh) + _dot(x_lo, wh) + _dot(x_hi, wl_ref[...]) + b_ref[...]
    lane = lax.broadcasted_iota(jnp.int32, logits.shape, 1).astype(F32)
    big = float(LANES)
    neg = -jnp.inf
    gl = jnp.where(lane < N_GROUPS, logits, neg)
    g_max = jnp.max(gl, axis=-1, keepdims=True)
    g_sel = jnp.min(jnp.where(gl == g_max, lane, big), axis=-1, keepdims=True)
    g_gate = 1.0 / jnp.sum(jnp.exp(gl - g_max), axis=-1, keepdims=True)
    lo = N_GROUPS + g_sel * EXPERTS_PER_GROUP
    el = jnp.where((lane >= lo) & (lane < lo + EXPERTS_PER_GROUP), logits, neg)
    e_max = jnp.max(el, axis=-1, keepdims=True)
    i1 = jnp.min(jnp.where(el == e_max, lane, big), axis=-1, keepdims=True)
    denom = jnp.sum(jnp.exp(el - e_max), axis=-1, keepdims=True)
    el2 = jnp.where(lane == i1, neg, el)
    m2 = jnp.max(el2, axis=-1, keepdims=True)
    i2 = jnp.min(jnp.where(el2 == m2, lane, big), axis=-1, keepdims=True)
    p1 = 1.0 / denom
    p2 = jnp.exp(m2 - e_max) / denom
    w1 = g_gate * p1 / (p1 + p2)
    w2 = g_gate * p2 / (p1 + p2)
    e1 = i1 - N_GROUPS
    e2 = i2 - N_GROUPS
    o_ref[...] = jnp.where(lane == 0, e1, jnp.where(lane == 1, e2,
                           jnp.where(lane == 2, w1, jnp.where(lane == 3, w2, 0.0))))


def _router(x, w_hi, w_lo, bias):
    m, d = x.shape
    tm = _tile(m, 512)
    fixed = lambda i: (0, 0)
    return pl.pallas_call(
        _router_kernel,
        grid=(m // tm,),
        in_specs=[pl.BlockSpec((tm, d), lambda i: (i, 0)), pl.BlockSpec((d, LANES), fixed),
                  pl.BlockSpec((d, LANES), fixed), pl.BlockSpec((1, LANES), fixed)],
        out_specs=pl.BlockSpec((tm, LANES), lambda i: (i, 0)),
        out_shape=jax.ShapeDtypeStruct((m, LANES), F32),
        compiler_params=_cparams("parallel"),
        name="router",
    )(x, w_hi, w_lo, bias)


def _expert_weights(t, te_ref, first_ref, slot_ref, nxt_ref, layer, hbm_refs, buf_refs, sem_refs, bf_refs):
    def copies(expert, slot):
        return [pltpu.make_async_copy(hbm.at[layer, expert], buf.at[slot], sem.at[slot])
                for hbm, buf, sem in zip(hbm_refs, buf_refs, sem_refs)]

    @pl.when(t == 0)
    def _():
        for cp in copies(te_ref[0], slot_ref[0]):
            cp.start()

    @pl.when(first_ref[t] == 1)
    def _():
        slot = slot_ref[t]

        @pl.when(nxt_ref[t] >= 0)
        def _():
            for cp in copies(nxt_ref[t], 1 - slot):
                cp.start()

        for cp in copies(te_ref[t], slot):
            cp.wait()
        for buf, bf in zip(buf_refs, bf_refs):
            bf[...] = buf[slot].astype(BF16)


def _moe_up_kernel(te_ref, first_ref, slot_ref, nxt_ref, nu_ref, x_ref, wg_hbm, wu_hbm, h_ref,
                   wg_buf, wu_buf, wg_sem, wu_sem, wgb_ref, wub_ref, *, layer):
    t = pl.program_id(0)

    @pl.when(t < nu_ref[0])
    def _():
        _expert_weights(t, te_ref, first_ref, slot_ref, nxt_ref, layer, (wg_hbm, wu_hbm),
                        (wg_buf, wu_buf), (wg_sem, wu_sem), (wgb_ref, wub_ref))
        x = x_ref[...].astype(BF16)
        gate = _dot(x, wgb_ref[...])
        up = _dot(x, wub_ref[...])
        h_ref[...] = (gate * jax.nn.sigmoid(gate) * up).astype(h_ref.dtype)

    @pl.when(t >= nu_ref[0])
    def _():
        h_ref[...] = jnp.zeros(h_ref.shape, h_ref.dtype)


def _moe_down_kernel(te_ref, first_ref, slot_ref, nxt_ref, nu_ref, h_ref, wd_hbm, y_ref,
                     wd_buf, wd_sem, wdb_ref, *, layer):
    t = pl.program_id(0)

    @pl.when(t < nu_ref[0])
    def _():
        _expert_weights(t, te_ref, first_ref, slot_ref, nxt_ref, layer, (wd_hbm,), (wd_buf,),
                        (wd_sem,), (wdb_ref,))
        y_ref[...] = _dot(h_ref[...], wdb_ref[...])

    @pl.when(t >= nu_ref[0])
    def _():
        y_ref[...] = jnp.zeros(y_ref.shape, y_ref.dtype)


def _moe_experts(xs, plan, w_gate, w_up, w_down, layer):
    cap, d = xs.shape
    de = w_gate.shape[-1]
    n_tiles = cap // MOE_TILE
    n_plan = len(plan)

    def live(t, *refs):
        return (jnp.minimum(t, refs[n_plan - 1][0] - 1), 0)

    own = lambda t, *refs: (t, 0)
    hbm = pl.BlockSpec(memory_space=pl.ANY)
    two = lambda *shape: pltpu.VMEM((2,) + shape, F32)
    sem = pltpu.SemaphoreType.DMA((2,))

    hid = pl.pallas_call(
        functools.partial(_moe_up_kernel, layer=layer),
        grid_spec=pltpu.PrefetchScalarGridSpec(
            num_scalar_prefetch=n_plan,
            grid=(n_tiles,),
            in_specs=[pl.BlockSpec((MOE_TILE, d), live), hbm, hbm],
            out_specs=pl.BlockSpec((MOE_TILE, de), own),
            scratch_shapes=[two(d, de), two(d, de), sem, sem,
                            pltpu.VMEM((d, de), BF16), pltpu.VMEM((d, de), BF16)]),
        out_shape=jax.ShapeDtypeStruct((cap, de), BF16),
        compiler_params=_cparams("arbitrary"),
        name="moe_up",
    )(*plan, xs, w_gate, w_up)

    return pl.pallas_call(
        functools.partial(_moe_down_kernel, layer=layer),
        grid_spec=pltpu.PrefetchScalarGridSpec(
            num_scalar_prefetch=n_plan,
            grid=(n_tiles,),
            in_specs=[pl.BlockSpec((MOE_TILE, de), live), hbm],
            out_specs=pl.BlockSpec((MOE_TILE, d), own),
            scratch_shapes=[two(de, d), sem, pltpu.VMEM((de, d), BF16)]),
        out_shape=jax.ShapeDtypeStruct((cap, d), F32),
        compiler_params=_cparams("arbitrary"),
        name="moe_down",
    )(*plan, hid, w_down)


def _moe_ln_kernel(x_ref, y0_ref, y1_ref, r_ref, g_ref, b_ref, o_ref, *, alpha):
    r = r_ref[...]
    moe = r[:, TOP_K:TOP_K + 1] * y0_ref[...] + r[:, TOP_K + 1:TOP_K + 2] * y1_ref[...]
    o_ref[...] = _layer_norm_rows(alpha * x_ref[...] + moe, g_ref[...], b_ref[...])


def _moe_ln(x, y2, routed, g, b, alpha):
    m, d = x.shape
    tm = _tile(m, 512)
    nt = m // tm
    row = lambda i: (i, 0)
    fixed = lambda i: (0, 0)
    return pl.pallas_call(
        functools.partial(_moe_ln_kernel, alpha=alpha),
        grid=(nt,),
        in_specs=[pl.BlockSpec((tm, d), row), pl.BlockSpec((tm, d), row),
                  pl.BlockSpec((tm, d), lambda i: (nt + i, 0)), pl.BlockSpec((tm, LANES), row),
                  pl.BlockSpec((1, d), fixed), pl.BlockSpec((1, d), fixed)],
        out_specs=pl.BlockSpec((tm, d), row),
        out_shape=jax.ShapeDtypeStruct((m, d), F32),
        compiler_params=_cparams("parallel"),
        name="moe_ln",
    )(x, y2, y2, routed, g, b)


def _moe_layer(x, layer, w_rg, b_rg, w_re, b_re, w_gate, w_up, w_down, ln_g, ln_b, alpha):
    m, d = x.shape
    wr = jnp.concatenate([w_rg, w_re], axis=1)
    wr = jnp.pad(wr, ((0, 0), (0, LANES - wr.shape[1])))
    wr_hi = wr.astype(BF16)
    wr_lo = (wr - wr_hi.astype(F32)).astype(BF16)
    br = jnp.pad(jnp.concatenate([b_rg, b_re]), (0, LANES - N_GROUPS - N_EXPERTS))[None, :]
    routed = _router(x, wr_hi, wr_lo, br)
    ids = routed[:, 0:TOP_K].astype(jnp.int32)

    n_assign = m * TOP_K
    n_tiles = -(-n_assign // MOE_TILE) + N_EXPERTS
    cap = n_tiles * MOE_TILE
    flat_e = ids.reshape(-1)
    onehot = (flat_e[:, None] == jnp.arange(N_EXPERTS, dtype=jnp.int32)[None, :]).astype(jnp.int32)
    csum = jnp.cumsum(onehot, axis=0)
    counts = csum[-1]
    rank = jnp.take_along_axis(csum, flat_e[:, None], axis=1)[:, 0] - 1
    tiles_per = (counts + MOE_TILE - 1) // MOE_TILE
    tile_ends = jnp.cumsum(tiles_per)
    pad_starts = (tile_ends - tiles_per) * MOE_TILE
    dest = pad_starts[flat_e] + rank
    n_used = tile_ends[-1:].astype(jnp.int32)
    tile_idx = jnp.arange(n_tiles, dtype=jnp.int32)
    tile_e = jnp.sum((tile_ends[None, :] <= tile_idx[:, None]).astype(jnp.int32), axis=1)
    last_e = jnp.max(jnp.where(counts > 0, jnp.arange(N_EXPERTS), 0))
    tile_e = jnp.minimum(tile_e, last_e).astype(jnp.int32)
    prev_e = jnp.concatenate([jnp.full((1,), -1, jnp.int32), tile_e[:-1]])
    first = ((tile_e != prev_e) & (tile_idx < n_used[0])).astype(jnp.int32)
    slot = (jnp.cumsum(first) - 1) % 2
    next_tile = tile_ends[tile_e]
    nxt = jnp.where(next_tile < n_used[0], tile_e[jnp.minimum(next_tile, n_tiles - 1)], -1)
    plan = (tile_e, first, slot.astype(jnp.int32), nxt.astype(jnp.int32), n_used)
    tok = jnp.arange(n_assign, dtype=jnp.int32) // TOP_K
    slot_tok = (jnp.arange(cap, dtype=jnp.int32) % m).at[dest].set(tok)

    xs = x[slot_tok]
    y = _moe_experts(xs, plan, w_gate, w_up, w_down, layer)
    y2 = y[dest.reshape(m, TOP_K).T.reshape(-1)]
    return _moe_ln(x, y2, routed, ln_g[None, :], ln_b[None, :], alpha)


def _rope_tables(pos):
    half = ROPE_DIM // 2
    inv_freq = ROPE_THETA ** (-jnp.arange(half, dtype=F32) / half)
    ang = pos.astype(F32)[:, None] * inv_freq
    cos, sin = jnp.cos(ang), jnp.sin(ang)
    z = jnp.zeros_like(cos)
    pad = jnp.zeros((pos.shape[0], LANES - ROPE_DIM), F32)
    c = jnp.concatenate([cos, cos, pad], axis=1)
    sa = jnp.concatenate([-sin, z, pad], axis=1)
    sb = jnp.concatenate([z, sin, pad], axis=1)
    return c, sa, sb


def _hybrid_layer(x, dims, rope, conv_state, cache_lat, cache_pe, page_table, j,
                  w_in, conv_w, q_g, kv_g, w_q_up, w_uk, w_uv, w_out, ln_g, ln_b, alpha):
    bp, sp, bs, ts = dims
    n_p = bp * sp
    cw = conv_w.shape[1]
    heads = w_q_up.shape[1]
    n_main = 3 * cw + Q_LORA + KV_LORA

    w_pe = jnp.pad(w_in[j, :, n_main:], ((0, 0), (0, LANES - ROPE_DIM))).astype(BF16)
    h = _mm(x, w_in, layer=j, n=n_main, name="hyb_in")
    h_pe = _mm(x, w_pe, name="hyb_in_pe")

    a_p, conv_p = _conv_prompt(h, conv_w, bp, sp, cw)
    a_s, conv_s = _conv_sample(h, conv_state, conv_w, n_p, bs, ts, cw)

    w_q_cat = jnp.pad(w_q_up, ((0, 0), (0, 0), (0, HEAD_PAD - NOPE_DIM - ROPE_DIM)))
    w_q_cat = w_q_cat.reshape(Q_LORA, heads * HEAD_PAD).astype(BF16)
    q_cat = _q_proj(h, 3 * cw // Q_LORA, q_g[None, :], w_q_cat, *rope, heads)
    lat, kpe = _kv_norm(h, (3 * cw + Q_LORA) // KV_LORA, h_pe, kv_g[None, :], *rope)
    w_uk2 = w_uk.reshape(KV_LORA, heads * NOPE_DIM).astype(BF16)
    w_uv2 = w_uv.reshape(KV_LORA, heads * V_DIM).astype(BF16)
    w_uv_t = w_uv2.T

    k_cat, v_t = _kv_expand(lat, kpe, w_uk2, w_uv_t, n_p, heads)
    o_p = _attn_prompt(q_cat, k_cat, v_t, bp, sp, heads)

    n_s = bs * ts
    q_lat = _q_latent(q_cat, w_uk2, n_p, n_s, heads)
    q_lat = q_lat.reshape(heads, bs, ts, KV_LORA).transpose(1, 0, 2, 3).reshape(bs, heads * ts, KV_LORA)
    q_pe = q_cat[n_p:].reshape(bs, ts, heads, HEAD_PAD)[..., NOPE_DIM:NOPE_DIM + ROPE_DIM]
    q_pe = q_pe.transpose(0, 2, 1, 3).reshape(bs, heads * ts, ROPE_DIM)
    lat_s = lat[n_p:].reshape(bs, ts, KV_LORA)
    kpe_s = kpe[n_p:, :ROPE_DIM].reshape(bs, ts, ROPE_DIM)
    new_lat = jnp.pad(lat_s, ((0, 0), (0, ts), (0, 0)))
    new_pe = jnp.pad(kpe_s, ((0, 0), (0, ts), (0, 0)))
    o_lat = _attn_sample(page_table, q_lat, q_pe, cache_lat, cache_pe, j, new_lat, new_pe, ts)
    o_lat = o_lat.reshape(bs, heads, ts, KV_LORA).transpose(1, 0, 2, 3).reshape(heads, n_s, KV_LORA)
    o_s = _o_from_latent(o_lat, w_uv2, heads)

    y = _hyb_out(a_p, a_s, o_p, o_s, w_out.astype(BF16), x, ln_g[None, :], ln_b[None, :], alpha)
    outs = (lat[:n_p].reshape(bp, sp, KV_LORA), kpe[:n_p, :ROPE_DIM].reshape(bp, sp, ROPE_DIM),
            lat_s, kpe_s, conv_p, conv_s)
    return y, outs


def _rec_layer(x, dims, state, j, lb, w_in, norm_g, w_out, ln_g, ln_b, alpha):
    bp, sp, bs, ts = dims
    n_p = bp * sp
    width = w_in.shape[-1] // 4
    heads = width // REC_DK
    h = _mm(x, w_in, layer=j, name="rec_in")
    lb2 = lb[None, :]
    o_p, st_p = _gla_prompt(h, lb2, bp, sp, heads)
    o_s, st_s = _gla_sample(h, lb2, state, j, n_p, bs, ts, heads)
    y = _rec_out(o_p, o_s, h, 3, norm_g[None, :], w_out.astype(BF16), x, ln_g[None, :], ln_b[None, :], alpha)
    return y, (jnp.swapaxes(st_p, -1, -2), st_s)


def kernel(x_prompt, x_sample, cache_kv_latent, cache_k_rope, state_conv, state_hgrn, page_table,
           w_in_hyb, conv_w, q_norm_g, kv_norm_g, w_q_up, w_uk, w_uv, w_out_hyb,
           w_in_rec, lb_logits, rec_norm_g, w_out_rec,
           ln1_g, ln1_b, ln2_g, ln2_b,
           w_router_group, b_router_group, w_router_expert, b_router_expert,
           w_gate, w_up, w_down):
    bp, sp, d = x_prompt.shape
    bs, ts, _ = x_sample.shape
    depth = ln1_g.shape[0]
    past_len = page_table.shape[1] * cache_kv_latent.shape[2]
    dims = (bp, sp, bs, ts)
    n_p = bp * sp
    alpha = (2 * depth) ** 0.25

    lower = jnp.cumsum(jax.nn.softmax(lb_logits.astype(F32), axis=0), axis=0)
    lower = lower - lower[0]

    pos = jnp.concatenate([jnp.tile(jnp.arange(sp), bp), jnp.tile(past_len + jnp.arange(ts), bs)])
    rope = _rope_tables(pos)

    x = jnp.concatenate([x_prompt.reshape(n_p, d), x_sample.reshape(bs * ts, d)], axis=0)
    cache_pe_t = jnp.swapaxes(cache_k_rope, 2, 3)
    hyb_outs, rec_outs = [], []
    for layer in range(depth):
        j = layer // 2
        if layer % 2 == 0:
            x, outs = _hybrid_layer(x, dims, rope, state_conv[j], cache_kv_latent, cache_pe_t, page_table, j,
                                    w_in_hyb, conv_w[j], q_norm_g[j], kv_norm_g[j], w_q_up[j], w_uk[j],
                                    w_uv[j], w_out_hyb[j], ln1_g[layer], ln1_b[layer], alpha)
            hyb_outs.append(outs)
        else:
            x, outs = _rec_layer(x, dims, state_hgrn, j, lower[layer], w_in_rec, rec_norm_g[j],
                                 w_out_rec[j], ln1_g[layer], ln1_b[layer], alpha)
            rec_outs.append(outs)
        x = _moe_layer(x, layer, w_router_group[layer], b_router_group[layer], w_router_expert[layer],
                       b_router_expert[layer], w_gate, w_up, w_down, ln2_g[layer], ln2_b[layer], alpha)

    stack = lambda parts, i: jnp.stack([p[i] for p in parts])
    return (x[:n_p].reshape(bp, sp, d), x[n_p:].reshape(bs, ts, d),
            stack(hyb_outs, 0), stack(hyb_outs, 1), stack(hyb_outs, 2), stack(hyb_outs, 3),
            stack(hyb_outs, 4), stack(hyb_outs, 5), stack(rec_outs, 0), stack(rec_outs, 1))
```

```python
import functools
import math

import numpy as np
import jax
import jax.numpy as jnp
from jax import lax
from jax.experimental import pallas as pl
from jax.experimental.pallas import tpu as pltpu

F32 = jnp.float32
BF16 = jnp.bfloat16

CONV_K = 3
Q_LORA = 512
KV_LORA = 512
NOPE_DIM = 128
ROPE_DIM = 64
V_DIM = 128
REC_DK = 128
N_GROUPS = 4
EXPERTS_PER_GROUP = 8
N_EXPERTS = N_GROUPS * EXPERTS_PER_GROUP
TOP_K = 2
ROPE_THETA = 10000.0
LN_EPS = 1e-5
RMS_EPS = 1e-6
MLA_SCALE = (NOPE_DIM + ROPE_DIM) ** -0.5

LANES = 128
SUBLANES = 8
VMEM_LIMIT_BYTES = 52 * 1024 * 1024

HEAD_PAD = 2 * LANES
GLA_CHUNK = 64
GLA_HEADS_PER_STEP = 8
MOE_TILE = 256
ATTN_TQ = 512
ATTN_HEADS_PER_STEP = 2
PAGES_PER_STEP = 16


def _cparams(*sem):
    return pltpu.CompilerParams(dimension_semantics=sem, vmem_limit_bytes=VMEM_LIMIT_BYTES)


def _tile(n, pref):
    if n <= pref:
        return n
    for t in range(pref, 7, -1):
        if n % t == 0 and t % 8 == 0:
            return t
    return n


def _dot(a, b):
    return jnp.dot(a, b, preferred_element_type=F32)


def _dot_nt(a, b):
    return lax.dot_general(a, b, (((1,), (1,)), ((), ())), preferred_element_type=F32)


def _dot_tn(a, b):
    return lax.dot_general(a, b, (((0,), (0,)), ((), ())), preferred_element_type=F32)


def _mm_kernel(x_ref, w_ref, o_ref, xb_ref):
    @pl.when(pl.program_id(1) == 0)
    def _():
        xb_ref[...] = x_ref[...].astype(BF16)

    o_ref[...] = _dot(xb_ref[...], w_ref[...].astype(BF16)).astype(o_ref.dtype)


def _mm(x, w, *, layer=None, n=None, tm=1536, tn=512, out_dtype=F32, name="mm"):
    m, k = x.shape
    n = w.shape[-1] if n is None else n
    tm = _tile(m, tm)
    tn = _tile(n, tn)
    if layer is None:
        w_spec = pl.BlockSpec((k, tn), lambda i, j: (0, j))
    else:
        w_spec = pl.BlockSpec((None, k, tn), lambda i, j: (layer, 0, j))
    return pl.pallas_call(
        _mm_kernel,
        grid=(m // tm, n // tn),
        in_specs=[pl.BlockSpec((tm, k), lambda i, j: (i, 0)), w_spec],
        out_specs=pl.BlockSpec((tm, tn), lambda i, j: (i, j)),
        out_shape=jax.ShapeDtypeStruct((m, n), out_dtype),
        scratch_shapes=[pltpu.VMEM((tm, k), BF16)],
        compiler_params=_cparams("parallel", "arbitrary"),
        name=name,
    )(x, w)


def _layer_norm_rows(z, g, b):
    mu = jnp.mean(z, axis=-1, keepdims=True)
    zc = z - mu
    var = jnp.mean(zc * zc, axis=-1, keepdims=True)
    return zc * lax.rsqrt(var + LN_EPS) * g + b


def _rms_norm_rows(z, g):
    return z * lax.rsqrt(jnp.mean(z * z, axis=-1, keepdims=True) + RMS_EPS) * g


def _conv_prompt_kernel(cx_ref, gb_ref, gc_ref, cxp_ref, gcp_ref, w_ref, a_ref, st_ref, *, tt):
    i = pl.program_id(1)
    u = gc_ref[...] * cx_ref[...]
    prev = jnp.where(i > 0, gcp_ref[...] * cxp_ref[...], 0.0)
    row = lax.broadcasted_iota(jnp.int32, u.shape, 0)
    u1 = jnp.where(row == 0, prev[7:8], pltpu.roll(u, 1, 0))
    u2 = jnp.where(row == 0, prev[6:7], jnp.where(row == 1, prev[7:8], pltpu.roll(u, 2, 0)))
    v = w_ref[0:1, :] * u2 + w_ref[1:2, :] * u1 + w_ref[2:3, :] * u
    a_ref[...] = gb_ref[...] * v

    @pl.when(i == pl.num_programs(1) - 1)
    def _():
        st_ref[...] = u[tt - (CONV_K - 1):tt]


def _conv_prompt(h, conv_w, bsz, seq, cw):
    tt = _tile(seq, 512)
    nt = seq // tt
    sub = tt // SUBLANES

    def main(c):
        return pl.BlockSpec((tt, cw), lambda b, i: (b * nt + i, c))

    def halo(c):
        return pl.BlockSpec((SUBLANES, cw), lambda b, i: (jnp.maximum((b * nt + i) * sub - 1, 0), c))

    return pl.pallas_call(
        functools.partial(_conv_prompt_kernel, tt=tt),
        grid=(bsz, nt),
        in_specs=[main(0), main(1), main(2), halo(0), halo(2),
                  pl.BlockSpec((CONV_K, cw), lambda b, i: (0, 0))],
        out_specs=[pl.BlockSpec((tt, cw), lambda b, i: (b * nt + i, 0)),
                   pl.BlockSpec((None, CONV_K - 1, cw), lambda b, i: (b, 0, 0))],
        out_shape=[jax.ShapeDtypeStruct((bsz * seq, cw), F32),
                   jax.ShapeDtypeStruct((bsz, CONV_K - 1, cw), F32)],
        compiler_params=_cparams("parallel", "arbitrary"),
        name="conv_prompt",
    )(h, h, h, h, h, conv_w)


def _conv_sample_kernel(cx_ref, gb_ref, gc_ref, st_ref, w_ref, a_ref, sto_ref, *, nb, ts):
    cw = cx_ref.shape[-1]
    u = (gc_ref[...] * cx_ref[...]).reshape(nb, ts, cw)
    st = st_ref[...]
    t = lax.broadcasted_iota(jnp.int32, u.shape, 1)
    s0 = st[:, 0:1, :]
    s1 = st[:, 1:2, :]
    u1 = jnp.where(t == 0, s1, pltpu.roll(u, 1, 1))
    u2 = jnp.where(t == 0, s0, jnp.where(t == 1, s1, pltpu.roll(u, 2, 1)))
    w = w_ref[...]
    v = w[0:1, :][None] * u2 + w[1:2, :][None] * u1 + w[2:3, :][None] * u
    a_ref[...] = gb_ref[...] * v.reshape(nb * ts, cw)
    sto_ref[...] = u[:, ts - (CONV_K - 1):ts, :]


def _conv_sample(h, state, conv_w, row0, bsz, ts, cw):
    nb = _tile(bsz, 64)
    rows = nb * ts
    off = row0 // rows

    def main(c):
        return pl.BlockSpec((rows, cw), lambda i: (off + i, c))

    return pl.pallas_call(
        functools.partial(_conv_sample_kernel, nb=nb, ts=ts),
        grid=(bsz // nb,),
        in_specs=[main(0), main(1), main(2),
                  pl.BlockSpec((nb, CONV_K - 1, cw), lambda i: (i, 0, 0)),
                  pl.BlockSpec((CONV_K, cw), lambda i: (0, 0))],
        out_specs=[pl.BlockSpec((rows, cw), lambda i: (i, 0)),
                   pl.BlockSpec((nb, CONV_K - 1, cw), lambda i: (i, 0, 0))],
        out_shape=[jax.ShapeDtypeStruct((bsz * ts, cw), F32),
                   jax.ShapeDtypeStruct((bsz, CONV_K - 1, cw), F32)],
        compiler_params=_cparams("parallel"),
        name="conv_sample",
    )(h, h, h, state, conv_w)


def _rope_lanes(x, c, sa, sb):
    half = ROPE_DIM // 2
    return x * c + pltpu.roll(x, LANES - half, 1) * sa + pltpu.roll(x, half, 1) * sb


def _q_proj_kernel(qc_ref, g_ref, w_ref, c_ref, sa_ref, sb_ref, o_ref, *, heads):
    xn = _rms_norm_rows(qc_ref[...], g_ref[...]).astype(BF16)
    q = _dot(xn, w_ref[...]) * MLA_SCALE
    c, sa, sb = c_ref[...], sa_ref[...], sb_ref[...]
    for hd in range(heads):
        lo = hd * HEAD_PAD
        o_ref[:, lo:lo + LANES] = q[:, lo:lo + LANES].astype(o_ref.dtype)
        pe = _rope_lanes(q[:, lo + LANES:lo + HEAD_PAD], c, sa, sb)
        o_ref[:, lo + LANES:lo + HEAD_PAD] = pe.astype(o_ref.dtype)


def _q_proj(h, col_block, g, w_cat, rope_c, rope_sa, rope_sb, heads):
    m = h.shape[0]
    tm = _tile(m, 512)
    n = heads * HEAD_PAD
    row = lambda i: (i, 0)
    return pl.pallas_call(
        functools.partial(_q_proj_kernel, heads=heads),
        grid=(m // tm,),
        in_specs=[pl.BlockSpec((tm, Q_LORA), lambda i: (i, col_block)),
                  pl.BlockSpec((1, Q_LORA), lambda i: (0, 0)),
                  pl.BlockSpec((Q_LORA, n), lambda i: (0, 0)),
                  pl.BlockSpec((tm, LANES), row), pl.BlockSpec((tm, LANES), row),
                  pl.BlockSpec((tm, LANES), row)],
        out_specs=pl.BlockSpec((tm, n), row),
        out_shape=jax.ShapeDtypeStruct((m, n), BF16),
        compiler_params=_cparams("parallel"),
        name="q_proj",
    )(h, g, w_cat, rope_c, rope_sa, rope_sb)


def _kv_norm_kernel(kv_ref, pe_ref, g_ref, c_ref, sa_ref, sb_ref, lat_ref, kpe_ref):
    lat_ref[...] = _rms_norm_rows(kv_ref[...], g_ref[...])
    kpe_ref[...] = _rope_lanes(pe_ref[...], c_ref[...], sa_ref[...], sb_ref[...])


def _kv_norm(h, col_block, h_pe, g, rope_c, rope_sa, rope_sb):
    m = h.shape[0]
    tm = _tile(m, 1024)
    row = lambda i: (i, 0)
    return pl.pallas_call(
        _kv_norm_kernel,
        grid=(m // tm,),
        in_specs=[pl.BlockSpec((tm, KV_LORA), lambda i: (i, col_block)),
                  pl.BlockSpec((tm, LANES), row),
                  pl.BlockSpec((1, KV_LORA), lambda i: (0, 0)),
                  pl.BlockSpec((tm, LANES), row), pl.BlockSpec((tm, LANES), row),
                  pl.BlockSpec((tm, LANES), row)],
        out_specs=[pl.BlockSpec((tm, KV_LORA), row), pl.BlockSpec((tm, LANES), row)],
        out_shape=[jax.ShapeDtypeStruct((m, KV_LORA), F32),
                   jax.ShapeDtypeStruct((m, LANES), F32)],
        compiler_params=_cparams("parallel"),
        name="kv_norm",
    )(h, h_pe, g, rope_c, rope_sa, rope_sb)


def _kv_expand_kernel(lat_ref, kpe_ref, wk_ref, wvt_ref, k_ref, vt_ref, *, heads):
    lat = lat_ref[...].astype(BF16)
    kn = _dot(lat, wk_ref[...])
    kpe = kpe_ref[...].astype(k_ref.dtype)
    for hd in range(heads):
        lo = hd * HEAD_PAD
        k_ref[:, lo:lo + LANES] = kn[:, hd * NOPE_DIM:(hd + 1) * NOPE_DIM].astype(k_ref.dtype)
        k_ref[:, lo + LANES:lo + HEAD_PAD] = kpe
    vt_ref[...] = _dot_nt(wvt_ref[...], lat).astype(vt_ref.dtype)


def _kv_expand(lat, kpe, w_uk, w_uv_t, rows, heads):
    tm = _tile(rows, 512)
    row = lambda i: (i, 0)
    fixed = lambda i: (0, 0)
    return pl.pallas_call(
        functools.partial(_kv_expand_kernel, heads=heads),
        grid=(rows // tm,),
        in_specs=[pl.BlockSpec((tm, KV_LORA), row), pl.BlockSpec((tm, LANES), row),
                  pl.BlockSpec((KV_LORA, heads * NOPE_DIM), fixed),
                  pl.BlockSpec((heads * V_DIM, KV_LORA), fixed)],
        out_specs=[pl.BlockSpec((tm, heads * HEAD_PAD), row),
                   pl.BlockSpec((heads * V_DIM, tm), lambda i: (0, i))],
        out_shape=[jax.ShapeDtypeStruct((rows, heads * HEAD_PAD), BF16),
                   jax.ShapeDtypeStruct((heads * V_DIM, rows), BF16)],
        compiler_params=_cparams("parallel"),
        name="kv_expand",
    )(lat, kpe, w_uk, w_uv_t)


def _attn_prompt_kernel(q_ref, k_ref, vt_ref, o_ref, m_ref, l_ref, acc_ref, *, tq, nh):
    i = pl.program_id(2)
    m_ref[...] = jnp.full(m_ref.shape, -jnp.inf, F32)
    l_ref[...] = jnp.zeros(l_ref.shape, F32)
    acc_ref[...] = jnp.zeros(acc_ref.shape, F32)
    q_all = q_ref[...]

    def chunk(c, masked):
        r0 = pl.multiple_of(c * tq, tq)
        k_all = k_ref[pl.ds(r0, tq), :]
        vt_all = vt_ref[:, pl.ds(r0, tq)]
        m_all, l_all, acc_all = m_ref[...], l_ref[...], acc_ref[...]
        ms, ls, accs = [], [], []
        for hd in range(nh):
            hp = slice(hd * HEAD_PAD, (hd + 1) * HEAD_PAD)
            hv = slice(hd * V_DIM, (hd + 1) * V_DIM)
            s = _dot_nt(k_all[:, hp], q_all[:, hp])
            if masked:
                key = lax.broadcasted_iota(jnp.int32, s.shape, 0)
                qry = lax.broadcasted_iota(jnp.int32, s.shape, 1)
                s = jnp.where(key <= qry, s, -jnp.inf)
            m_old = m_all[hd:hd + 1, :]
            m_new = jnp.maximum(m_old, jnp.max(s, axis=0, keepdims=True))
            alpha = jnp.exp(m_old - m_new)
            p = jnp.exp(s - m_new)
            ms.append(m_new)
            ls.append(alpha * l_all[hd:hd + 1, :] + jnp.sum(p, axis=0, keepdims=True))
            accs.append(alpha * acc_all[hv, :] + _dot(vt_all[hv, :], p.astype(BF16)))
        m_ref[...] = jnp.concatenate(ms, axis=0)
        l_ref[...] = jnp.concatenate(ls, axis=0)
        acc_ref[...] = jnp.concatenate(accs, axis=0)

    def body(c, carry):
        chunk(c, False)
        return carry

    lax.fori_loop(0, i, body, 0)
    chunk(i, True)
    l_all = l_ref[...]
    o_ref[...] = jnp.concatenate(
        [(acc_ref[hd * V_DIM:(hd + 1) * V_DIM, :] / l_all[hd:hd + 1, :]).T for hd in range(nh)],
        axis=1).astype(o_ref.dtype)


def _attn_prompt(q_cat, k_cat, v_t, bsz, seq, heads):
    tq = _tile(seq, ATTN_TQ)
    nq = seq // tq
    nh = math.gcd(ATTN_HEADS_PER_STEP, heads)
    return pl.pallas_call(
        functools.partial(_attn_prompt_kernel, tq=tq, nh=nh),
        grid=(bsz, heads // nh, nq),
        in_specs=[pl.BlockSpec((tq, nh * HEAD_PAD), lambda b, h, i: (b * nq + i, h)),
                  pl.BlockSpec((seq, nh * HEAD_PAD), lambda b, h, i: (b, h)),
                  pl.BlockSpec((nh * V_DIM, seq), lambda b, h, i: (h, b))],
        out_specs=pl.BlockSpec((tq, nh * V_DIM), lambda b, h, i: (b * nq + i, h)),
        out_shape=jax.ShapeDtypeStruct((bsz * seq, heads * V_DIM), F32),
        scratch_shapes=[pltpu.VMEM((nh, tq), F32), pltpu.VMEM((nh, tq), F32),
                        pltpu.VMEM((nh * V_DIM, tq), F32)],
        compiler_params=_cparams("parallel", "parallel", "arbitrary"),
        name="attn_prompt",
    )(q_cat, k_cat, v_t)


def _head_mm_kernel(x_ref, w_ref, o_ref, *, nt):
    x = x_ref[...].astype(BF16)
    o_ref[...] = (_dot_nt(x, w_ref[...]) if nt else _dot(x, w_ref[...])).astype(o_ref.dtype)


def _q_latent(q_cat, w_uk, row0, rows, heads):
    off = row0 // rows
    return pl.pallas_call(
        functools.partial(_head_mm_kernel, nt=True),
        grid=(heads,),
        in_specs=[pl.BlockSpec((rows, NOPE_DIM), lambda h: (off, 2 * h)),
                  pl.BlockSpec((KV_LORA, NOPE_DIM), lambda h: (0, h))],
        out_specs=pl.BlockSpec((None, rows, KV_LORA), lambda h: (h, 0, 0)),
        out_shape=jax.ShapeDtypeStruct((heads, rows, KV_LORA), BF16),
        compiler_params=_cparams("parallel"),
        name="q_latent",
    )(q_cat, w_uk)


def _o_from_latent(o_lat, w_uv, heads):
    rows = o_lat.shape[1]
    return pl.pallas_call(
        functools.partial(_head_mm_kernel, nt=False),
        grid=(heads,),
        in_specs=[pl.BlockSpec((None, rows, KV_LORA), lambda h: (h, 0, 0)),
                  pl.BlockSpec((KV_LORA, V_DIM), lambda h: (0, h))],
        out_specs=pl.BlockSpec((rows, V_DIM), lambda h: (0, h)),
        out_shape=jax.ShapeDtypeStruct((rows, heads * V_DIM), F32),
        compiler_params=_cparams("parallel"),
        name="o_from_latent",
    )(o_lat, w_uv)


def _lane_chunks(s):
    w = s.shape[-1]
    if w % LANES or w == LANES:
        return [s]
    return [s[:, c * LANES:(c + 1) * LANES] for c in range(w // LANES)]


def _row_max(s):
    parts = _lane_chunks(s)
    return jnp.max(functools.reduce(jnp.maximum, parts), axis=-1, keepdims=True)


def _row_sum(s):
    parts = _lane_chunks(s)
    return jnp.sum(functools.reduce(jnp.add, parts), axis=-1, keepdims=True)


def _attn_sample_kernel(pt_ref, ql_ref, qp_ref, lat_hbm, pe_hbm, nl_ref, np_ref, o_ref,
                        lat_buf, pe_buf, lat_sem, pe_sem, k_ref, pe_ref, m_ref, l_ref, acc_ref,
                        *, layer, npg, ts, page):
    s_idx = pl.program_id(1)
    n_groups = pl.num_programs(1)
    lin = pl.program_id(0) * n_groups + s_idx
    slot = lin % 2

    def page_copies(group, slot_, j):
        pid = pt_ref[group * npg + j]
        rows = pl.ds(j * page, page)
        return (pltpu.make_async_copy(lat_hbm.at[layer, pid], lat_buf.at[slot_, rows, :], lat_sem.at[slot_]),
                pltpu.make_async_copy(pe_hbm.at[layer, pid], pe_buf.at[slot_, j], pe_sem.at[slot_]))

    def start_group(group, slot_):
        for j in range(npg):
            for cp in page_copies(group, slot_, j):
                cp.start()

    @pl.when(lin == 0)
    def _():
        start_group(lin, slot)

    @pl.when(lin + 1 < pl.num_programs(0) * n_groups)
    def _():
        start_group(lin + 1, 1 - slot)

    for j in range(npg):
        for cp in page_copies(lin, slot, j):
            cp.wait()

    @pl.when(s_idx == 0)
    def _():
        m_ref[...] = jnp.full(m_ref.shape, -jnp.inf, F32)
        l_ref[...] = jnp.zeros(l_ref.shape, F32)
        acc_ref[...] = jnp.zeros(acc_ref.shape, F32)

    ql = ql_ref[...]
    qp = qp_ref[...]

    def update(s, values):
        m_old = m_ref[...]
        m_new = jnp.maximum(m_old, _row_max(s))
        alpha = jnp.exp(m_old - m_new)
        p = jnp.exp(s - m_new)
        l_ref[...] = alpha * l_ref[...] + _row_sum(p)
        acc_ref[...] = alpha * acc_ref[...] + _dot(p.astype(BF16), values)
        m_ref[...] = m_new

    k_ref[...] = lat_buf[slot].astype(BF16)
    for j in range(npg):
        pe_ref[:, j * page:(j + 1) * page] = pe_buf[slot, j].astype(BF16)
    keys = k_ref[...]
    update(_dot_nt(ql, keys) + _dot(qp, pe_ref[...]), keys)

    @pl.when(s_idx == pl.num_programs(1) - 1)
    def _():
        kn = nl_ref[...].astype(BF16)
        s = _dot_nt(ql, kn) + _dot_nt(qp, np_ref[...].astype(BF16))
        t_q = lax.broadcasted_iota(jnp.int32, s.shape, 0) % ts
        t_k = lax.broadcasted_iota(jnp.int32, s.shape, 1)
        update(jnp.where(t_k <= t_q, s, -jnp.inf), kn)
        o_ref[...] = (acc_ref[...] / l_ref[...]).astype(o_ref.dtype)


def _attn_sample(page_table, q_lat, q_pe, cache_lat, cache_pe_t, layer, new_lat, new_pe, ts):
    bsz, qrows, _ = q_lat.shape
    n_pages = page_table.shape[1]
    page = cache_lat.shape[2]
    npg = math.gcd(PAGES_PER_STEP, n_pages)
    steps = n_pages // npg
    pt_flat = page_table.reshape(-1)

    per_b = lambda b, s, pt: (b, 0, 0)
    keys = npg * page
    grid_spec = pltpu.PrefetchScalarGridSpec(
        num_scalar_prefetch=1,
        grid=(bsz, steps),
        in_specs=[pl.BlockSpec((None, qrows, KV_LORA), per_b),
                  pl.BlockSpec((None, qrows, ROPE_DIM), per_b),
                  pl.BlockSpec(memory_space=pl.ANY), pl.BlockSpec(memory_space=pl.ANY),
                  pl.BlockSpec((None, 2 * ts, KV_LORA), per_b),
                  pl.BlockSpec((None, 2 * ts, ROPE_DIM), per_b)],
        out_specs=pl.BlockSpec((None, qrows, KV_LORA), per_b),
        scratch_shapes=[pltpu.VMEM((2, keys, KV_LORA), F32), pltpu.VMEM((2, npg, ROPE_DIM, page), F32),
                        pltpu.SemaphoreType.DMA((2,)), pltpu.SemaphoreType.DMA((2,)),
                        pltpu.VMEM((keys, KV_LORA), BF16), pltpu.VMEM((ROPE_DIM, keys), BF16),
                        pltpu.VMEM((qrows, 1), F32), pltpu.VMEM((qrows, 1), F32),
                        pltpu.VMEM((qrows, KV_LORA), F32)],
    )
    return pl.pallas_call(
        functools.partial(_attn_sample_kernel, layer=layer, npg=npg, ts=ts, page=page),
        grid_spec=grid_spec,
        out_shape=jax.ShapeDtypeStruct((bsz, qrows, KV_LORA), BF16),
        compiler_params=_cparams("arbitrary", "arbitrary"),
        name="attn_sample",
    )(pt_flat, q_lat, q_pe, cache_lat, cache_pe_t, new_lat, new_pe)


def _two_part_specs(tm, width, nt_p):
    return (pl.BlockSpec((tm, width), lambda i: (jnp.minimum(i, nt_p - 1), 0)),
            pl.BlockSpec((tm, width), lambda i: (jnp.maximum(i - nt_p, 0), 0)))


def _pick_part(nt_p, p_ref, s_ref):
    return jnp.where(pl.program_id(0) < nt_p, p_ref[...], s_ref[...])


def _hyb_out_kernel(ap_ref, as_ref, op_ref, os_ref, w_ref, x_ref, g_ref, b_ref, y_ref, *, alpha, nt_p):
    a = _pick_part(nt_p, ap_ref, as_ref).astype(BF16)
    o = _pick_part(nt_p, op_ref, os_ref).astype(BF16)
    ka = a.shape[1]
    mix = _dot(a, w_ref[0:ka, :]) + _dot(o, w_ref[ka:, :])
    y_ref[...] = _layer_norm_rows(alpha * x_ref[...] + mix, g_ref[...], b_ref[...])


def _hyb_out(a_p, a_s, o_p, o_s, w_out, x, g, b, alpha):
    m, d = x.shape
    n_p, n_s = a_p.shape[0], a_s.shape[0]
    tm = _tile(math.gcd(n_p, n_s), 256)
    nt_p = n_p // tm
    row = lambda i: (i, 0)
    fixed = lambda i: (0, 0)
    return pl.pallas_call(
        functools.partial(_hyb_out_kernel, alpha=alpha, nt_p=nt_p),
        grid=(m // tm,),
        in_specs=[*_two_part_specs(tm, a_p.shape[1], nt_p), *_two_part_specs(tm, o_p.shape[1], nt_p),
                  pl.BlockSpec(w_out.shape, fixed), pl.BlockSpec((tm, d), row),
                  pl.BlockSpec((1, d), fixed), pl.BlockSpec((1, d), fixed)],
        out_specs=pl.BlockSpec((tm, d), row),
        out_shape=jax.ShapeDtypeStruct((m, d), F32),
        compiler_params=_cparams("parallel"),
        name="hyb_out",
    )(a_p, a_s, o_p, o_s, w_out, x, g, b)


def _rec_out_kernel(op_ref, os_ref, gate_ref, ng_ref, w_ref, x_ref, g_ref, b_ref, y_ref, *, alpha, nt_p):
    gate = gate_ref[...]
    o = _pick_part(nt_p, op_ref, os_ref)
    on = _rms_norm_rows(o, ng_ref[...]) * (gate * jax.nn.sigmoid(gate))
    mix = _dot(on.astype(BF16), w_ref[...])
    y_ref[...] = _layer_norm_rows(alpha * x_ref[...] + mix, g_ref[...], b_ref[...])


def _rec_out(o_p, o_s, h, gate_col_block, norm_g, w_out, x, g, b, alpha):
    m, d = x.shape
    vw = o_p.shape[1]
    n_p, n_s = o_p.shape[0], o_s.shape[0]
    tm = _tile(math.gcd(n_p, n_s), 256)
    nt_p = n_p // tm
    row = lambda i: (i, 0)
    fixed = lambda i: (0, 0)
    return pl.pallas_call(
        functools.partial(_rec_out_kernel, alpha=alpha, nt_p=nt_p),
        grid=(m // tm,),
        in_specs=[*_two_part_specs(tm, vw, nt_p),
                  pl.BlockSpec((tm, vw), lambda i: (i, gate_col_block)),
                  pl.BlockSpec((1, vw), fixed), pl.BlockSpec(w_out.shape, fixed),
                  pl.BlockSpec((tm, d), row), pl.BlockSpec((1, d), fixed),
                  pl.BlockSpec((1, d), fixed)],
        out_specs=pl.BlockSpec((tm, d), row),
        out_shape=jax.ShapeDtypeStruct((m, d), F32),
        compiler_params=_cparams("parallel"),
        name="rec_out",
    )(o_p, o_s, h, norm_g, w_out, x, g, b)


def _gla_levels(chunk):
    m = chunk // 2
    out = []
    while m >= 1:
        out.append(m)
        m //= 2
    return tuple(out)


def _gla_prefix_matrix(chunk):
    t = np.arange(chunk)
    return (t[None, :] <= t[:, None]).astype(np.float32)


def _gla_block_ref(cum, m):
    c, w = cum.shape
    if 2 * m >= SUBLANES:
        parts = [jnp.broadcast_to(cum[b + m:b + m + 1, :], (2 * m, w)) for b in range(0, c, 2 * m)]
        return parts[0] if len(parts) == 1 else jnp.concatenate(parts, axis=0)
    row = lax.broadcasted_iota(jnp.int32, cum.shape, 0)
    if m == 1:
        return jnp.where(row % 2 == 1, cum, pltpu.roll(cum, c - 1, 0))
    assert m == 2
    sub = row % 4
    return jnp.where(sub == 2, cum,
                     jnp.where(sub == 3, pltpu.roll(cum, 1, 0),
                               jnp.where(sub == 1, pltpu.roll(cum, c - 1, 0), pltpu.roll(cum, c - 2, 0))))


def _gla_gates(q_raw, f_raw, lb):
    q = q_raw * jax.nn.sigmoid(q_raw) * (REC_DK ** -0.5)
    forget = lb + (1.0 - lb) * jax.nn.sigmoid(f_raw)
    return q, 1.0 - forget, jnp.log(forget)


def _split3(x):
    hi = x.astype(BF16)
    r1 = x - hi.astype(F32)
    mid = r1.astype(BF16)
    lo = (r1 - mid.astype(F32)).astype(BF16)
    return hi, mid, lo


def _gla_prompt_kernel(q_ref, f_ref, v_ref, lb_ref, pm_ref, o_ref, st_ref, s_ref, *, tt, gh):
    c = GLA_CHUNK
    dk = REC_DK
    i = pl.program_id(2)

    @pl.when(i == 0)
    def _():
        s_ref[...] = jnp.zeros(s_ref.shape, F32)

    lb = lb_ref[...]
    pm = pm_ref[...]
    levels = _gla_levels(c)
    wide = gh * dk
    row = lax.broadcasted_iota(jnp.int32, (c, wide), 0)
    ti = lax.broadcasted_iota(jnp.int32, (c, c), 0)
    si = lax.broadcasted_iota(jnp.int32, (c, c), 1)

    def pair_mask(m):
        return (ti // (2 * m) == si // (2 * m)) & ((ti // m) % 2 == 1) & ((si // m) % 2 == 0)

    def body(ci, carry):
        r0 = pl.multiple_of(ci * c, c)
        q_all, k_all, g_all = _gla_gates(q_ref[pl.ds(r0, c), :], f_ref[pl.ds(r0, c), :], lb)
        v_all = v_ref[pl.ds(r0, c), :].astype(BF16)
        hi, mid, lo = _split3(g_all)
        cum_all = _dot(pm, hi) + _dot(pm, mid) + _dot(pm, lo)
        last_all = cum_all[c - 1:c, :]
        z_all = [(jnp.where((row // m) % 2 == 1, q_all, k_all)
                  * jnp.exp(-jnp.abs(cum_all - _gla_block_ref(cum_all, m)))).astype(BF16) for m in levels]
        qe_all = (q_all * jnp.exp(cum_all)).astype(BF16)
        kd_all = (k_all * jnp.exp(last_all - cum_all)).astype(BF16)
        dec_all = jnp.exp(last_all)
        qb_all, kb_all = q_all.astype(BF16), k_all.astype(BF16)
        outs, states = [], []
        for hd in range(gh):
            sl = slice(hd * dk, (hd + 1) * dk)
            att = jnp.where(ti == si, _dot_nt(qb_all[:, sl], kb_all[:, sl]), 0.0)
            for m, z in zip(levels, z_all):
                att = att + jnp.where(pair_mask(m), _dot_nt(z[:, sl], z[:, sl]), 0.0)
            s_t = s_ref[hd * dk:(hd + 1) * dk, :]
            inter = _dot_nt(qe_all[:, sl], s_t.astype(BF16))
            outs.append(inter + _dot(att.astype(BF16), v_all[:, sl]))
            states.append(s_t * dec_all[:, sl] + _dot_tn(v_all[:, sl], kd_all[:, sl]))
        o_ref[pl.ds(r0, c), :] = jnp.concatenate(outs, axis=1)
        s_ref[...] = jnp.concatenate(states, axis=0)
        return carry

    lax.fori_loop(0, tt // c, body, 0)

    @pl.when(i == pl.num_programs(2) - 1)
    def _():
        for hd in range(gh):
            st_ref[hd] = s_ref[hd * dk:(hd + 1) * dk, :]


def _gla_prompt(h, lb, bsz, seq, heads):
    tt = _tile(seq, 512)
    assert tt % GLA_CHUNK == 0
    nt = seq // tt
    gh = math.gcd(GLA_HEADS_PER_STEP, heads)
    ng = heads // gh
    wide = gh * REC_DK
    pm = jnp.asarray(_gla_prefix_matrix(GLA_CHUNK), BF16)

    def col(base):
        return pl.BlockSpec((tt, wide), lambda b, hg, i: (b * nt + i, base + hg))

    return pl.pallas_call(
        functools.partial(_gla_prompt_kernel, tt=tt, gh=gh),
        grid=(bsz, ng, nt),
        in_specs=[col(0), col(ng), col(2 * ng),
                  pl.BlockSpec((1, wide), lambda b, hg, i: (0, hg)),
                  pl.BlockSpec(pm.shape, lambda b, hg, i: (0, 0))],
        out_specs=[pl.BlockSpec((tt, wide), lambda b, hg, i: (b * nt + i, hg)),
                   pl.BlockSpec((None, gh, REC_DK, REC_DK), lambda b, hg, i: (b, hg, 0, 0))],
        out_shape=[jax.ShapeDtypeStruct((bsz * seq, heads * REC_DK), F32),
                   jax.ShapeDtypeStruct((bsz, heads, REC_DK, REC_DK), F32)],
        scratch_shapes=[pltpu.VMEM((wide, REC_DK), F32)],
        compiler_params=_cparams("parallel", "parallel", "arbitrary"),
        name="gla_prompt",
    )(h, h, h, lb, pm)


def _gla_sample_kernel(q_ref, f_ref, v_ref, lb_ref, hs_ref, he_ref, st_ref, o_ref, sto_ref, *, ts, heads):
    dk = REC_DK
    q, k, g = _gla_gates(q_ref[...], f_ref[...], lb_ref[...])
    v = v_ref[...]
    t = lax.broadcasted_iota(jnp.int32, q.shape, 0)
    cum = g
    sh = 1
    while sh < ts:
        cum = cum + jnp.where(t >= sh, pltpu.roll(cum, sh, 0), 0.0)
        sh *= 2
    last = cum[ts - 1:ts, :]
    xs = []
    for s in range(ts):
        dec = jnp.exp(jnp.where(t >= s, cum - cum[s:s + 1, :], -jnp.inf))
        xs.append(q * k[s:s + 1, :] * dec)
    x = jnp.concatenate(xs, axis=0)
    x_hi = x.astype(BF16)
    x_lo = (x - x_hi.astype(F32)).astype(BF16)
    att = _dot(x_hi, hs_ref[...]) + _dot(x_lo, hs_ref[...])
    att_e = _dot(att.astype(BF16), he_ref[...])
    intra = jnp.zeros(q.shape, F32)
    for s in range(ts):
        intra = intra + att_e[s * ts:(s + 1) * ts, :] * v[s:s + 1, :]
    qe = (q * jnp.exp(cum)).astype(BF16)
    kd = (k * jnp.exp(last - cum)).astype(BF16)
    dec_last = jnp.exp(last)
    eye = (lax.broadcasted_iota(jnp.int32, (dk, dk), 0)
           == lax.broadcasted_iota(jnp.int32, (dk, dk), 1))
    vb = v.astype(BF16)
    for hd in range(heads):
        sl = slice(hd * dk, (hd + 1) * dk)
        s_h = st_ref[hd]
        o_ref[:, sl] = _dot(qe[:, sl], s_h.astype(BF16)) + intra[:, sl]
        d_col = jnp.sum(jnp.where(eye, dec_last[:, sl], 0.0), axis=1, keepdims=True)
        sto_ref[hd] = s_h * d_col + _dot_tn(kd[:, sl], vb[:, sl])


def _gla_sample(h, lb, state, j, row0, bsz, ts, heads):
    width = heads * REC_DK
    off = row0 // ts
    head_of = np.arange(width) // REC_DK
    hsum = (head_of[:, None] == np.arange(LANES)[None, :]).astype(np.float32)
    hs = jnp.asarray(hsum, BF16)
    he = jnp.asarray(hsum.T, BF16)

    def col(cb):
        return pl.BlockSpec((ts, width), lambda b: (off + b, cb))

    fixed = lambda b: (0, 0)
    st_in = pl.BlockSpec((None, None, heads, REC_DK, REC_DK), lambda b: (j, b, 0, 0, 0))
    st_out = pl.BlockSpec((None, heads, REC_DK, REC_DK), lambda b: (b, 0, 0, 0))
    return pl.pallas_call(
        functools.partial(_gla_sample_kernel, ts=ts, heads=heads),
        grid=(bsz,),
        in_specs=[col(0), col(1), col(2), pl.BlockSpec((1, width), fixed),
                  pl.BlockSpec(hs.shape, fixed), pl.BlockSpec(he.shape, fixed), st_in],
        out_specs=[pl.BlockSpec((ts, width), lambda b: (b, 0)), st_out],
        out_shape=[jax.ShapeDtypeStruct((bsz * ts, width), F32),
                   jax.ShapeDtypeStruct(state.shape[1:], F32)],
        compiler_params=_cparams("parallel"),
        name="gla_sample",
    )(h, h, h, lb, hs, he, state)


def _router_kernel(x_ref, wh_ref, wl_ref, b_ref, o_ref):
    x = x_ref[...]
    x_hi = x.astype(BF16)
    x_lo = (x - x_hi.astype(F32)).astype(BF16)
    wh = wh_ref[...]
    logits = _dot(x_hi, wh) + _dot(x_lo, wh) + _dot(x_hi, wl_ref[...]) + b_ref[...]
    lane = lax.broadcasted_iota(jnp.int32, logits.shape, 1).astype(F32)
    big = float(LANES)
    neg = -jnp.inf
    gl = jnp.where(lane < N_GROUPS, logits, neg)
    g_max = jnp.max(gl, axis=-1, keepdims=True)
    g_sel = jnp.min(jnp.where(gl == g_max, lane, big), axis=-1, keepdims=True)
    g_gate = 1.0 / jnp.sum(jnp.exp(gl - g_max), axis=-1, keepdims=True)
    lo = N_GROUPS + g_sel * EXPERTS_PER_GROUP
    el = jnp.where((lane >= lo) & (lane < lo + EXPERTS_PER_GROUP), logits, neg)
    e_max = jnp.max(el, axis=-1, keepdims=True)
    i1 = jnp.min(jnp.where(el == e_max, lane, big), axis=-1, keepdims=True)
    denom = jnp.sum(jnp.exp(el - e_max), axis=-1, keepdims=True)
    el2 = jnp.where(lane == i1, neg, el)
    m2 = jnp.max(el2, axis=-1, keepdims=True)
    i2 = jnp.min(jnp.where(el2 == m2, lane, big), axis=-1, keepdims=True)
    p1 = 1.0 / denom
    p2 = jnp.exp(m2 - e_max) / denom
    w1 = g_gate * p1 / (p1 + p2)
    w2 = g_gate * p2 / (p1 + p2)
    e1 = i1 - N_GROUPS
    e2 = i2 - N_GROUPS
    o_ref[...] = jnp.where(lane == 0, e1, jnp.where(lane == 1, e2,
                           jnp.where(lane == 2, w1, jnp.where(lane == 3, w2, 0.0))))


def _router(x, w_hi, w_lo, bias):
    m, d = x.shape
    tm = _tile(m, 512)
    fixed = lambda i: (0, 0)
    return pl.pallas_call(
        _router_kernel,
        grid=(m // tm,),
        in_specs=[pl.BlockSpec((tm, d), lambda i: (i, 0)), pl.BlockSpec((d, LANES), fixed),
                  pl.BlockSpec((d, LANES), fixed), pl.BlockSpec((1, LANES), fixed)],
        out_specs=pl.BlockSpec((tm, LANES), lambda i: (i, 0)),
        out_shape=jax.ShapeDtypeStruct((m, LANES), F32),
        compiler_params=_cparams("parallel"),
        name="router",
    )(x, w_hi, w_lo, bias)


def _expert_weights(t, te_ref, first_ref, slot_ref, nxt_ref, layer, hbm_refs, buf_refs, sem_refs, bf_refs):
    def copies(expert, slot):
        return [pltpu.make_async_copy(hbm.at[layer, expert], buf.at[slot], sem.at[slot])
                for hbm, buf, sem in zip(hbm_refs, buf_refs, sem_refs)]

    @pl.when(t == 0)
    def _():
        for cp in copies(te_ref[0], slot_ref[0]):
            cp.start()

    @pl.when(first_ref[t] == 1)
    def _():
        slot = slot_ref[t]

        @pl.when(nxt_ref[t] >= 0)
        def _():
            for cp in copies(nxt_ref[t], 1 - slot):
                cp.start()

        for cp in copies(te_ref[t], slot):
            cp.wait()
        for buf, bf in zip(buf_refs, bf_refs):
            bf[...] = buf[slot].astype(BF16)


def _moe_up_kernel(te_ref, first_ref, slot_ref, nxt_ref, nu_ref, x_ref, wg_hbm, wu_hbm, h_ref,
                   wg_buf, wu_buf, wg_sem, wu_sem, wgb_ref, wub_ref, *, layer):
    t = pl.program_id(0)

    @pl.when(t < nu_ref[0])
    def _():
        _expert_weights(t, te_ref, first_ref, slot_ref, nxt_ref, layer, (wg_hbm, wu_hbm),
                        (wg_buf, wu_buf), (wg_sem, wu_sem), (wgb_ref, wub_ref))
        x = x_ref[...].astype(BF16)
        gate = _dot(x, wgb_ref[...])
        up = _dot(x, wub_ref[...])
        h_ref[...] = (gate * jax.nn.sigmoid(gate) * up).astype(h_ref.dtype)

    @pl.when(t >= nu_ref[0])
    def _():
        h_ref[...] = jnp.zeros(h_ref.shape, h_ref.dtype)


def _moe_down_kernel(te_ref, first_ref, slot_ref, nxt_ref, nu_ref, h_ref, wd_hbm, y_ref,
                     wd_buf, wd_sem, wdb_ref, *, layer):
    t = pl.program_id(0)

    @pl.when(t < nu_ref[0])
    def _():
        _expert_weights(t, te_ref, first_ref, slot_ref, nxt_ref, layer, (wd_hbm,), (wd_buf,),
                        (wd_sem,), (wdb_ref,))
        y_ref[...] = _dot(h_ref[...], wdb_ref[...])

    @pl.when(t >= nu_ref[0])
    def _():
        y_ref[...] = jnp.zeros(y_ref.shape, y_ref.dtype)


def _moe_experts(xs, plan, w_gate, w_up, w_down, layer):
    cap, d = xs.shape
    de = w_gate.shape[-1]
    n_tiles = cap // MOE_TILE
    n_plan = len(plan)

    def live(t, *refs):
        return (jnp.minimum(t, refs[n_plan - 1][0] - 1), 0)

    own = lambda t, *refs: (t, 0)
    hbm = pl.BlockSpec(memory_space=pl.ANY)
    two = lambda *shape: pltpu.VMEM((2,) + shape, F32)
    sem = pltpu.SemaphoreType.DMA((2,))

    hid = pl.pallas_call(
        functools.partial(_moe_up_kernel, layer=layer),
        grid_spec=pltpu.PrefetchScalarGridSpec(
            num_scalar_prefetch=n_plan,
            grid=(n_tiles,),
            in_specs=[pl.BlockSpec((MOE_TILE, d), live), hbm, hbm],
            out_specs=pl.BlockSpec((MOE_TILE, de), own),
            scratch_shapes=[two(d, de), two(d, de), sem, sem,
                            pltpu.VMEM((d, de), BF16), pltpu.VMEM((d, de), BF16)]),
        out_shape=jax.ShapeDtypeStruct((cap, de), BF16),
        compiler_params=_cparams("arbitrary"),
        name="moe_up",
    )(*plan, xs, w_gate, w_up)

    return pl.pallas_call(
        functools.partial(_moe_down_kernel, layer=layer),
        grid_spec=pltpu.PrefetchScalarGridSpec(
            num_scalar_prefetch=n_plan,
            grid=(n_tiles,),
            in_specs=[pl.BlockSpec((MOE_TILE, de), live), hbm],
            out_specs=pl.BlockSpec((MOE_TILE, d), own),
            scratch_shapes=[two(de, d), sem, pltpu.VMEM((de, d), BF16)]),
        out_shape=jax.ShapeDtypeStruct((cap, d), F32),
        compiler_params=_cparams("arbitrary"),
        name="moe_down",
    )(*plan, hid, w_down)


def _moe_ln_kernel(x_ref, y0_ref, y1_ref, r_ref, g_ref, b_ref, o_ref, *, alpha):
    r = r_ref[...]
    moe = r[:, TOP_K:TOP_K + 1] * y0_ref[...] + r[:, TOP_K + 1:TOP_K + 2] * y1_ref[...]
    o_ref[...] = _layer_norm_rows(alpha * x_ref[...] + moe, g_ref[...], b_ref[...])


def _moe_ln(x, y2, routed, g, b, alpha):
    m, d = x.shape
    tm = _tile(m, 512)
    nt = m // tm
    row = lambda i: (i, 0)
    fixed = lambda i: (0, 0)
    return pl.pallas_call(
        functools.partial(_moe_ln_kernel, alpha=alpha),
        grid=(nt,),
        in_specs=[pl.BlockSpec((tm, d), row), pl.BlockSpec((tm, d), row),
                  pl.BlockSpec((tm, d), lambda i: (nt + i, 0)), pl.BlockSpec((tm, LANES), row),
                  pl.BlockSpec((1, d), fixed), pl.BlockSpec((1, d), fixed)],
        out_specs=pl.BlockSpec((tm, d), row),
        out_shape=jax.ShapeDtypeStruct((m, d), F32),
        compiler_params=_cparams("parallel"),
        name="moe_ln",
    )(x, y2, y2, routed, g, b)


def _moe_layer(x, layer, w_rg, b_rg, w_re, b_re, w_gate, w_up, w_down, ln_g, ln_b, alpha):
    m, d = x.shape
    wr = jnp.concatenate([w_rg, w_re], axis=1)
    wr = jnp.pad(wr, ((0, 0), (0, LANES - wr.shape[1])))
    wr_hi = wr.astype(BF16)
    wr_lo = (wr - wr_hi.astype(F32)).astype(BF16)
    br = jnp.pad(jnp.concatenate([b_rg, b_re]), (0, LANES - N_GROUPS - N_EXPERTS))[None, :]
    routed = _router(x, wr_hi, wr_lo, br)
    ids = routed[:, 0:TOP_K].astype(jnp.int32)

    n_assign = m * TOP_K
    n_tiles = -(-n_assign // MOE_TILE) + N_EXPERTS
    cap = n_tiles * MOE_TILE
    flat_e = ids.reshape(-1)
    onehot = (flat_e[:, None] == jnp.arange(N_EXPERTS, dtype=jnp.int32)[None, :]).astype(jnp.int32)
    csum = jnp.cumsum(onehot, axis=0)
    counts = csum[-1]
    rank = jnp.take_along_axis(csum, flat_e[:, None], axis=1)[:, 0] - 1
    tiles_per = (counts + MOE_TILE - 1) // MOE_TILE
    tile_ends = jnp.cumsum(tiles_per)
    pad_starts = (tile_ends - tiles_per) * MOE_TILE
    dest = pad_starts[flat_e] + rank
    n_used = tile_ends[-1:].astype(jnp.int32)
    tile_idx = jnp.arange(n_tiles, dtype=jnp.int32)
    tile_e = jnp.sum((tile_ends[None, :] <= tile_idx[:, None]).astype(jnp.int32), axis=1)
    last_e = jnp.max(jnp.where(counts > 0, jnp.arange(N_EXPERTS), 0))
    tile_e = jnp.minimum(tile_e, last_e).astype(jnp.int32)
    prev_e = jnp.concatenate([jnp.full((1,), -1, jnp.int32), tile_e[:-1]])
    first = ((tile_e != prev_e) & (tile_idx < n_used[0])).astype(jnp.int32)
    slot = (jnp.cumsum(first) - 1) % 2
    next_tile = tile_ends[tile_e]
    nxt = jnp.where(next_tile < n_used[0], tile_e[jnp.minimum(next_tile, n_tiles - 1)], -1)
    plan = (tile_e, first, slot.astype(jnp.int32), nxt.astype(jnp.int32), n_used)
    tok = jnp.arange(n_assign, dtype=jnp.int32) // TOP_K
    slot_tok = (jnp.arange(cap, dtype=jnp.int32) % m).at[dest].set(tok)

    xs = x[slot_tok]
    y = _moe_experts(xs, plan, w_gate, w_up, w_down, layer)
    y2 = y[dest.reshape(m, TOP_K).T.reshape(-1)]
    return _moe_ln(x, y2, routed, ln_g[None, :], ln_b[None, :], alpha)


def _rope_tables(pos):
    half = ROPE_DIM // 2
    inv_freq = ROPE_THETA ** (-jnp.arange(half, dtype=F32) / half)
    ang = pos.astype(F32)[:, None] * inv_freq
    cos, sin = jnp.cos(ang), jnp.sin(ang)
    z = jnp.zeros_like(cos)
    pad = jnp.zeros((pos.shape[0], LANES - ROPE_DIM), F32)
    c = jnp.concatenate([cos, cos, pad], axis=1)
    sa = jnp.concatenate([-sin, z, pad], axis=1)
    sb = jnp.concatenate([z, sin, pad], axis=1)
    return c, sa, sb


def _hybrid_layer(x, dims, rope, conv_state, cache_lat, cache_pe, page_table, j,
                  w_in, conv_w, q_g, kv_g, w_q_up, w_uk, w_uv, w_out, ln_g, ln_b, alpha):
    bp, sp, bs, ts = dims
    n_p = bp * sp
    cw = conv_w.shape[1]
    heads = w_q_up.shape[1]
    n_main = 3 * cw + Q_LORA + KV_LORA

    w_pe = jnp.pad(w_in[j, :, n_main:], ((0, 0), (0, LANES - ROPE_DIM))).astype(BF16)
    h = _mm(x, w_in, layer=j, n=n_main, name="hyb_in")
    h_pe = _mm(x, w_pe, name="hyb_in_pe")

    a_p, conv_p = _conv_prompt(h, conv_w, bp, sp, cw)
    a_s, conv_s = _conv_sample(h, conv_state, conv_w, n_p, bs, ts, cw)

    w_q_cat = jnp.pad(w_q_up, ((0, 0), (0, 0), (0, HEAD_PAD - NOPE_DIM - ROPE_DIM)))
    w_q_cat = w_q_cat.reshape(Q_LORA, heads * HEAD_PAD).astype(BF16)
    q_cat = _q_proj(h, 3 * cw // Q_LORA, q_g[None, :], w_q_cat, *rope, heads)
    lat, kpe = _kv_norm(h, (3 * cw + Q_LORA) // KV_LORA, h_pe, kv_g[None, :], *rope)
    w_uk2 = w_uk.reshape(KV_LORA, heads * NOPE_DIM).astype(BF16)
    w_uv2 = w_uv.reshape(KV_LORA, heads * V_DIM).astype(BF16)
    w_uv_t = w_uv2.T

    k_cat, v_t = _kv_expand(lat, kpe, w_uk2, w_uv_t, n_p, heads)
    o_p = _attn_prompt(q_cat, k_cat, v_t, bp, sp, heads)

    n_s = bs * ts
    q_lat = _q_latent(q_cat, w_uk2, n_p, n_s, heads)
    q_lat = q_lat.reshape(heads, bs, ts, KV_LORA).transpose(1, 0, 2, 3).reshape(bs, heads * ts, KV_LORA)
    q_pe = q_cat[n_p:].reshape(bs, ts, heads, HEAD_PAD)[..., NOPE_DIM:NOPE_DIM + ROPE_DIM]
    q_pe = q_pe.transpose(0, 2, 1, 3).reshape(bs, heads * ts, ROPE_DIM)
    lat_s = lat[n_p:].reshape(bs, ts, KV_LORA)
    kpe_s = kpe[n_p:, :ROPE_DIM].reshape(bs, ts, ROPE_DIM)
    new_lat = jnp.pad(lat_s, ((0, 0), (0, ts), (0, 0)))
    new_pe = jnp.pad(kpe_s, ((0, 0), (0, ts), (0, 0)))
    o_lat = _attn_sample(page_table, q_lat, q_pe, cache_lat, cache_pe, j, new_lat, new_pe, ts)
    o_lat = o_lat.reshape(bs, heads, ts, KV_LORA).transpose(1, 0, 2, 3).reshape(heads, n_s, KV_LORA)
    o_s = _o_from_latent(o_lat, w_uv2, heads)

    y = _hyb_out(a_p, a_s, o_p, o_s, w_out.astype(BF16), x, ln_g[None, :], ln_b[None, :], alpha)
    outs = (lat[:n_p].reshape(bp, sp, KV_LORA), kpe[:n_p, :ROPE_DIM].reshape(bp, sp, ROPE_DIM),
            lat_s, kpe_s, conv_p, conv_s)
    return y, outs


def _rec_layer(x, dims, state, j, lb, w_in, norm_g, w_out, ln_g, ln_b, alpha):
    bp, sp, bs, ts = dims
    n_p = bp * sp
    width = w_in.shape[-1] // 4
    heads = width // REC_DK
    h = _mm(x, w_in, layer=j, name="rec_in")
    lb2 = lb[None, :]
    o_p, st_p = _gla_prompt(h, lb2, bp, sp, heads)
    o_s, st_s = _gla_sample(h, lb2, state, j, n_p, bs, ts, heads)
    y = _rec_out(o_p, o_s, h, 3, norm_g[None, :], w_out.astype(BF16), x, ln_g[None, :], ln_b[None, :], alpha)
    return y, (jnp.swapaxes(st_p, -1, -2), st_s)


def kernel(x_prompt, x_sample, cache_kv_latent, cache_k_rope, state_conv, state_hgrn, page_table,
           w_in_hyb, conv_w, q_norm_g, kv_norm_g, w_q_up, w_uk, w_uv, w_out_hyb,
           w_in_rec, lb_logits, rec_norm_g, w_out_rec,
           ln1_g, ln1_b, ln2_g, ln2_b,
           w_router_group, b_router_group, w_router_expert, b_router_expert,
           w_gate, w_up, w_down):
    bp, sp, d = x_prompt.shape
    bs, ts, _ = x_sample.shape
    depth = ln1_g.shape[0]
    past_len = page_table.shape[1] * cache_kv_latent.shape[2]
    dims = (bp, sp, bs, ts)
    n_p = bp * sp
    alpha = (2 * depth) ** 0.25

    lower = jnp.cumsum(jax.nn.softmax(lb_logits.astype(F32), axis=0), axis=0)
    lower = lower - lower[0]

    pos = jnp.concatenate([jnp.tile(jnp.arange(sp), bp), jnp.tile(past_len + jnp.arange(ts), bs)])
    rope = _rope_tables(pos)

    x = jnp.concatenate([x_prompt.reshape(n_p, d), x_sample.reshape(bs * ts, d)], axis=0)
    cache_pe_t = jnp.swapaxes(cache_k_rope, 2, 3)
    hyb_outs, rec_outs = [], []
    for layer in range(depth):
        j = layer // 2
        if layer % 2 == 0:
            x, outs = _hybrid_layer(x, dims, rope, state_conv[j], cache_kv_latent, cache_pe_t, page_table, j,
                                    w_in_hyb, conv_w[j], q_norm_g[j], kv_norm_g[j], w_q_up[j], w_uk[j],
                                    w_uv[j], w_out_hyb[j], ln1_g[layer], ln1_b[layer], alpha)
            hyb_outs.append(outs)
        else:
            x, outs = _rec_layer(x, dims, state_hgrn, j, lower[layer], w_in_rec, rec_norm_g[j],
                                 w_out_rec[j], ln1_g[layer], ln1_b[layer], alpha)
            rec_outs.append(outs)
        x = _moe_layer(x, layer, w_router_group[layer], b_router_group[layer], w_router_expert[layer],
                       b_router_expert[layer], w_gate, w_up, w_down, ln2_g[layer], ln2_b[layer], alpha)

    stack = lambda parts, i: jnp.stack([p[i] for p in parts])
    return (x[:n_p].reshape(bp, sp, d), x[n_p:].reshape(bs, ts, d),
            stack(hyb_outs, 0), stack(hyb_outs, 1), stack(hyb_outs, 2), stack(hyb_outs, 3),
            stack(hyb_outs, 4), stack(hyb_outs, 5), stack(rec_outs, 0), stack(rec_outs, 1))
```

```python
import functools
import math

import numpy as np
import jax
import jax.numpy as jnp
from jax import lax
from jax.experimental import pallas as pl
from jax.experimental.pallas import tpu as pltpu

F32 = jnp.float32
BF16 = jnp.bfloat16

CONV_K = 3
Q_LORA = 512
KV_LORA = 512
NOPE_DIM = 128
ROPE_DIM = 64
V_DIM = 128
REC_DK = 128
N_GROUPS = 4
EXPERTS_PER_GROUP = 8
N_EXPERTS = N_GROUPS * EXPERTS_PER_GROUP
TOP_K = 2
ROPE_THETA = 10000.0
LN_EPS = 1e-5
RMS_EPS = 1e-6
MLA_SCALE = (NOPE_DIM + ROPE_DIM) ** -0.5

LANES = 128
SUBLANES = 8
VMEM_LIMIT_BYTES = 52 * 1024 * 1024

HEAD_PAD = 2 * LANES
GLA_CHUNK = 64
GLA_HEADS_PER_STEP = 16
MOE_TILE = 256
ATTN_TQ = 512
ATTN_HEADS_PER_STEP = 2
PAGES_PER_STEP = 16


def _cparams(*sem):
    return pltpu.CompilerParams(dimension_semantics=sem, vmem_limit_bytes=VMEM_LIMIT_BYTES)


def _tile(n, pref):
    if n <= pref:
        return n
    for t in range(pref, 7, -1):
        if n % t == 0 and t % 8 == 0:
            return t
    return n


def _dot(a, b):
    return jnp.dot(a, b, preferred_element_type=F32)


def _dot_nt(a, b):
    return lax.dot_general(a, b, (((1,), (1,)), ((), ())), preferred_element_type=F32)


def _dot_tn(a, b):
    return lax.dot_general(a, b, (((0,), (0,)), ((), ())), preferred_element_type=F32)


def _mm_kernel(x_ref, w_ref, o_ref, xb_ref):
    @pl.when(pl.program_id(1) == 0)
    def _():
        xb_ref[...] = x_ref[...].astype(BF16)

    o_ref[...] = _dot(xb_ref[...], w_ref[...].astype(BF16)).astype(o_ref.dtype)


def _mm(x, w, *, layer=None, n=None, tm=1536, tn=512, out_dtype=F32, name="mm"):
    m, k = x.shape
    n = w.shape[-1] if n is None else n
    tm = _tile(m, tm)
    tn = _tile(n, tn)
    if layer is None:
        w_spec = pl.BlockSpec((k, tn), lambda i, j: (0, j))
    else:
        w_spec = pl.BlockSpec((None, k, tn), lambda i, j: (layer, 0, j))
    return pl.pallas_call(
        _mm_kernel,
        grid=(m // tm, n // tn),
        in_specs=[pl.BlockSpec((tm, k), lambda i, j: (i, 0)), w_spec],
        out_specs=pl.BlockSpec((tm, tn), lambda i, j: (i, j)),
        out_shape=jax.ShapeDtypeStruct((m, n), out_dtype),
        scratch_shapes=[pltpu.VMEM((tm, k), BF16)],
        compiler_params=_cparams("parallel", "arbitrary"),
        name=name,
    )(x, w)


def _layer_norm_rows(z, g, b):
    mu = jnp.mean(z, axis=-1, keepdims=True)
    zc = z - mu
    var = jnp.mean(zc * zc, axis=-1, keepdims=True)
    return zc * lax.rsqrt(var + LN_EPS) * g + b


def _rms_norm_rows(z, g):
    return z * lax.rsqrt(jnp.mean(z * z, axis=-1, keepdims=True) + RMS_EPS) * g


def _conv_prompt_kernel(cx_ref, gb_ref, gc_ref, cxp_ref, gcp_ref, w_ref, a_ref, st_ref, *, tt):
    i = pl.program_id(1)
    u = gc_ref[...] * cx_ref[...]
    prev = jnp.where(i > 0, gcp_ref[...] * cxp_ref[...], 0.0)
    row = lax.broadcasted_iota(jnp.int32, u.shape, 0)
    u1 = jnp.where(row == 0, prev[7:8], pltpu.roll(u, 1, 0))
    u2 = jnp.where(row == 0, prev[6:7], jnp.where(row == 1, prev[7:8], pltpu.roll(u, 2, 0)))
    v = w_ref[0:1, :] * u2 + w_ref[1:2, :] * u1 + w_ref[2:3, :] * u
    a_ref[...] = gb_ref[...] * v

    @pl.when(i == pl.num_programs(1) - 1)
    def _():
        st_ref[...] = u[tt - (CONV_K - 1):tt]


def _conv_prompt(h, conv_w, bsz, seq, cw):
    tt = _tile(seq, 512)
    nt = seq // tt
    sub = tt // SUBLANES

    def main(c):
        return pl.BlockSpec((tt, cw), lambda b, i: (b * nt + i, c))

    def halo(c):
        return pl.BlockSpec((SUBLANES, cw), lambda b, i: (jnp.maximum((b * nt + i) * sub - 1, 0), c))

    return pl.pallas_call(
        functools.partial(_conv_prompt_kernel, tt=tt),
        grid=(bsz, nt),
        in_specs=[main(0), main(1), main(2), halo(0), halo(2),
                  pl.BlockSpec((CONV_K, cw), lambda b, i: (0, 0))],
        out_specs=[pl.BlockSpec((tt, cw), lambda b, i: (b * nt + i, 0)),
                   pl.BlockSpec((None, CONV_K - 1, cw), lambda b, i: (b, 0, 0))],
        out_shape=[jax.ShapeDtypeStruct((bsz * seq, cw), F32),
                   jax.ShapeDtypeStruct((bsz, CONV_K - 1, cw), F32)],
        compiler_params=_cparams("parallel", "arbitrary"),
        name="conv_prompt",
    )(h, h, h, h, h, conv_w)


def _conv_sample_kernel(cx_ref, gb_ref, gc_ref, st_ref, w_ref, a_ref, sto_ref, *, nb, ts):
    cw = cx_ref.shape[-1]
    u = (gc_ref[...] * cx_ref[...]).reshape(nb, ts, cw)
    st = st_ref[...]
    t = lax.broadcasted_iota(jnp.int32, u.shape, 1)
    s0 = st[:, 0:1, :]
    s1 = st[:, 1:2, :]
    u1 = jnp.where(t == 0, s1, pltpu.roll(u, 1, 1))
    u2 = jnp.where(t == 0, s0, jnp.where(t == 1, s1, pltpu.roll(u, 2, 1)))
    w = w_ref[...]
    v = w[0:1, :][None] * u2 + w[1:2, :][None] * u1 + w[2:3, :][None] * u
    a_ref[...] = gb_ref[...] * v.reshape(nb * ts, cw)
    sto_ref[...] = u[:, ts - (CONV_K - 1):ts, :]


def _conv_sample(h, state, conv_w, row0, bsz, ts, cw):
    nb = _tile(bsz, 64)
    rows = nb * ts
    off = row0 // rows

    def main(c):
        return pl.BlockSpec((rows, cw), lambda i: (off + i, c))

    return pl.pallas_call(
        functools.partial(_conv_sample_kernel, nb=nb, ts=ts),
        grid=(bsz // nb,),
        in_specs=[main(0), main(1), main(2),
                  pl.BlockSpec((nb, CONV_K - 1, cw), lambda i: (i, 0, 0)),
                  pl.BlockSpec((CONV_K, cw), lambda i: (0, 0))],
        out_specs=[pl.BlockSpec((rows, cw), lambda i: (i, 0)),
                   pl.BlockSpec((nb, CONV_K - 1, cw), lambda i: (i, 0, 0))],
        out_shape=[jax.ShapeDtypeStruct((bsz * ts, cw), F32),
                   jax.ShapeDtypeStruct((bsz, CONV_K - 1, cw), F32)],
        compiler_params=_cparams("parallel"),
        name="conv_sample",
    )(h, h, h, state, conv_w)


def _rope_lanes(x, c, sa, sb):
    half = ROPE_DIM // 2
    return x * c + pltpu.roll(x, LANES - half, 1) * sa + pltpu.roll(x, half, 1) * sb


def _q_proj_kernel(qc_ref, g_ref, w_ref, c_ref, sa_ref, sb_ref, o_ref, *, heads):
    xn = _rms_norm_rows(qc_ref[...], g_ref[...]).astype(BF16)
    q = _dot(xn, w_ref[...]) * MLA_SCALE
    c, sa, sb = c_ref[...], sa_ref[...], sb_ref[...]
    for hd in range(heads):
        lo = hd * HEAD_PAD
        o_ref[:, lo:lo + LANES] = q[:, lo:lo + LANES].astype(o_ref.dtype)
        pe = _rope_lanes(q[:, lo + LANES:lo + HEAD_PAD], c, sa, sb)
        o_ref[:, lo + LANES:lo + HEAD_PAD] = pe.astype(o_ref.dtype)


def _q_proj(h, col_block, g, w_cat, rope_c, rope_sa, rope_sb, heads):
    m = h.shape[0]
    tm = _tile(m, 512)
    n = heads * HEAD_PAD
    row = lambda i: (i, 0)
    return pl.pallas_call(
        functools.partial(_q_proj_kernel, heads=heads),
        grid=(m // tm,),
        in_specs=[pl.BlockSpec((tm, Q_LORA), lambda i: (i, col_block)),
                  pl.BlockSpec((1, Q_LORA), lambda i: (0, 0)),
                  pl.BlockSpec((Q_LORA, n), lambda i: (0, 0)),
                  pl.BlockSpec((tm, LANES), row), pl.BlockSpec((tm, LANES), row),
                  pl.BlockSpec((tm, LANES), row)],
        out_specs=pl.BlockSpec((tm, n), row),
        out_shape=jax.ShapeDtypeStruct((m, n), BF16),
        compiler_params=_cparams("parallel"),
        name="q_proj",
    )(h, g, w_cat, rope_c, rope_sa, rope_sb)


def _kv_norm_kernel(kv_ref, pe_ref, g_ref, c_ref, sa_ref, sb_ref, lat_ref, kpe_ref):
    lat_ref[...] = _rms_norm_rows(kv_ref[...], g_ref[...])
    kpe_ref[...] = _rope_lanes(pe_ref[...], c_ref[...], sa_ref[...], sb_ref[...])


def _kv_norm(h, col_block, h_pe, g, rope_c, rope_sa, rope_sb):
    m = h.shape[0]
    tm = _tile(m, 1024)
    row = lambda i: (i, 0)
    return pl.pallas_call(
        _kv_norm_kernel,
        grid=(m // tm,),
        in_specs=[pl.BlockSpec((tm, KV_LORA), lambda i: (i, col_block)),
                  pl.BlockSpec((tm, LANES), row),
                  pl.BlockSpec((1, KV_LORA), lambda i: (0, 0)),
                  pl.BlockSpec((tm, LANES), row), pl.BlockSpec((tm, LANES), row),
                  pl.BlockSpec((tm, LANES), row)],
        out_specs=[pl.BlockSpec((tm, KV_LORA), row), pl.BlockSpec((tm, LANES), row)],
        out_shape=[jax.ShapeDtypeStruct((m, KV_LORA), F32),
                   jax.ShapeDtypeStruct((m, LANES), F32)],
        compiler_params=_cparams("parallel"),
        name="kv_norm",
    )(h, h_pe, g, rope_c, rope_sa, rope_sb)


def _kv_expand_kernel(lat_ref, kpe_ref, wk_ref, wvt_ref, k_ref, vt_ref, *, heads):
    lat = lat_ref[...].astype(BF16)
    kn = _dot(lat, wk_ref[...])
    kpe = kpe_ref[...].astype(k_ref.dtype)
    for hd in range(heads):
        lo = hd * HEAD_PAD
        k_ref[:, lo:lo + LANES] = kn[:, hd * NOPE_DIM:(hd + 1) * NOPE_DIM].astype(k_ref.dtype)
        k_ref[:, lo + LANES:lo + HEAD_PAD] = kpe
    vt_ref[...] = _dot_nt(wvt_ref[...], lat).astype(vt_ref.dtype)


def _kv_expand(lat, kpe, w_uk, w_uv_t, rows, heads):
    tm = _tile(rows, 512)
    row = lambda i: (i, 0)
    fixed = lambda i: (0, 0)
    return pl.pallas_call(
        functools.partial(_kv_expand_kernel, heads=heads),
        grid=(rows // tm,),
        in_specs=[pl.BlockSpec((tm, KV_LORA), row), pl.BlockSpec((tm, LANES), row),
                  pl.BlockSpec((KV_LORA, heads * NOPE_DIM), fixed),
                  pl.BlockSpec((heads * V_DIM, KV_LORA), fixed)],
        out_specs=[pl.BlockSpec((tm, heads * HEAD_PAD), row),
                   pl.BlockSpec((heads * V_DIM, tm), lambda i: (0, i))],
        out_shape=[jax.ShapeDtypeStruct((rows, heads * HEAD_PAD), BF16),
                   jax.ShapeDtypeStruct((heads * V_DIM, rows), BF16)],
        compiler_params=_cparams("parallel"),
        name="kv_expand",
    )(lat, kpe, w_uk, w_uv_t)


def _attn_prompt_kernel(q_ref, k_ref, vt_ref, o_ref, m_ref, l_ref, acc_ref, *, tq, nh):
    i = pl.program_id(2)
    m_ref[...] = jnp.full(m_ref.shape, -jnp.inf, F32)
    l_ref[...] = jnp.zeros(l_ref.shape, F32)
    acc_ref[...] = jnp.zeros(acc_ref.shape, F32)
    q_all = q_ref[...]

    def chunk(c, masked):
        r0 = pl.multiple_of(c * tq, tq)
        k_all = k_ref[pl.ds(r0, tq), :]
        vt_all = vt_ref[:, pl.ds(r0, tq)]
        m_all, l_all, acc_all = m_ref[...], l_ref[...], acc_ref[...]
        ms, ls, accs = [], [], []
        for hd in range(nh):
            hp = slice(hd * HEAD_PAD, (hd + 1) * HEAD_PAD)
            hv = slice(hd * V_DIM, (hd + 1) * V_DIM)
            s = _dot_nt(k_all[:, hp], q_all[:, hp])
            if masked:
                key = lax.broadcasted_iota(jnp.int32, s.shape, 0)
                qry = lax.broadcasted_iota(jnp.int32, s.shape, 1)
                s = jnp.where(key <= qry, s, -jnp.inf)
            m_old = m_all[hd:hd + 1, :]
            m_new = jnp.maximum(m_old, jnp.max(s, axis=0, keepdims=True))
            alpha = jnp.exp(m_old - m_new)
            p = jnp.exp(s - m_new)
            ms.append(m_new)
            ls.append(alpha * l_all[hd:hd + 1, :] + jnp.sum(p, axis=0, keepdims=True))
            accs.append(alpha * acc_all[hv, :] + _dot(vt_all[hv, :], p.astype(BF16)))
        m_ref[...] = jnp.concatenate(ms, axis=0)
        l_ref[...] = jnp.concatenate(ls, axis=0)
        acc_ref[...] = jnp.concatenate(accs, axis=0)

    def body(c, carry):
        chunk(c, False)
        return carry

    lax.fori_loop(0, i, body, 0)
    chunk(i, True)
    l_all = l_ref[...]
    o_ref[...] = jnp.concatenate(
        [(acc_ref[hd * V_DIM:(hd + 1) * V_DIM, :] / l_all[hd:hd + 1, :]).T for hd in range(nh)],
        axis=1).astype(o_ref.dtype)


def _attn_prompt(q_cat, k_cat, v_t, bsz, seq, heads):
    tq = _tile(seq, ATTN_TQ)
    nq = seq // tq
    nh = math.gcd(ATTN_HEADS_PER_STEP, heads)
    return pl.pallas_call(
        functools.partial(_attn_prompt_kernel, tq=tq, nh=nh),
        grid=(bsz, heads // nh, nq),
        in_specs=[pl.BlockSpec((tq, nh * HEAD_PAD), lambda b, h, i: (b * nq + i, h)),
                  pl.BlockSpec((seq, nh * HEAD_PAD), lambda b, h, i: (b, h)),
                  pl.BlockSpec((nh * V_DIM, seq), lambda b, h, i: (h, b))],
        out_specs=pl.BlockSpec((tq, nh * V_DIM), lambda b, h, i: (b * nq + i, h)),
        out_shape=jax.ShapeDtypeStruct((bsz * seq, heads * V_DIM), F32),
        scratch_shapes=[pltpu.VMEM((nh, tq), F32), pltpu.VMEM((nh, tq), F32),
                        pltpu.VMEM((nh * V_DIM, tq), F32)],
        compiler_params=_cparams("parallel", "parallel", "arbitrary"),
        name="attn_prompt",
    )(q_cat, k_cat, v_t)


def _head_mm_kernel(x_ref, w_ref, o_ref, *, nt):
    x = x_ref[...].astype(BF16)
    o_ref[...] = (_dot_nt(x, w_ref[...]) if nt else _dot(x, w_ref[...])).astype(o_ref.dtype)


def _q_latent(q_cat, w_uk, row0, rows, heads):
    off = row0 // rows
    return pl.pallas_call(
        functools.partial(_head_mm_kernel, nt=True),
        grid=(heads,),
        in_specs=[pl.BlockSpec((rows, NOPE_DIM), lambda h: (off, 2 * h)),
                  pl.BlockSpec((KV_LORA, NOPE_DIM), lambda h: (0, h))],
        out_specs=pl.BlockSpec((None, rows, KV_LORA), lambda h: (h, 0, 0)),
        out_shape=jax.ShapeDtypeStruct((heads, rows, KV_LORA), BF16),
        compiler_params=_cparams("parallel"),
        name="q_latent",
    )(q_cat, w_uk)


def _o_from_latent(o_lat, w_uv, heads):
    rows = o_lat.shape[1]
    return pl.pallas_call(
        functools.partial(_head_mm_kernel, nt=False),
        grid=(heads,),
        in_specs=[pl.BlockSpec((None, rows, KV_LORA), lambda h: (h, 0, 0)),
                  pl.BlockSpec((KV_LORA, V_DIM), lambda h: (0, h))],
        out_specs=pl.BlockSpec((rows, V_DIM), lambda h: (0, h)),
        out_shape=jax.ShapeDtypeStruct((rows, heads * V_DIM), F32),
        compiler_params=_cparams("parallel"),
        name="o_from_latent",
    )(o_lat, w_uv)


def _lane_chunks(s):
    w = s.shape[-1]
    if w % LANES or w == LANES:
        return [s]
    return [s[:, c * LANES:(c + 1) * LANES] for c in range(w // LANES)]


def _row_max(s):
    parts = _lane_chunks(s)
    return jnp.max(functools.reduce(jnp.maximum, parts), axis=-1, keepdims=True)


def _row_sum(s):
    parts = _lane_chunks(s)
    return jnp.sum(functools.reduce(jnp.add, parts), axis=-1, keepdims=True)


def _attn_sample_kernel(pt_ref, ql_ref, qp_ref, lat_hbm, pe_hbm, nl_ref, np_ref, o_ref,
                        lat_buf, pe_buf, lat_sem, pe_sem, k_ref, pe_ref, m_ref, l_ref, acc_ref,
                        *, layer, npg, ts, page):
    s_idx = pl.program_id(1)
    n_groups = pl.num_programs(1)
    lin = pl.program_id(0) * n_groups + s_idx
    slot = lin % 2

    def page_copies(group, slot_, j):
        pid = pt_ref[group * npg + j]
        rows = pl.ds(j * page, page)
        return (pltpu.make_async_copy(lat_hbm.at[layer, pid], lat_buf.at[slot_, rows, :], lat_sem.at[slot_]),
                pltpu.make_async_copy(pe_hbm.at[layer, pid], pe_buf.at[slot_, j], pe_sem.at[slot_]))

    def start_group(group, slot_):
        for j in range(npg):
            for cp in page_copies(group, slot_, j):
                cp.start()

    @pl.when(lin == 0)
    def _():
        start_group(lin, slot)

    @pl.when(lin + 1 < pl.num_programs(0) * n_groups)
    def _():
        start_group(lin + 1, 1 - slot)

    for j in range(npg):
        for cp in page_copies(lin, slot, j):
            cp.wait()

    @pl.when(s_idx == 0)
    def _():
        m_ref[...] = jnp.full(m_ref.shape, -jnp.inf, F32)
        l_ref[...] = jnp.zeros(l_ref.shape, F32)
        acc_ref[...] = jnp.zeros(acc_ref.shape, F32)

    ql = ql_ref[...]
    qp = qp_ref[...]

    def update(s, values):
        m_old = m_ref[...]
        m_new = jnp.maximum(m_old, _row_max(s))
        alpha = jnp.exp(m_old - m_new)
        p = jnp.exp(s - m_new)
        l_ref[...] = alpha * l_ref[...] + _row_sum(p)
        acc_ref[...] = alpha * acc_ref[...] + _dot(p.astype(BF16), values)
        m_ref[...] = m_new

    k_ref[...] = lat_buf[slot].astype(BF16)
    for j in range(npg):
        pe_ref[:, j * page:(j + 1) * page] = pe_buf[slot, j].astype(BF16)
    keys = k_ref[...]
    update(_dot_nt(ql, keys) + _dot(qp, pe_ref[...]), keys)

    @pl.when(s_idx == pl.num_programs(1) - 1)
    def _():
        kn = nl_ref[...].astype(BF16)
        s = _dot_nt(ql, kn) + _dot_nt(qp, np_ref[...].astype(BF16))
        t_q = lax.broadcasted_iota(jnp.int32, s.shape, 0) % ts
        t_k = lax.broadcasted_iota(jnp.int32, s.shape, 1)
        update(jnp.where(t_k <= t_q, s, -jnp.inf), kn)
        o_ref[...] = (acc_ref[...] / l_ref[...]).astype(o_ref.dtype)


def _attn_sample(page_table, q_lat, q_pe, cache_lat, cache_pe_t, layer, new_lat, new_pe, ts):
    bsz, qrows, _ = q_lat.shape
    n_pages = page_table.shape[1]
    page = cache_lat.shape[2]
    npg = math.gcd(PAGES_PER_STEP, n_pages)
    steps = n_pages // npg
    pt_flat = page_table.reshape(-1)

    per_b = lambda b, s, pt: (b, 0, 0)
    keys = npg * page
    grid_spec = pltpu.PrefetchScalarGridSpec(
        num_scalar_prefetch=1,
        grid=(bsz, steps),
        in_specs=[pl.BlockSpec((None, qrows, KV_LORA), per_b),
                  pl.BlockSpec((None, qrows, ROPE_DIM), per_b),
                  pl.BlockSpec(memory_space=pl.ANY), pl.BlockSpec(memory_space=pl.ANY),
                  pl.BlockSpec((None, 2 * ts, KV_LORA), per_b),
                  pl.BlockSpec((None, 2 * ts, ROPE_DIM), per_b)],
        out_specs=pl.BlockSpec((None, qrows, KV_LORA), per_b),
        scratch_shapes=[pltpu.VMEM((2, keys, KV_LORA), F32), pltpu.VMEM((2, npg, ROPE_DIM, page), F32),
                        pltpu.SemaphoreType.DMA((2,)), pltpu.SemaphoreType.DMA((2,)),
                        pltpu.VMEM((keys, KV_LORA), BF16), pltpu.VMEM((ROPE_DIM, keys), BF16),
                        pltpu.VMEM((qrows, 1), F32), pltpu.VMEM((qrows, 1), F32),
                        pltpu.VMEM((qrows, KV_LORA), F32)],
    )
    return pl.pallas_call(
        functools.partial(_attn_sample_kernel, layer=layer, npg=npg, ts=ts, page=page),
        grid_spec=grid_spec,
        out_shape=jax.ShapeDtypeStruct((bsz, qrows, KV_LORA), BF16),
        compiler_params=_cparams("arbitrary", "arbitrary"),
        name="attn_sample",
    )(pt_flat, q_lat, q_pe, cache_lat, cache_pe_t, new_lat, new_pe)


def _two_part_specs(tm, width, nt_p):
    return (pl.BlockSpec((tm, width), lambda i: (jnp.minimum(i, nt_p - 1), 0)),
            pl.BlockSpec((tm, width), lambda i: (jnp.maximum(i - nt_p, 0), 0)))


def _pick_part(nt_p, p_ref, s_ref):
    return jnp.where(pl.program_id(0) < nt_p, p_ref[...], s_ref[...])


def _hyb_out_kernel(ap_ref, as_ref, op_ref, os_ref, w_ref, x_ref, g_ref, b_ref, y_ref, *, alpha, nt_p):
    a = _pick_part(nt_p, ap_ref, as_ref).astype(BF16)
    o = _pick_part(nt_p, op_ref, os_ref).astype(BF16)
    ka = a.shape[1]
    mix = _dot(a, w_ref[0:ka, :]) + _dot(o, w_ref[ka:, :])
    y_ref[...] = _layer_norm_rows(alpha * x_ref[...] + mix, g_ref[...], b_ref[...])


def _hyb_out(a_p, a_s, o_p, o_s, w_out, x, g, b, alpha):
    m, d = x.shape
    n_p, n_s = a_p.shape[0], a_s.shape[0]
    tm = _tile(math.gcd(n_p, n_s), 256)
    nt_p = n_p // tm
    row = lambda i: (i, 0)
    fixed = lambda i: (0, 0)
    return pl.pallas_call(
        functools.partial(_hyb_out_kernel, alpha=alpha, nt_p=nt_p),
        grid=(m // tm,),
        in_specs=[*_two_part_specs(tm, a_p.shape[1], nt_p), *_two_part_specs(tm, o_p.shape[1], nt_p),
                  pl.BlockSpec(w_out.shape, fixed), pl.BlockSpec((tm, d), row),
                  pl.BlockSpec((1, d), fixed), pl.BlockSpec((1, d), fixed)],
        out_specs=pl.BlockSpec((tm, d), row),
        out_shape=jax.ShapeDtypeStruct((m, d), F32),
        compiler_params=_cparams("parallel"),
        name="hyb_out",
    )(a_p, a_s, o_p, o_s, w_out, x, g, b)


def _rec_out_kernel(op_ref, os_ref, gate_ref, ng_ref, w_ref, x_ref, g_ref, b_ref, y_ref, *, alpha, nt_p):
    gate = gate_ref[...]
    o = _pick_part(nt_p, op_ref, os_ref)
    on = _rms_norm_rows(o, ng_ref[...]) * (gate * jax.nn.sigmoid(gate))
    mix = _dot(on.astype(BF16), w_ref[...])
    y_ref[...] = _layer_norm_rows(alpha * x_ref[...] + mix, g_ref[...], b_ref[...])


def _rec_out(o_p, o_s, h, gate_col_block, norm_g, w_out, x, g, b, alpha):
    m, d = x.shape
    vw = o_p.shape[1]
    n_p, n_s = o_p.shape[0], o_s.shape[0]
    tm = _tile(math.gcd(n_p, n_s), 256)
    nt_p = n_p // tm
    row = lambda i: (i, 0)
    fixed = lambda i: (0, 0)
    return pl.pallas_call(
        functools.partial(_rec_out_kernel, alpha=alpha, nt_p=nt_p),
        grid=(m // tm,),
        in_specs=[*_two_part_specs(tm, vw, nt_p),
                  pl.BlockSpec((tm, vw), lambda i: (i, gate_col_block)),
                  pl.BlockSpec((1, vw), fixed), pl.BlockSpec(w_out.shape, fixed),
                  pl.BlockSpec((tm, d), row), pl.BlockSpec((1, d), fixed),
                  pl.BlockSpec((1, d), fixed)],
        out_specs=pl.BlockSpec((tm, d), row),
        out_shape=jax.ShapeDtypeStruct((m, d), F32),
        compiler_params=_cparams("parallel"),
        name="rec_out",
    )(o_p, o_s, h, norm_g, w_out, x, g, b)


def _gla_levels(chunk):
    m = chunk // 2
    out = []
    while m >= 1:
        out.append(m)
        m //= 2
    return tuple(out)


def _gla_prefix_matrix(chunk):
    t = np.arange(chunk)
    return (t[None, :] <= t[:, None]).astype(np.float32)


def _gla_block_ref(cum, m):
    c, w = cum.shape
    if 2 * m >= SUBLANES:
        parts = [jnp.broadcast_to(cum[b + m:b + m + 1, :], (2 * m, w)) for b in range(0, c, 2 * m)]
        return parts[0] if len(parts) == 1 else jnp.concatenate(parts, axis=0)
    row = lax.broadcasted_iota(jnp.int32, cum.shape, 0)
    if m == 1:
        return jnp.where(row % 2 == 1, cum, pltpu.roll(cum, c - 1, 0))
    assert m == 2
    sub = row % 4
    return jnp.where(sub == 2, cum,
                     jnp.where(sub == 3, pltpu.roll(cum, 1, 0),
                               jnp.where(sub == 1, pltpu.roll(cum, c - 1, 0), pltpu.roll(cum, c - 2, 0))))


def _gla_gates(q_raw, f_raw, lb):
    q = q_raw * jax.nn.sigmoid(q_raw) * (REC_DK ** -0.5)
    forget = lb + (1.0 - lb) * jax.nn.sigmoid(f_raw)
    return q, 1.0 - forget, jnp.log(forget)


def _split3(x):
    hi = x.astype(BF16)
    r1 = x - hi.astype(F32)
    mid = r1.astype(BF16)
    lo = (r1 - mid.astype(F32)).astype(BF16)
    return hi, mid, lo


def _gla_pair_masks(chunk):
    t = np.arange(chunk)[:, None]
    s = np.arange(chunk)[None, :]
    masks = [t == s]
    for m in _gla_levels(chunk):
        masks.append((t // (2 * m) == s // (2 * m)) & ((t // m) % 2 == 1) & ((s // m) % 2 == 0))
    return np.stack(masks).astype(np.float32)


def _gla_prompt_kernel(q_ref, f_ref, v_ref, lb_ref, pm_ref, mask_ref, o_ref, st_ref, s_ref, *, tt, gh):
    c = GLA_CHUNK
    dk = REC_DK
    i = pl.program_id(2)

    @pl.when(i == 0)
    def _():
        s_ref[...] = jnp.zeros(s_ref.shape, F32)

    lb = lb_ref[...]
    pm = pm_ref[...]
    levels = _gla_levels(c)

    def level_operand(q, k, cum, m):
        if m >= SUBLANES:
            parts = []
            for b in range(0, c, 2 * m):
                ref = cum[b + m:b + m + 1, :]
                parts.append(k[b:b + m] * jnp.exp(ref - cum[b:b + m]))
                parts.append(q[b + m:b + 2 * m] * jnp.exp(cum[b + m:b + 2 * m] - ref))
            return jnp.concatenate(parts, axis=0)
        row = lax.broadcasted_iota(jnp.int32, q.shape, 0)
        return (jnp.where((row // m) % 2 == 1, q, k)
                * jnp.exp(-jnp.abs(cum - _gla_block_ref(cum, m))))

    def body(ci, carry):
        r0 = pl.multiple_of(ci * c, c)
        q_all, k_all, g_all = _gla_gates(q_ref[pl.ds(r0, c), :], f_ref[pl.ds(r0, c), :], lb)
        v_all = v_ref[pl.ds(r0, c), :].astype(BF16)
        hi, mid, lo = _split3(g_all)
        cum_all = _dot(pm, hi) + _dot(pm, mid) + _dot(pm, lo)
        last_all = cum_all[c - 1:c, :]
        z_all = [level_operand(q_all, k_all, cum_all, m).astype(BF16) for m in levels]
        qe_all = (q_all * jnp.exp(cum_all)).astype(BF16)
        kd_all = (k_all * jnp.exp(last_all - cum_all)).astype(BF16)
        dec_all = jnp.exp(last_all)
        qb_all, kb_all = q_all.astype(BF16), k_all.astype(BF16)
        outs, states = [], []
        for hd in range(gh):
            sl = slice(hd * dk, (hd + 1) * dk)
            att = mask_ref[0] * _dot_nt(qb_all[:, sl], kb_all[:, sl])
            for li, z in enumerate(z_all):
                att = att + mask_ref[li + 1] * _dot_nt(z[:, sl], z[:, sl])
            s_t = s_ref[hd * dk:(hd + 1) * dk, :]
            inter = _dot_nt(qe_all[:, sl], s_t.astype(BF16))
            outs.append(inter + _dot(att.astype(BF16), v_all[:, sl]))
            states.append(s_t * dec_all[:, sl] + _dot_tn(v_all[:, sl], kd_all[:, sl]))
        o_ref[pl.ds(r0, c), :] = jnp.concatenate(outs, axis=1)
        s_ref[...] = jnp.concatenate(states, axis=0)
        return carry

    lax.fori_loop(0, tt // c, body, 0)

    @pl.when(i == pl.num_programs(2) - 1)
    def _():
        for hd in range(gh):
            st_ref[hd] = s_ref[hd * dk:(hd + 1) * dk, :]


def _gla_prompt(h, lb, bsz, seq, heads):
    tt = _tile(seq, 512)
    assert tt % GLA_CHUNK == 0
    nt = seq // tt
    gh = math.gcd(GLA_HEADS_PER_STEP, heads)
    ng = heads // gh
    wide = gh * REC_DK
    pm = jnp.asarray(_gla_prefix_matrix(GLA_CHUNK), BF16)
    masks = jnp.asarray(_gla_pair_masks(GLA_CHUNK), F32)

    def col(base):
        return pl.BlockSpec((tt, wide), lambda b, hg, i: (b * nt + i, base + hg))

    return pl.pallas_call(
        functools.partial(_gla_prompt_kernel, tt=tt, gh=gh),
        grid=(bsz, ng, nt),
        in_specs=[col(0), col(ng), col(2 * ng),
                  pl.BlockSpec((1, wide), lambda b, hg, i: (0, hg)),
                  pl.BlockSpec(pm.shape, lambda b, hg, i: (0, 0)),
                  pl.BlockSpec(masks.shape, lambda b, hg, i: (0, 0, 0))],
        out_specs=[pl.BlockSpec((tt, wide), lambda b, hg, i: (b * nt + i, hg)),
                   pl.BlockSpec((None, gh, REC_DK, REC_DK), lambda b, hg, i: (b, hg, 0, 0))],
        out_shape=[jax.ShapeDtypeStruct((bsz * seq, heads * REC_DK), F32),
                   jax.ShapeDtypeStruct((bsz, heads, REC_DK, REC_DK), F32)],
        scratch_shapes=[pltpu.VMEM((wide, REC_DK), F32)],
        compiler_params=_cparams("parallel", "parallel", "arbitrary"),
        name="gla_prompt",
    )(h, h, h, lb, pm, masks)


def _gla_sample_kernel(q_ref, f_ref, v_ref, lb_ref, hs_ref, he_ref, st_ref, o_ref, sto_ref, *, ts, heads):
    dk = REC_DK
    q, k, g = _gla_gates(q_ref[...], f_ref[...], lb_ref[...])
    v = v_ref[...]
    t = lax.broadcasted_iota(jnp.int32, q.shape, 0)
    cum = g
    sh = 1
    while sh < ts:
        cum = cum + jnp.where(t >= sh, pltpu.roll(cum, sh, 0), 0.0)
        sh *= 2
    last = cum[ts - 1:ts, :]
    xs = []
    for s in range(ts):
        dec = jnp.exp(jnp.where(t >= s, cum - cum[s:s + 1, :], -jnp.inf))
        xs.append(q * k[s:s + 1, :] * dec)
    x = jnp.concatenate(xs, axis=0)
    x_hi = x.astype(BF16)
    x_lo = (x - x_hi.astype(F32)).astype(BF16)
    att = _dot(x_hi, hs_ref[...]) + _dot(x_lo, hs_ref[...])
    att_e = _dot(att.astype(BF16), he_ref[...])
    intra = jnp.zeros(q.shape, F32)
    for s in range(ts):
        intra = intra + att_e[s * ts:(s + 1) * ts, :] * v[s:s + 1, :]
    qe = (q * jnp.exp(cum)).astype(BF16)
    kd = (k * jnp.exp(last - cum)).astype(BF16)
    dec_last = jnp.exp(last)
    eye = (lax.broadcasted_iota(jnp.int32, (dk, dk), 0)
           == lax.broadcasted_iota(jnp.int32, (dk, dk), 1))
    vb = v.astype(BF16)
    for hd in range(heads):
        sl = slice(hd * dk, (hd + 1) * dk)
        s_h = st_ref[hd]
        o_ref[:, sl] = _dot(qe[:, sl], s_h.astype(BF16)) + intra[:, sl]
        d_col = jnp.sum(jnp.where(eye, dec_last[:, sl], 0.0), axis=1, keepdims=True)
        sto_ref[hd] = s_h * d_col + _dot_tn(kd[:, sl], vb[:, sl])


def _gla_sample(h, lb, state, j, row0, bsz, ts, heads):
    width = heads * REC_DK
    off = row0 // ts
    head_of = np.arange(width) // REC_DK
    hsum = (head_of[:, None] == np.arange(LANES)[None, :]).astype(np.float32)
    hs = jnp.asarray(hsum, BF16)
    he = jnp.asarray(hsum.T, BF16)

    def col(cb):
        return pl.BlockSpec((ts, width), lambda b: (off + b, cb))

    fixed = lambda b: (0, 0)
    st_in = pl.BlockSpec((None, None, heads, REC_DK, REC_DK), lambda b: (j, b, 0, 0, 0))
    st_out = pl.BlockSpec((None, heads, REC_DK, REC_DK), lambda b: (b, 0, 0, 0))
    return pl.pallas_call(
        functools.partial(_gla_sample_kernel, ts=ts, heads=heads),
        grid=(bsz,),
        in_specs=[col(0), col(1), col(2), pl.BlockSpec((1, width), fixed),
                  pl.BlockSpec(hs.shape, fixed), pl.BlockSpec(he.shape, fixed), st_in],
        out_specs=[pl.BlockSpec((ts, width), lambda b: (b, 0)), st_out],
        out_shape=[jax.ShapeDtypeStruct((bsz * ts, width), F32),
                   jax.ShapeDtypeStruct(state.shape[1:], F32)],
        compiler_params=_cparams("parallel"),
        name="gla_sample",
    )(h, h, h, lb, hs, he, state)


def _router_kernel(x_ref, wh_ref, wl_ref, b_ref, o_ref):
    x = x_ref[...]
    x_hi = x.astype(BF16)
    x_lo = (x - x_hi.astype(F32)).astype(BF16)
    wh = wh_ref[...]
    logits = _dot(x_hi, wh) + _dot(x_lo, wh) + _dot(x_hi, wl_ref[...]) + b_ref[...]
    lane = lax.broadcasted_iota(jnp.int32, logits.shape, 1).astype(F32)
    big = float(LANES)
    neg = -jnp.inf
    gl = jnp.where(lane < N_GROUPS, logits, neg)
    g_max = jnp.max(gl, axis=-1, keepdims=True)
    g_sel = jnp.min(jnp.where(gl == g_max, lane, big), axis=-1, keepdims=True)
    g_gate = 1.0 / jnp.sum(jnp.exp(gl - g_max), axis=-1, keepdims=True)
    lo = N_GROUPS + g_sel * EXPERTS_PER_GROUP
    el = jnp.where((lane >= lo) & (lane < lo + EXPERTS_PER_GROUP), logits, neg)
    e_max = jnp.max(el, axis=-1, keepdims=True)
    i1 = jnp.min(jnp.where(el == e_max, lane, big), axis=-1, keepdims=True)
    denom = jnp.sum(jnp.exp(el - e_max), axis=-1, keepdims=True)
    el2 = jnp.where(lane == i1, neg, el)
    m2 = jnp.max(el2, axis=-1, keepdims=True)
    i2 = jnp.min(jnp.where(el2 == m2, lane, big), axis=-1, keepdims=True)
    p1 = 1.0 / denom
    p2 = jnp.exp(m2 - e_max) / denom
    w1 = g_gate * p1 / (p1 + p2)
    w2 = g_gate * p2 / (p1 + p2)
    e1 = i1 - N_GROUPS
    e2 = i2 - N_GROUPS
    o_ref[...] = jnp.where(lane == 0, e1, jnp.where(lane == 1, e2,
                           jnp.where(lane == 2, w1, jnp.where(lane == 3, w2, 0.0))))


def _router(x, w_hi, w_lo, bias):
    m, d = x.shape
    tm = _tile(m, 512)
    fixed = lambda i: (0, 0)
    return pl.pallas_call(
        _router_kernel,
        grid=(m // tm,),
        in_specs=[pl.BlockSpec((tm, d), lambda i: (i, 0)), pl.BlockSpec((d, LANES), fixed),
                  pl.BlockSpec((d, LANES), fixed), pl.BlockSpec((1, LANES), fixed)],
        out_specs=pl.BlockSpec((tm, LANES), lambda i: (i, 0)),
        out_shape=jax.ShapeDtypeStruct((m, LANES), F32),
        compiler_params=_cparams("parallel"),
        name="router",
    )(x, w_hi, w_lo, bias)


def _expert_weights(t, te_ref, first_ref, slot_ref, nxt_ref, layer, hbm_refs, buf_refs, sem_refs, bf_refs):
    def copies(expert, slot):
        return [pltpu.make_async_copy(hbm.at[layer, expert], buf.at[slot], sem.at[slot])
                for hbm, buf, sem in zip(hbm_refs, buf_refs, sem_refs)]

    @pl.when(t == 0)
    def _():
        for cp in copies(te_ref[0], slot_ref[0]):
            cp.start()

    @pl.when(first_ref[t] == 1)
    def _():
        slot = slot_ref[t]

        @pl.when(nxt_ref[t] >= 0)
        def _():
            for cp in copies(nxt_ref[t], 1 - slot):
                cp.start()

        for cp in copies(te_ref[t], slot):
            cp.wait()
        for buf, bf in zip(buf_refs, bf_refs):
            bf[...] = buf[slot].astype(BF16)


def _moe_up_kernel(te_ref, first_ref, slot_ref, nxt_ref, nu_ref, x_ref, wg_hbm, wu_hbm, h_ref,
                   wg_buf, wu_buf, wg_sem, wu_sem, wgb_ref, wub_ref, *, layer):
    t = pl.program_id(0)

    @pl.when(t < nu_ref[0])
    def _():
        _expert_weights(t, te_ref, first_ref, slot_ref, nxt_ref, layer, (wg_hbm, wu_hbm),
                        (wg_buf, wu_buf), (wg_sem, wu_sem), (wgb_ref, wub_ref))
        x = x_ref[...].astype(BF16)
        gate = _dot(x, wgb_ref[...])
        up = _dot(x, wub_ref[...])
        h_ref[...] = (gate * jax.nn.sigmoid(gate) * up).astype(h_ref.dtype)

    @pl.when(t >= nu_ref[0])
    def _():
        h_ref[...] = jnp.zeros(h_ref.shape, h_ref.dtype)


def _moe_down_kernel(te_ref, first_ref, slot_ref, nxt_ref, nu_ref, h_ref, wd_hbm, y_ref,
                     wd_buf, wd_sem, wdb_ref, *, layer):
    t = pl.program_id(0)

    @pl.when(t < nu_ref[0])
    def _():
        _expert_weights(t, te_ref, first_ref, slot_ref, nxt_ref, layer, (wd_hbm,), (wd_buf,),
                        (wd_sem,), (wdb_ref,))
        y_ref[...] = _dot(h_ref[...], wdb_ref[...])

    @pl.when(t >= nu_ref[0])
    def _():
        y_ref[...] = jnp.zeros(y_ref.shape, y_ref.dtype)


def _moe_experts(xs, plan, w_gate, w_up, w_down, layer):
    cap, d = xs.shape
    de = w_gate.shape[-1]
    n_tiles = cap // MOE_TILE
    n_plan = len(plan)

    def live(t, *refs):
        return (jnp.minimum(t, refs[n_plan - 1][0] - 1), 0)

    own = lambda t, *refs: (t, 0)
    hbm = pl.BlockSpec(memory_space=pl.ANY)
    two = lambda *shape: pltpu.VMEM((2,) + shape, F32)
    sem = pltpu.SemaphoreType.DMA((2,))

    hid = pl.pallas_call(
        functools.partial(_moe_up_kernel, layer=layer),
        grid_spec=pltpu.PrefetchScalarGridSpec(
            num_scalar_prefetch=n_plan,
            grid=(n_tiles,),
            in_specs=[pl.BlockSpec((MOE_TILE, d), live), hbm, hbm],
            out_specs=pl.BlockSpec((MOE_TILE, de), own),
            scratch_shapes=[two(d, de), two(d, de), sem, sem,
                            pltpu.VMEM((d, de), BF16), pltpu.VMEM((d, de), BF16)]),
        out_shape=jax.ShapeDtypeStruct((cap, de), BF16),
        compiler_params=_cparams("arbitrary"),
        name="moe_up",
    )(*plan, xs, w_gate, w_up)

    return pl.pallas_call(
        functools.partial(_moe_down_kernel, layer=layer),
        grid_spec=pltpu.PrefetchScalarGridSpec(
            num_scalar_prefetch=n_plan,
            grid=(n_tiles,),
            in_specs=[pl.BlockSpec((MOE_TILE, de), live), hbm],
            out_specs=pl.BlockSpec((MOE_TILE, d), own),
            scratch_shapes=[two(de, d), sem, pltpu.VMEM((de, d), BF16)]),
        out_shape=jax.ShapeDtypeStruct((cap, d), F32),
        compiler_params=_cparams("arbitrary"),
        name="moe_down",
    )(*plan, hid, w_down)


def _moe_ln_kernel(x_ref, y0_ref, y1_ref, r_ref, g_ref, b_ref, o_ref, *, alpha):
    r = r_ref[...]
    moe = r[:, TOP_K:TOP_K + 1] * y0_ref[...] + r[:, TOP_K + 1:TOP_K + 2] * y1_ref[...]
    o_ref[...] = _layer_norm_rows(alpha * x_ref[...] + moe, g_ref[...], b_ref[...])


def _moe_ln(x, y2, routed, g, b, alpha):
    m, d = x.shape
    tm = _tile(m, 512)
    nt = m // tm
    row = lambda i: (i, 0)
    fixed = lambda i: (0, 0)
    return pl.pallas_call(
        functools.partial(_moe_ln_kernel, alpha=alpha),
        grid=(nt,),
        in_specs=[pl.BlockSpec((tm, d), row), pl.BlockSpec((tm, d), row),
                  pl.BlockSpec((tm, d), lambda i: (nt + i, 0)), pl.BlockSpec((tm, LANES), row),
                  pl.BlockSpec((1, d), fixed), pl.BlockSpec((1, d), fixed)],
        out_specs=pl.BlockSpec((tm, d), row),
        out_shape=jax.ShapeDtypeStruct((m, d), F32),
        compiler_params=_cparams("parallel"),
        name="moe_ln",
    )(x, y2, y2, routed, g, b)


def _moe_layer(x, layer, w_rg, b_rg, w_re, b_re, w_gate, w_up, w_down, ln_g, ln_b, alpha):
    m, d = x.shape
    wr = jnp.concatenate([w_rg, w_re], axis=1)
    wr = jnp.pad(wr, ((0, 0), (0, LANES - wr.shape[1])))
    wr_hi = wr.astype(BF16)
    wr_lo = (wr - wr_hi.astype(F32)).astype(BF16)
    br = jnp.pad(jnp.concatenate([b_rg, b_re]), (0, LANES - N_GROUPS - N_EXPERTS))[None, :]
    routed = _router(x, wr_hi, wr_lo, br)
    ids = routed[:, 0:TOP_K].astype(jnp.int32)

    n_assign = m * TOP_K
    n_tiles = -(-n_assign // MOE_TILE) + N_EXPERTS
    cap = n_tiles * MOE_TILE
    flat_e = ids.reshape(-1)
    onehot = (flat_e[:, None] == jnp.arange(N_EXPERTS, dtype=jnp.int32)[None, :]).astype(jnp.int32)
    csum = jnp.cumsum(onehot, axis=0)
    counts = csum[-1]
    rank = jnp.take_along_axis(csum, flat_e[:, None], axis=1)[:, 0] - 1
    tiles_per = (counts + MOE_TILE - 1) // MOE_TILE
    tile_ends = jnp.cumsum(tiles_per)
    pad_starts = (tile_ends - tiles_per) * MOE_TILE
    dest = pad_starts[flat_e] + rank
    n_used = tile_ends[-1:].astype(jnp.int32)
    tile_idx = jnp.arange(n_tiles, dtype=jnp.int32)
    tile_e = jnp.sum((tile_ends[None, :] <= tile_idx[:, None]).astype(jnp.int32), axis=1)
    last_e = jnp.max(jnp.where(counts > 0, jnp.arange(N_EXPERTS), 0))
    tile_e = jnp.minimum(tile_e, last_e).astype(jnp.int32)
    prev_e = jnp.concatenate([jnp.full((1,), -1, jnp.int32), tile_e[:-1]])
    first = ((tile_e != prev_e) & (tile_idx < n_used[0])).astype(jnp.int32)
    slot = (jnp.cumsum(first) - 1) % 2
    next_tile = tile_ends[tile_e]
    nxt = jnp.where(next_tile < n_used[0], tile_e[jnp.minimum(next_tile, n_tiles - 1)], -1)
    plan = (tile_e, first, slot.astype(jnp.int32), nxt.astype(jnp.int32), n_used)
    tok = jnp.arange(n_assign, dtype=jnp.int32) // TOP_K
    slot_tok = (jnp.arange(cap, dtype=jnp.int32) % m).at[dest].set(tok)

    xs = x[slot_tok]
    y = _moe_experts(xs, plan, w_gate, w_up, w_down, layer)
    y2 = y[dest.reshape(m, TOP_K).T.reshape(-1)]
    return _moe_ln(x, y2, routed, ln_g[None, :], ln_b[None, :], alpha)


def _rope_tables(pos):
    half = ROPE_DIM // 2
    inv_freq = ROPE_THETA ** (-jnp.arange(half, dtype=F32) / half)
    ang = pos.astype(F32)[:, None] * inv_freq
    cos, sin = jnp.cos(ang), jnp.sin(ang)
    z = jnp.zeros_like(cos)
    pad = jnp.zeros((pos.shape[0], LANES - ROPE_DIM), F32)
    c = jnp.concatenate([cos, cos, pad], axis=1)
    sa = jnp.concatenate([-sin, z, pad], axis=1)
    sb = jnp.concatenate([z, sin, pad], axis=1)
    return c, sa, sb


def _hybrid_layer(x, dims, rope, conv_state, cache_lat, cache_pe, page_table, j,
                  w_in, conv_w, q_g, kv_g, w_q_up, w_uk, w_uv, w_out, ln_g, ln_b, alpha):
    bp, sp, bs, ts = dims
    n_p = bp * sp
    cw = conv_w.shape[1]
    heads = w_q_up.shape[1]
    n_main = 3 * cw + Q_LORA + KV_LORA

    w_pe = jnp.pad(w_in[j, :, n_main:], ((0, 0), (0, LANES - ROPE_DIM))).astype(BF16)
    h = _mm(x, w_in, layer=j, n=n_main, name="hyb_in")
    h_pe = _mm(x, w_pe, name="hyb_in_pe")

    a_p, conv_p = _conv_prompt(h, conv_w, bp, sp, cw)
    a_s, conv_s = _conv_sample(h, conv_state, conv_w, n_p, bs, ts, cw)

    w_q_cat = jnp.pad(w_q_up, ((0, 0), (0, 0), (0, HEAD_PAD - NOPE_DIM - ROPE_DIM)))
    w_q_cat = w_q_cat.reshape(Q_LORA, heads * HEAD_PAD).astype(BF16)
    q_cat = _q_proj(h, 3 * cw // Q_LORA, q_g[None, :], w_q_cat, *rope, heads)
    lat, kpe = _kv_norm(h, (3 * cw + Q_LORA) // KV_LORA, h_pe, kv_g[None, :], *rope)
    w_uk2 = w_uk.reshape(KV_LORA, heads * NOPE_DIM).astype(BF16)
    w_uv2 = w_uv.reshape(KV_LORA, heads * V_DIM).astype(BF16)
    w_uv_t = w_uv2.T

    k_cat, v_t = _kv_expand(lat, kpe, w_uk2, w_uv_t, n_p, heads)
    o_p = _attn_prompt(q_cat, k_cat, v_t, bp, sp, heads)

    n_s = bs * ts
    q_lat = _q_latent(q_cat, w_uk2, n_p, n_s, heads)
    q_lat = q_lat.reshape(heads, bs, ts, KV_LORA).transpose(1, 0, 2, 3).reshape(bs, heads * ts, KV_LORA)
    q_pe = q_cat[n_p:].reshape(bs, ts, heads, HEAD_PAD)[..., NOPE_DIM:NOPE_DIM + ROPE_DIM]
    q_pe = q_pe.transpose(0, 2, 1, 3).reshape(bs, heads * ts, ROPE_DIM)
    lat_s = lat[n_p:].reshape(bs, ts, KV_LORA)
    kpe_s = kpe[n_p:, :ROPE_DIM].reshape(bs, ts, ROPE_DIM)
    new_lat = jnp.pad(lat_s, ((0, 0), (0, ts), (0, 0)))
    new_pe = jnp.pad(kpe_s, ((0, 0), (0, ts), (0, 0)))
    o_lat = _attn_sample(page_table, q_lat, q_pe, cache_lat, cache_pe, j, new_lat, new_pe, ts)
    o_lat = o_lat.reshape(bs, heads, ts, KV_LORA).transpose(1, 0, 2, 3).reshape(heads, n_s, KV_LORA)
    o_s = _o_from_latent(o_lat, w_uv2, heads)

    y = _hyb_out(a_p, a_s, o_p, o_s, w_out.astype(BF16), x, ln_g[None, :], ln_b[None, :], alpha)
    outs = (lat[:n_p].reshape(bp, sp, KV_LORA), kpe[:n_p, :ROPE_DIM].reshape(bp, sp, ROPE_DIM),
            lat_s, kpe_s, conv_p, conv_s)
    return y, outs


def _rec_layer(x, dims, state, j, lb, w_in, norm_g, w_out, ln_g, ln_b, alpha):
    bp, sp, bs, ts = dims
    n_p = bp * sp
    width = w_in.shape[-1] // 4
    heads = width // REC_DK
    h = _mm(x, w_in, layer=j, name="rec_in")
    lb2 = lb[None, :]
    o_p, st_p = _gla_prompt(h, lb2, bp, sp, heads)
    o_s, st_s = _gla_sample(h, lb2, state, j, n_p, bs, ts, heads)
    y = _rec_out(o_p, o_s, h, 3, norm_g[None, :], w_out.astype(BF16), x, ln_g[None, :], ln_b[None, :], alpha)
    return y, (jnp.swapaxes(st_p, -1, -2), st_s)


def kernel(x_prompt, x_sample, cache_kv_latent, cache_k_rope, state_conv, state_hgrn, page_table,
           w_in_hyb, conv_w, q_norm_g, kv_norm_g, w_q_up, w_uk, w_uv, w_out_hyb,
           w_in_rec, lb_logits, rec_norm_g, w_out_rec,
           ln1_g, ln1_b, ln2_g, ln2_b,
           w_router_group, b_router_group, w_router_expert, b_router_expert,
           w_gate, w_up, w_down):
    bp, sp, d = x_prompt.shape
    bs, ts, _ = x_sample.shape
    depth = ln1_g.shape[0]
    past_len = page_table.shape[1] * cache_kv_latent.shape[2]
    dims = (bp, sp, bs, ts)
    n_p = bp * sp
    alpha = (2 * depth) ** 0.25

    lower = jnp.cumsum(jax.nn.softmax(lb_logits.astype(F32), axis=0), axis=0)
    lower = lower - lower[0]

    pos = jnp.concatenate([jnp.tile(jnp.arange(sp), bp), jnp.tile(past_len + jnp.arange(ts), bs)])
    rope = _rope_tables(pos)

    x = jnp.concatenate([x_prompt.reshape(n_p, d), x_sample.reshape(bs * ts, d)], axis=0)
    cache_pe_t = jnp.swapaxes(cache_k_rope, 2, 3)
    hyb_outs, rec_outs = [], []
    for layer in range(depth):
        j = layer // 2
        if layer % 2 == 0:
            x, outs = _hybrid_layer(x, dims, rope, state_conv[j], cache_kv_latent, cache_pe_t, page_table, j,
                                    w_in_hyb, conv_w[j], q_norm_g[j], kv_norm_g[j], w_q_up[j], w_uk[j],
                                    w_uv[j], w_out_hyb[j], ln1_g[layer], ln1_b[layer], alpha)
            hyb_outs.append(outs)
        else:
            x, outs = _rec_layer(x, dims, state_hgrn, j, lower[layer], w_in_rec, rec_norm_g[j],
                                 w_out_rec[j], ln1_g[layer], ln1_b[layer], alpha)
            rec_outs.append(outs)
        x = _moe_layer(x, layer, w_router_group[layer], b_router_group[layer], w_router_expert[layer],
                       b_router_expert[layer], w_gate, w_up, w_down, ln2_g[layer], ln2_b[layer], alpha)

    stack = lambda parts, i: jnp.stack([p[i] for p in parts])
    return (x[:n_p].reshape(bp, sp, d), x[n_p:].reshape(bs, ts, d),
            stack(hyb_outs, 0), stack(hyb_outs, 1), stack(hyb_outs, 2), stack(hyb_outs, 3),
            stack(hyb_outs, 4), stack(hyb_outs, 5), stack(rec_outs, 0), stack(rec_outs, 1))
```

```python
import functools
import math

import numpy as np
import jax
import jax.numpy as jnp
from jax import lax
from jax.experimental import pallas as pl
from jax.experimental.pallas import tpu as pltpu

F32 = jnp.float32
BF16 = jnp.bfloat16

CONV_K = 3
Q_LORA = 512
KV_LORA = 512
NOPE_DIM = 128
ROPE_DIM = 64
V_DIM = 128
REC_DK = 128
N_GROUPS = 4
EXPERTS_PER_GROUP = 8
N_EXPERTS = N_GROUPS * EXPERTS_PER_GROUP
TOP_K = 2
ROPE_THETA = 10000.0
LN_EPS = 1e-5
RMS_EPS = 1e-6
MLA_SCALE = (NOPE_DIM + ROPE_DIM) ** -0.5

LANES = 128
SUBLANES = 8
VMEM_LIMIT_BYTES = 52 * 1024 * 1024

HEAD_PAD = 2 * LANES
GLA_CHUNK = 64
GLA_HEADS_PER_STEP = 16
MOE_TILE = 256
ATTN_TQ = 512
ATTN_HEADS_PER_STEP = 2
PAGES_PER_STEP = 32


def _cparams(*sem):
    return pltpu.CompilerParams(dimension_semantics=sem, vmem_limit_bytes=VMEM_LIMIT_BYTES)


def _tile(n, pref):
    if n <= pref:
        return n
    for t in range(pref, 7, -1):
        if n % t == 0 and t % 8 == 0:
            return t
    return n


def _dot(a, b):
    return jnp.dot(a, b, preferred_element_type=F32)


def _dot_nt(a, b):
    return lax.dot_general(a, b, (((1,), (1,)), ((), ())), preferred_element_type=F32)


def _dot_tn(a, b):
    return lax.dot_general(a, b, (((0,), (0,)), ((), ())), preferred_element_type=F32)


def _mm_kernel(x_ref, w_ref, o_ref, xb_ref):
    @pl.when(pl.program_id(1) == 0)
    def _():
        xb_ref[...] = x_ref[...].astype(BF16)

    o_ref[...] = _dot(xb_ref[...], w_ref[...].astype(BF16)).astype(o_ref.dtype)


def _mm(x, w, *, layer=None, n=None, tm=1536, tn=512, out_dtype=F32, name="mm"):
    m, k = x.shape
    n = w.shape[-1] if n is None else n
    tm = _tile(m, tm)
    tn = _tile(n, tn)
    if layer is None:
        w_spec = pl.BlockSpec((k, tn), lambda i, j: (0, j))
    else:
        w_spec = pl.BlockSpec((None, k, tn), lambda i, j: (layer, 0, j))
    return pl.pallas_call(
        _mm_kernel,
        grid=(m // tm, n // tn),
        in_specs=[pl.BlockSpec((tm, k), lambda i, j: (i, 0)), w_spec],
        out_specs=pl.BlockSpec((tm, tn), lambda i, j: (i, j)),
        out_shape=jax.ShapeDtypeStruct((m, n), out_dtype),
        scratch_shapes=[pltpu.VMEM((tm, k), BF16)],
        compiler_params=_cparams("parallel", "arbitrary"),
        name=name,
    )(x, w)


def _layer_norm_rows(z, g, b):
    mu = jnp.mean(z, axis=-1, keepdims=True)
    zc = z - mu
    var = jnp.mean(zc * zc, axis=-1, keepdims=True)
    return zc * lax.rsqrt(var + LN_EPS) * g + b


def _rms_norm_rows(z, g):
    return z * lax.rsqrt(jnp.mean(z * z, axis=-1, keepdims=True) + RMS_EPS) * g


def _conv_prompt_kernel(cx_ref, gb_ref, gc_ref, cxp_ref, gcp_ref, w_ref, a_ref, st_ref, *, tt):
    i = pl.program_id(1)
    u = gc_ref[...] * cx_ref[...]
    prev = jnp.where(i > 0, gcp_ref[...] * cxp_ref[...], 0.0)
    row = lax.broadcasted_iota(jnp.int32, u.shape, 0)
    u1 = jnp.where(row == 0, prev[7:8], pltpu.roll(u, 1, 0))
    u2 = jnp.where(row == 0, prev[6:7], jnp.where(row == 1, prev[7:8], pltpu.roll(u, 2, 0)))
    v = w_ref[0:1, :] * u2 + w_ref[1:2, :] * u1 + w_ref[2:3, :] * u
    a_ref[...] = gb_ref[...] * v

    @pl.when(i == pl.num_programs(1) - 1)
    def _():
        st_ref[...] = u[tt - (CONV_K - 1):tt]


def _conv_prompt(h, conv_w, bsz, seq, cw):
    tt = _tile(seq, 512)
    nt = seq // tt
    sub = tt // SUBLANES

    def main(c):
        return pl.BlockSpec((tt, cw), lambda b, i: (b * nt + i, c))

    def halo(c):
        return pl.BlockSpec((SUBLANES, cw), lambda b, i: (jnp.maximum((b * nt + i) * sub - 1, 0), c))

    return pl.pallas_call(
        functools.partial(_conv_prompt_kernel, tt=tt),
        grid=(bsz, nt),
        in_specs=[main(0), main(1), main(2), halo(0), halo(2),
                  pl.BlockSpec((CONV_K, cw), lambda b, i: (0, 0))],
        out_specs=[pl.BlockSpec((tt, cw), lambda b, i: (b * nt + i, 0)),
                   pl.BlockSpec((None, CONV_K - 1, cw), lambda b, i: (b, 0, 0))],
        out_shape=[jax.ShapeDtypeStruct((bsz * seq, cw), F32),
                   jax.ShapeDtypeStruct((bsz, CONV_K - 1, cw), F32)],
        compiler_params=_cparams("parallel", "arbitrary"),
        name="conv_prompt",
    )(h, h, h, h, h, conv_w)


def _conv_sample_kernel(cx_ref, gb_ref, gc_ref, st_ref, w_ref, a_ref, sto_ref, *, nb, ts):
    cw = cx_ref.shape[-1]
    u = (gc_ref[...] * cx_ref[...]).reshape(nb, ts, cw)
    st = st_ref[...]
    t = lax.broadcasted_iota(jnp.int32, u.shape, 1)
    s0 = st[:, 0:1, :]
    s1 = st[:, 1:2, :]
    u1 = jnp.where(t == 0, s1, pltpu.roll(u, 1, 1))
    u2 = jnp.where(t == 0, s0, jnp.where(t == 1, s1, pltpu.roll(u, 2, 1)))
    w = w_ref[...]
    v = w[0:1, :][None] * u2 + w[1:2, :][None] * u1 + w[2:3, :][None] * u
    a_ref[...] = gb_ref[...] * v.reshape(nb * ts, cw)
    sto_ref[...] = u[:, ts - (CONV_K - 1):ts, :]


def _conv_sample(h, state, conv_w, row0, bsz, ts, cw):
    nb = _tile(bsz, 64)
    rows = nb * ts
    off = row0 // rows

    def main(c):
        return pl.BlockSpec((rows, cw), lambda i: (off + i, c))

    return pl.pallas_call(
        functools.partial(_conv_sample_kernel, nb=nb, ts=ts),
        grid=(bsz // nb,),
        in_specs=[main(0), main(1), main(2),
                  pl.BlockSpec((nb, CONV_K - 1, cw), lambda i: (i, 0, 0)),
                  pl.BlockSpec((CONV_K, cw), lambda i: (0, 0))],
        out_specs=[pl.BlockSpec((rows, cw), lambda i: (i, 0)),
                   pl.BlockSpec((nb, CONV_K - 1, cw), lambda i: (i, 0, 0))],
        out_shape=[jax.ShapeDtypeStruct((bsz * ts, cw), F32),
                   jax.ShapeDtypeStruct((bsz, CONV_K - 1, cw), F32)],
        compiler_params=_cparams("parallel"),
        name="conv_sample",
    )(h, h, h, state, conv_w)


def _rope_lanes(x, c, sa, sb):
    half = ROPE_DIM // 2
    return x * c + pltpu.roll(x, LANES - half, 1) * sa + pltpu.roll(x, half, 1) * sb


def _q_proj_kernel(qc_ref, g_ref, w_ref, c_ref, sa_ref, sb_ref, o_ref, *, heads):
    xn = _rms_norm_rows(qc_ref[...], g_ref[...]).astype(BF16)
    q = _dot(xn, w_ref[...]) * MLA_SCALE
    c, sa, sb = c_ref[...], sa_ref[...], sb_ref[...]
    for hd in range(heads):
        lo = hd * HEAD_PAD
        o_ref[:, lo:lo + LANES] = q[:, lo:lo + LANES].astype(o_ref.dtype)
        pe = _rope_lanes(q[:, lo + LANES:lo + HEAD_PAD], c, sa, sb)
        o_ref[:, lo + LANES:lo + HEAD_PAD] = pe.astype(o_ref.dtype)


def _q_proj(h, col_block, g, w_cat, rope_c, rope_sa, rope_sb, heads):
    m = h.shape[0]
    tm = _tile(m, 512)
    n = heads * HEAD_PAD
    row = lambda i: (i, 0)
    return pl.pallas_call(
        functools.partial(_q_proj_kernel, heads=heads),
        grid=(m // tm,),
        in_specs=[pl.BlockSpec((tm, Q_LORA), lambda i: (i, col_block)),
                  pl.BlockSpec((1, Q_LORA), lambda i: (0, 0)),
                  pl.BlockSpec((Q_LORA, n), lambda i: (0, 0)),
                  pl.BlockSpec((tm, LANES), row), pl.BlockSpec((tm, LANES), row),
                  pl.BlockSpec((tm, LANES), row)],
        out_specs=pl.BlockSpec((tm, n), row),
        out_shape=jax.ShapeDtypeStruct((m, n), BF16),
        compiler_params=_cparams("parallel"),
        name="q_proj",
    )(h, g, w_cat, rope_c, rope_sa, rope_sb)


def _kv_norm_kernel(kv_ref, pe_ref, g_ref, c_ref, sa_ref, sb_ref, lat_ref, kpe_ref):
    lat_ref[...] = _rms_norm_rows(kv_ref[...], g_ref[...])
    kpe_ref[...] = _rope_lanes(pe_ref[...], c_ref[...], sa_ref[...], sb_ref[...])


def _kv_norm(h, col_block, h_pe, g, rope_c, rope_sa, rope_sb):
    m = h.shape[0]
    tm = _tile(m, 1024)
    row = lambda i: (i, 0)
    return pl.pallas_call(
        _kv_norm_kernel,
        grid=(m // tm,),
        in_specs=[pl.BlockSpec((tm, KV_LORA), lambda i: (i, col_block)),
                  pl.BlockSpec((tm, LANES), row),
                  pl.BlockSpec((1, KV_LORA), lambda i: (0, 0)),
                  pl.BlockSpec((tm, LANES), row), pl.BlockSpec((tm, LANES), row),
                  pl.BlockSpec((tm, LANES), row)],
        out_specs=[pl.BlockSpec((tm, KV_LORA), row), pl.BlockSpec((tm, LANES), row)],
        out_shape=[jax.ShapeDtypeStruct((m, KV_LORA), F32),
                   jax.ShapeDtypeStruct((m, LANES), F32)],
        compiler_params=_cparams("parallel"),
        name="kv_norm",
    )(h, h_pe, g, rope_c, rope_sa, rope_sb)


def _kv_expand_kernel(lat_ref, kpe_ref, wk_ref, wvt_ref, k_ref, vt_ref, *, heads):
    lat = lat_ref[...].astype(BF16)
    kn = _dot(lat, wk_ref[...])
    kpe = kpe_ref[...].astype(k_ref.dtype)
    for hd in range(heads):
        lo = hd * HEAD_PAD
        k_ref[:, lo:lo + LANES] = kn[:, hd * NOPE_DIM:(hd + 1) * NOPE_DIM].astype(k_ref.dtype)
        k_ref[:, lo + LANES:lo + HEAD_PAD] = kpe
    vt_ref[...] = _dot_nt(wvt_ref[...], lat).astype(vt_ref.dtype)


def _kv_expand(lat, kpe, w_uk, w_uv_t, rows, heads):
    tm = _tile(rows, 512)
    row = lambda i: (i, 0)
    fixed = lambda i: (0, 0)
    return pl.pallas_call(
        functools.partial(_kv_expand_kernel, heads=heads),
        grid=(rows // tm,),
        in_specs=[pl.BlockSpec((tm, KV_LORA), row), pl.BlockSpec((tm, LANES), row),
                  pl.BlockSpec((KV_LORA, heads * NOPE_DIM), fixed),
                  pl.BlockSpec((heads * V_DIM, KV_LORA), fixed)],
        out_specs=[pl.BlockSpec((tm, heads * HEAD_PAD), row),
                   pl.BlockSpec((heads * V_DIM, tm), lambda i: (0, i))],
        out_shape=[jax.ShapeDtypeStruct((rows, heads * HEAD_PAD), BF16),
                   jax.ShapeDtypeStruct((heads * V_DIM, rows), BF16)],
        compiler_params=_cparams("parallel"),
        name="kv_expand",
    )(lat, kpe, w_uk, w_uv_t)


def _attn_prompt_kernel(q_ref, k_ref, vt_ref, o_ref, m_ref, l_ref, acc_ref, *, tq, nh):
    i = pl.program_id(2)
    m_ref[...] = jnp.full(m_ref.shape, -jnp.inf, F32)
    l_ref[...] = jnp.zeros(l_ref.shape, F32)
    acc_ref[...] = jnp.zeros(acc_ref.shape, F32)
    q_all = q_ref[...]

    def chunk(c, masked):
        r0 = pl.multiple_of(c * tq, tq)
        k_all = k_ref[pl.ds(r0, tq), :]
        vt_all = vt_ref[:, pl.ds(r0, tq)]
        m_all, l_all, acc_all = m_ref[...], l_ref[...], acc_ref[...]
        ms, ls, accs = [], [], []
        for hd in range(nh):
            hp = slice(hd * HEAD_PAD, (hd + 1) * HEAD_PAD)
            hv = slice(hd * V_DIM, (hd + 1) * V_DIM)
            s = _dot_nt(k_all[:, hp], q_all[:, hp])
            if masked:
                key = lax.broadcasted_iota(jnp.int32, s.shape, 0)
                qry = lax.broadcasted_iota(jnp.int32, s.shape, 1)
                s = jnp.where(key <= qry, s, -jnp.inf)
            m_old = m_all[hd:hd + 1, :]
            m_new = jnp.maximum(m_old, jnp.max(s, axis=0, keepdims=True))
            alpha = jnp.exp(m_old - m_new)
            p = jnp.exp(s - m_new)
            ms.append(m_new)
            ls.append(alpha * l_all[hd:hd + 1, :] + jnp.sum(p, axis=0, keepdims=True))
            accs.append(alpha * acc_all[hv, :] + _dot(vt_all[hv, :], p.astype(BF16)))
        m_ref[...] = jnp.concatenate(ms, axis=0)
        l_ref[...] = jnp.concatenate(ls, axis=0)
        acc_ref[...] = jnp.concatenate(accs, axis=0)

    def body(c, carry):
        chunk(c, False)
        return carry

    lax.fori_loop(0, i, body, 0)
    chunk(i, True)
    l_all = l_ref[...]
    o_ref[...] = jnp.concatenate(
        [(acc_ref[hd * V_DIM:(hd + 1) * V_DIM, :] / l_all[hd:hd + 1, :]).T for hd in range(nh)],
        axis=1).astype(o_ref.dtype)


def _attn_prompt(q_cat, k_cat, v_t, bsz, seq, heads):
    tq = _tile(seq, ATTN_TQ)
    nq = seq // tq
    nh = math.gcd(ATTN_HEADS_PER_STEP, heads)
    return pl.pallas_call(
        functools.partial(_attn_prompt_kernel, tq=tq, nh=nh),
        grid=(bsz, heads // nh, nq),
        in_specs=[pl.BlockSpec((tq, nh * HEAD_PAD), lambda b, h, i: (b * nq + i, h)),
                  pl.BlockSpec((seq, nh * HEAD_PAD), lambda b, h, i: (b, h)),
                  pl.BlockSpec((nh * V_DIM, seq), lambda b, h, i: (h, b))],
        out_specs=pl.BlockSpec((tq, nh * V_DIM), lambda b, h, i: (b * nq + i, h)),
        out_shape=jax.ShapeDtypeStruct((bsz * seq, heads * V_DIM), F32),
        scratch_shapes=[pltpu.VMEM((nh, tq), F32), pltpu.VMEM((nh, tq), F32),
                        pltpu.VMEM((nh * V_DIM, tq), F32)],
        compiler_params=_cparams("parallel", "parallel", "arbitrary"),
        name="attn_prompt",
    )(q_cat, k_cat, v_t)


def _head_mm_kernel(x_ref, w_ref, o_ref, *, nt):
    x = x_ref[...].astype(BF16)
    o_ref[...] = (_dot_nt(x, w_ref[...]) if nt else _dot(x, w_ref[...])).astype(o_ref.dtype)


def _q_latent(q_cat, w_uk, row0, rows, heads):
    off = row0 // rows
    return pl.pallas_call(
        functools.partial(_head_mm_kernel, nt=True),
        grid=(heads,),
        in_specs=[pl.BlockSpec((rows, NOPE_DIM), lambda h: (off, 2 * h)),
                  pl.BlockSpec((KV_LORA, NOPE_DIM), lambda h: (0, h))],
        out_specs=pl.BlockSpec((None, rows, KV_LORA), lambda h: (h, 0, 0)),
        out_shape=jax.ShapeDtypeStruct((heads, rows, KV_LORA), BF16),
        compiler_params=_cparams("parallel"),
        name="q_latent",
    )(q_cat, w_uk)


def _o_from_latent(o_lat, w_uv, heads):
    rows = o_lat.shape[1]
    return pl.pallas_call(
        functools.partial(_head_mm_kernel, nt=False),
        grid=(heads,),
        in_specs=[pl.BlockSpec((None, rows, KV_LORA), lambda h: (h, 0, 0)),
                  pl.BlockSpec((KV_LORA, V_DIM), lambda h: (0, h))],
        out_specs=pl.BlockSpec((rows, V_DIM), lambda h: (0, h)),
        out_shape=jax.ShapeDtypeStruct((rows, heads * V_DIM), F32),
        compiler_params=_cparams("parallel"),
        name="o_from_latent",
    )(o_lat, w_uv)


def _lane_chunks(s):
    w = s.shape[-1]
    if w % LANES or w == LANES:
        return [s]
    return [s[:, c * LANES:(c + 1) * LANES] for c in range(w // LANES)]


def _row_max(s):
    parts = _lane_chunks(s)
    return jnp.max(functools.reduce(jnp.maximum, parts), axis=-1, keepdims=True)


def _row_sum(s):
    parts = _lane_chunks(s)
    return jnp.sum(functools.reduce(jnp.add, parts), axis=-1, keepdims=True)


def _attn_sample_kernel(pt_ref, ql_ref, qp_ref, lat_hbm, pe_hbm, nl_ref, np_ref, o_ref,
                        lat_buf, pe_buf, lat_sem, pe_sem, k_ref, pe_ref, m_ref, l_ref, acc_ref,
                        *, layer, npg, ts, page):
    s_idx = pl.program_id(1)
    n_groups = pl.num_programs(1)
    lin = pl.program_id(0) * n_groups + s_idx
    slot = lin % 2

    def page_copies(group, slot_, j):
        pid = pt_ref[group * npg + j]
        rows = pl.ds(j * page, page)
        return (pltpu.make_async_copy(lat_hbm.at[layer, pid], lat_buf.at[slot_, rows, :], lat_sem.at[slot_]),
                pltpu.make_async_copy(pe_hbm.at[layer, pid], pe_buf.at[slot_, j], pe_sem.at[slot_]))

    def start_group(group, slot_):
        for j in range(npg):
            for cp in page_copies(group, slot_, j):
                cp.start()

    @pl.when(lin == 0)
    def _():
        start_group(lin, slot)

    @pl.when(lin + 1 < pl.num_programs(0) * n_groups)
    def _():
        start_group(lin + 1, 1 - slot)

    for j in range(npg):
        for cp in page_copies(lin, slot, j):
            cp.wait()

    @pl.when(s_idx == 0)
    def _():
        m_ref[...] = jnp.full(m_ref.shape, -jnp.inf, F32)
        l_ref[...] = jnp.zeros(l_ref.shape, F32)
        acc_ref[...] = jnp.zeros(acc_ref.shape, F32)

    ql = ql_ref[...]
    qp = qp_ref[...]

    def update(s, values):
        m_old = m_ref[...]
        m_new = jnp.maximum(m_old, _row_max(s))
        alpha = jnp.exp(m_old - m_new)
        p = jnp.exp(s - m_new)
        l_ref[...] = alpha * l_ref[...] + _row_sum(p)
        acc_ref[...] = alpha * acc_ref[...] + _dot(p.astype(BF16), values)
        m_ref[...] = m_new

    k_ref[...] = lat_buf[slot].astype(BF16)
    for j in range(npg):
        pe_ref[:, j * page:(j + 1) * page] = pe_buf[slot, j].astype(BF16)
    keys = k_ref[...]
    update(_dot_nt(ql, keys) + _dot(qp, pe_ref[...]), keys)

    @pl.when(s_idx == pl.num_programs(1) - 1)
    def _():
        kn = nl_ref[...].astype(BF16)
        s = _dot_nt(ql, kn) + _dot_nt(qp, np_ref[...].astype(BF16))
        t_q = lax.broadcasted_iota(jnp.int32, s.shape, 0) % ts
        t_k = lax.broadcasted_iota(jnp.int32, s.shape, 1)
        update(jnp.where(t_k <= t_q, s, -jnp.inf), kn)
        o_ref[...] = (acc_ref[...] / l_ref[...]).astype(o_ref.dtype)


def _attn_sample(page_table, q_lat, q_pe, cache_lat, cache_pe_t, layer, new_lat, new_pe, ts):
    bsz, qrows, _ = q_lat.shape
    n_pages = page_table.shape[1]
    page = cache_lat.shape[2]
    npg = math.gcd(PAGES_PER_STEP, n_pages)
    steps = n_pages // npg
    pt_flat = page_table.reshape(-1)

    per_b = lambda b, s, pt: (b, 0, 0)
    keys = npg * page
    grid_spec = pltpu.PrefetchScalarGridSpec(
        num_scalar_prefetch=1,
        grid=(bsz, steps),
        in_specs=[pl.BlockSpec((None, qrows, KV_LORA), per_b),
                  pl.BlockSpec((None, qrows, ROPE_DIM), per_b),
                  pl.BlockSpec(memory_space=pl.ANY), pl.BlockSpec(memory_space=pl.ANY),
                  pl.BlockSpec((None, 2 * ts, KV_LORA), per_b),
                  pl.BlockSpec((None, 2 * ts, ROPE_DIM), per_b)],
        out_specs=pl.BlockSpec((None, qrows, KV_LORA), per_b),
        scratch_shapes=[pltpu.VMEM((2, keys, KV_LORA), F32), pltpu.VMEM((2, npg, ROPE_DIM, page), F32),
                        pltpu.SemaphoreType.DMA((2,)), pltpu.SemaphoreType.DMA((2,)),
                        pltpu.VMEM((keys, KV_LORA), BF16), pltpu.VMEM((ROPE_DIM, keys), BF16),
                        pltpu.VMEM((qrows, 1), F32), pltpu.VMEM((qrows, 1), F32),
                        pltpu.VMEM((qrows, KV_LORA), F32)],
    )
    return pl.pallas_call(
        functools.partial(_attn_sample_kernel, layer=layer, npg=npg, ts=ts, page=page),
        grid_spec=grid_spec,
        out_shape=jax.ShapeDtypeStruct((bsz, qrows, KV_LORA), BF16),
        compiler_params=_cparams("arbitrary", "arbitrary"),
        name="attn_sample",
    )(pt_flat, q_lat, q_pe, cache_lat, cache_pe_t, new_lat, new_pe)


def _two_part_specs(tm, width, nt_p):
    return (pl.BlockSpec((tm, width), lambda i: (jnp.minimum(i, nt_p - 1), 0)),
            pl.BlockSpec((tm, width), lambda i: (jnp.maximum(i - nt_p, 0), 0)))


def _pick_part(nt_p, p_ref, s_ref):
    return jnp.where(pl.program_id(0) < nt_p, p_ref[...], s_ref[...])


def _hyb_out_kernel(ap_ref, as_ref, op_ref, os_ref, w_ref, x_ref, g_ref, b_ref, y_ref, *, alpha, nt_p):
    a = _pick_part(nt_p, ap_ref, as_ref).astype(BF16)
    o = _pick_part(nt_p, op_ref, os_ref).astype(BF16)
    ka = a.shape[1]
    mix = _dot(a, w_ref[0:ka, :]) + _dot(o, w_ref[ka:, :])
    y_ref[...] = _layer_norm_rows(alpha * x_ref[...] + mix, g_ref[...], b_ref[...])


def _hyb_out(a_p, a_s, o_p, o_s, w_out, x, g, b, alpha):
    m, d = x.shape
    n_p, n_s = a_p.shape[0], a_s.shape[0]
    tm = _tile(math.gcd(n_p, n_s), 256)
    nt_p = n_p // tm
    row = lambda i: (i, 0)
    fixed = lambda i: (0, 0)
    return pl.pallas_call(
        functools.partial(_hyb_out_kernel, alpha=alpha, nt_p=nt_p),
        grid=(m // tm,),
        in_specs=[*_two_part_specs(tm, a_p.shape[1], nt_p), *_two_part_specs(tm, o_p.shape[1], nt_p),
                  pl.BlockSpec(w_out.shape, fixed), pl.BlockSpec((tm, d), row),
                  pl.BlockSpec((1, d), fixed), pl.BlockSpec((1, d), fixed)],
        out_specs=pl.BlockSpec((tm, d), row),
        out_shape=jax.ShapeDtypeStruct((m, d), F32),
        compiler_params=_cparams("parallel"),
        name="hyb_out",
    )(a_p, a_s, o_p, o_s, w_out, x, g, b)


def _rec_out_kernel(op_ref, os_ref, gate_ref, ng_ref, w_ref, x_ref, g_ref, b_ref, y_ref, *, alpha, nt_p):
    gate = gate_ref[...]
    o = _pick_part(nt_p, op_ref, os_ref)
    on = _rms_norm_rows(o, ng_ref[...]) * (gate * jax.nn.sigmoid(gate))
    mix = _dot(on.astype(BF16), w_ref[...])
    y_ref[...] = _layer_norm_rows(alpha * x_ref[...] + mix, g_ref[...], b_ref[...])


def _rec_out(o_p, o_s, h, gate_col_block, norm_g, w_out, x, g, b, alpha):
    m, d = x.shape
    vw = o_p.shape[1]
    n_p, n_s = o_p.shape[0], o_s.shape[0]
    tm = _tile(math.gcd(n_p, n_s), 256)
    nt_p = n_p // tm
    row = lambda i: (i, 0)
    fixed = lambda i: (0, 0)
    return pl.pallas_call(
        functools.partial(_rec_out_kernel, alpha=alpha, nt_p=nt_p),
        grid=(m // tm,),
        in_specs=[*_two_part_specs(tm, vw, nt_p),
                  pl.BlockSpec((tm, vw), lambda i: (i, gate_col_block)),
                  pl.BlockSpec((1, vw), fixed), pl.BlockSpec(w_out.shape, fixed),
                  pl.BlockSpec((tm, d), row), pl.BlockSpec((1, d), fixed),
                  pl.BlockSpec((1, d), fixed)],
        out_specs=pl.BlockSpec((tm, d), row),
        out_shape=jax.ShapeDtypeStruct((m, d), F32),
        compiler_params=_cparams("parallel"),
        name="rec_out",
    )(o_p, o_s, h, norm_g, w_out, x, g, b)


def _gla_levels(chunk):
    m = chunk // 2
    out = []
    while m >= 1:
        out.append(m)
        m //= 2
    return tuple(out)


def _gla_prefix_matrix(chunk):
    t = np.arange(chunk)
    return (t[None, :] <= t[:, None]).astype(np.float32)


def _gla_block_ref(cum, m):
    c, w = cum.shape
    if 2 * m >= SUBLANES:
        parts = [jnp.broadcast_to(cum[b + m:b + m + 1, :], (2 * m, w)) for b in range(0, c, 2 * m)]
        return parts[0] if len(parts) == 1 else jnp.concatenate(parts, axis=0)
    row = lax.broadcasted_iota(jnp.int32, cum.shape, 0)
    if m == 1:
        return jnp.where(row % 2 == 1, cum, pltpu.roll(cum, c - 1, 0))
    assert m == 2
    sub = row % 4
    return jnp.where(sub == 2, cum,
                     jnp.where(sub == 3, pltpu.roll(cum, 1, 0),
                               jnp.where(sub == 1, pltpu.roll(cum, c - 1, 0), pltpu.roll(cum, c - 2, 0))))


def _gla_gates(q_raw, f_raw, lb):
    q = q_raw * jax.nn.sigmoid(q_raw) * (REC_DK ** -0.5)
    forget = lb + (1.0 - lb) * jax.nn.sigmoid(f_raw)
    return q, 1.0 - forget, jnp.log(forget)


def _split3(x):
    hi = x.astype(BF16)
    r1 = x - hi.astype(F32)
    mid = r1.astype(BF16)
    lo = (r1 - mid.astype(F32)).astype(BF16)
    return hi, mid, lo


def _gla_pair_masks(chunk):
    t = np.arange(chunk)[:, None]
    s = np.arange(chunk)[None, :]
    masks = [t == s]
    for m in _gla_levels(chunk):
        masks.append((t // (2 * m) == s // (2 * m)) & ((t // m) % 2 == 1) & ((s // m) % 2 == 0))
    return np.stack(masks).astype(np.float32)


def _gla_prompt_kernel(q_ref, f_ref, v_ref, lb_ref, pm_ref, mask_ref, o_ref, st_ref, s_ref, *, tt, gh):
    c = GLA_CHUNK
    dk = REC_DK
    i = pl.program_id(2)

    @pl.when(i == 0)
    def _():
        s_ref[...] = jnp.zeros(s_ref.shape, F32)

    lb = lb_ref[...]
    pm = pm_ref[...]
    levels = _gla_levels(c)

    def level_operand(q, k, cum, m):
        if m >= SUBLANES:
            parts = []
            for b in range(0, c, 2 * m):
                ref = cum[b + m:b + m + 1, :]
                parts.append(k[b:b + m] * jnp.exp(ref - cum[b:b + m]))
                parts.append(q[b + m:b + 2 * m] * jnp.exp(cum[b + m:b + 2 * m] - ref))
            return jnp.concatenate(parts, axis=0)
        row = lax.broadcasted_iota(jnp.int32, q.shape, 0)
        return (jnp.where((row // m) % 2 == 1, q, k)
                * jnp.exp(-jnp.abs(cum - _gla_block_ref(cum, m))))

    def body(ci, carry):
        r0 = pl.multiple_of(ci * c, c)
        q_all, k_all, g_all = _gla_gates(q_ref[pl.ds(r0, c), :], f_ref[pl.ds(r0, c), :], lb)
        v_all = v_ref[pl.ds(r0, c), :].astype(BF16)
        hi, mid, lo = _split3(g_all)
        cum_all = _dot(pm, hi) + _dot(pm, mid) + _dot(pm, lo)
        last_all = cum_all[c - 1:c, :]
        z_all = [level_operand(q_all, k_all, cum_all, m).astype(BF16) for m in levels]
        qe_all = (q_all * jnp.exp(cum_all)).astype(BF16)
        kd_all = (k_all * jnp.exp(last_all - cum_all)).astype(BF16)
        dec_all = jnp.exp(last_all)
        qb_all, kb_all = q_all.astype(BF16), k_all.astype(BF16)
        outs, states = [], []
        for hd in range(gh):
            sl = slice(hd * dk, (hd + 1) * dk)
            att = mask_ref[0] * _dot_nt(qb_all[:, sl], kb_all[:, sl])
            for li, z in enumerate(z_all):
                att = att + mask_ref[li + 1] * _dot_nt(z[:, sl], z[:, sl])
            s_t = s_ref[hd * dk:(hd + 1) * dk, :]
            inter = _dot_nt(qe_all[:, sl], s_t.astype(BF16))
            outs.append(inter + _dot(att.astype(BF16), v_all[:, sl]))
            states.append(s_t * dec_all[:, sl] + _dot_tn(v_all[:, sl], kd_all[:, sl]))
        o_ref[pl.ds(r0, c), :] = jnp.concatenate(outs, axis=1)
        s_ref[...] = jnp.concatenate(states, axis=0)
        return carry

    lax.fori_loop(0, tt // c, body, 0)

    @pl.when(i == pl.num_programs(2) - 1)
    def _():
        for hd in range(gh):
            st_ref[hd] = s_ref[hd * dk:(hd + 1) * dk, :]


def _gla_prompt(h, lb, bsz, seq, heads):
    tt = _tile(seq, 512)
    assert tt % GLA_CHUNK == 0
    nt = seq // tt
    gh = math.gcd(GLA_HEADS_PER_STEP, heads)
    ng = heads // gh
    wide = gh * REC_DK
    pm = jnp.asarray(_gla_prefix_matrix(GLA_CHUNK), BF16)
    masks = jnp.asarray(_gla_pair_masks(GLA_CHUNK), F32)

    def col(base):
        return pl.BlockSpec((tt, wide), lambda b, hg, i: (b * nt + i, base + hg))

    return pl.pallas_call(
        functools.partial(_gla_prompt_kernel, tt=tt, gh=gh),
        grid=(bsz, ng, nt),
        in_specs=[col(0), col(ng), col(2 * ng),
                  pl.BlockSpec((1, wide), lambda b, hg, i: (0, hg)),
                  pl.BlockSpec(pm.shape, lambda b, hg, i: (0, 0)),
                  pl.BlockSpec(masks.shape, lambda b, hg, i: (0, 0, 0))],
        out_specs=[pl.BlockSpec((tt, wide), lambda b, hg, i: (b * nt + i, hg)),
                   pl.BlockSpec((None, gh, REC_DK, REC_DK), lambda b, hg, i: (b, hg, 0, 0))],
        out_shape=[jax.ShapeDtypeStruct((bsz * seq, heads * REC_DK), F32),
                   jax.ShapeDtypeStruct((bsz, heads, REC_DK, REC_DK), F32)],
        scratch_shapes=[pltpu.VMEM((wide, REC_DK), F32)],
        compiler_params=_cparams("parallel", "parallel", "arbitrary"),
        name="gla_prompt",
    )(h, h, h, lb, pm, masks)


def _gla_sample_kernel(q_ref, f_ref, v_ref, lb_ref, hs_ref, he_ref, st_ref, o_ref, sto_ref, *, ts, heads):
    dk = REC_DK
    q, k, g = _gla_gates(q_ref[...], f_ref[...], lb_ref[...])
    v = v_ref[...]
    t = lax.broadcasted_iota(jnp.int32, q.shape, 0)
    cum = g
    sh = 1
    while sh < ts:
        cum = cum + jnp.where(t >= sh, pltpu.roll(cum, sh, 0), 0.0)
        sh *= 2
    last = cum[ts - 1:ts, :]
    xs = []
    for s in range(ts):
        dec = jnp.exp(jnp.where(t >= s, cum - cum[s:s + 1, :], -jnp.inf))
        xs.append(q * k[s:s + 1, :] * dec)
    x = jnp.concatenate(xs, axis=0)
    x_hi = x.astype(BF16)
    x_lo = (x - x_hi.astype(F32)).astype(BF16)
    att = _dot(x_hi, hs_ref[...]) + _dot(x_lo, hs_ref[...])
    att_e = _dot(att.astype(BF16), he_ref[...])
    intra = jnp.zeros(q.shape, F32)
    for s in range(ts):
        intra = intra + att_e[s * ts:(s + 1) * ts, :] * v[s:s + 1, :]
    qe = (q * jnp.exp(cum)).astype(BF16)
    kd = (k * jnp.exp(last - cum)).astype(BF16)
    dec_last = jnp.exp(last)
    eye = (lax.broadcasted_iota(jnp.int32, (dk, dk), 0)
           == lax.broadcasted_iota(jnp.int32, (dk, dk), 1))
    vb = v.astype(BF16)
    for hd in range(heads):
        sl = slice(hd * dk, (hd + 1) * dk)
        s_h = st_ref[hd]
        o_ref[:, sl] = _dot(qe[:, sl], s_h.astype(BF16)) + intra[:, sl]
        d_col = jnp.sum(jnp.where(eye, dec_last[:, sl], 0.0), axis=1, keepdims=True)
        sto_ref[hd] = s_h * d_col + _dot_tn(kd[:, sl], vb[:, sl])


def _gla_sample(h, lb, state, j, row0, bsz, ts, heads):
    width = heads * REC_DK
    off = row0 // ts
    head_of = np.arange(width) // REC_DK
    hsum = (head_of[:, None] == np.arange(LANES)[None, :]).astype(np.float32)
    hs = jnp.asarray(hsum, BF16)
    he = jnp.asarray(hsum.T, BF16)

    def col(cb):
        return pl.BlockSpec((ts, width), lambda b: (off + b, cb))

    fixed = lambda b: (0, 0)
    st_in = pl.BlockSpec((None, None, heads, REC_DK, REC_DK), lambda b: (j, b, 0, 0, 0))
    st_out = pl.BlockSpec((None, heads, REC_DK, REC_DK), lambda b: (b, 0, 0, 0))
    return pl.pallas_call(
        functools.partial(_gla_sample_kernel, ts=ts, heads=heads),
        grid=(bsz,),
        in_specs=[col(0), col(1), col(2), pl.BlockSpec((1, width), fixed),
                  pl.BlockSpec(hs.shape, fixed), pl.BlockSpec(he.shape, fixed), st_in],
        out_specs=[pl.BlockSpec((ts, width), lambda b: (b, 0)), st_out],
        out_shape=[jax.ShapeDtypeStruct((bsz * ts, width), F32),
                   jax.ShapeDtypeStruct(state.shape[1:], F32)],
        compiler_params=_cparams("parallel"),
        name="gla_sample",
    )(h, h, h, lb, hs, he, state)


def _router_kernel(x_ref, wh_ref, wl_ref, b_ref, o_ref):
    x = x_ref[...]
    x_hi = x.astype(BF16)
    x_lo = (x - x_hi.astype(F32)).astype(BF16)
    wh = wh_ref[...]
    logits = _dot(x_hi, wh) + _dot(x_lo, wh) + _dot(x_hi, wl_ref[...]) + b_ref[...]
    lane = lax.broadcasted_iota(jnp.int32, logits.shape, 1).astype(F32)
    big = float(LANES)
    neg = -jnp.inf
    gl = jnp.where(lane < N_GROUPS, logits, neg)
    g_max = jnp.max(gl, axis=-1, keepdims=True)
    g_sel = jnp.min(jnp.where(gl == g_max, lane, big), axis=-1, keepdims=True)
    g_gate = 1.0 / jnp.sum(jnp.exp(gl - g_max), axis=-1, keepdims=True)
    lo = N_GROUPS + g_sel * EXPERTS_PER_GROUP
    el = jnp.where((lane >= lo) & (lane < lo + EXPERTS_PER_GROUP), logits, neg)
    e_max = jnp.max(el, axis=-1, keepdims=True)
    i1 = jnp.min(jnp.where(el == e_max, lane, big), axis=-1, keepdims=True)
    denom = jnp.sum(jnp.exp(el - e_max), axis=-1, keepdims=True)
    el2 = jnp.where(lane == i1, neg, el)
    m2 = jnp.max(el2, axis=-1, keepdims=True)
    i2 = jnp.min(jnp.where(el2 == m2, lane, big), axis=-1, keepdims=True)
    p1 = 1.0 / denom
    p2 = jnp.exp(m2 - e_max) / denom
    w1 = g_gate * p1 / (p1 + p2)
    w2 = g_gate * p2 / (p1 + p2)
    e1 = i1 - N_GROUPS
    e2 = i2 - N_GROUPS
    o_ref[...] = jnp.where(lane == 0, e1, jnp.where(lane == 1, e2,
                           jnp.where(lane == 2, w1, jnp.where(lane == 3, w2, 0.0))))


def _router(x, w_hi, w_lo, bias):
    m, d = x.shape
    tm = _tile(m, 512)
    fixed = lambda i: (0, 0)
    return pl.pallas_call(
        _router_kernel,
        grid=(m // tm,),
        in_specs=[pl.BlockSpec((tm, d), lambda i: (i, 0)), pl.BlockSpec((d, LANES), fixed),
                  pl.BlockSpec((d, LANES), fixed), pl.BlockSpec((1, LANES), fixed)],
        out_specs=pl.BlockSpec((tm, LANES), lambda i: (i, 0)),
        out_shape=jax.ShapeDtypeStruct((m, LANES), F32),
        compiler_params=_cparams("parallel"),
        name="router",
    )(x, w_hi, w_lo, bias)


def _expert_weights(t, te_ref, first_ref, slot_ref, nxt_ref, layer, hbm_refs, buf_refs, sem_refs, bf_refs):
    def copies(expert, slot):
        return [pltpu.make_async_copy(hbm.at[layer, expert], buf.at[slot], sem.at[slot])
                for hbm, buf, sem in zip(hbm_refs, buf_refs, sem_refs)]

    @pl.when(t == 0)
    def _():
        for cp in copies(te_ref[0], slot_ref[0]):
            cp.start()

    @pl.when(first_ref[t] == 1)
    def _():
        slot = slot_ref[t]

        @pl.when(nxt_ref[t] >= 0)
        def _():
            for cp in copies(nxt_ref[t], 1 - slot):
                cp.start()

        for cp in copies(te_ref[t], slot):
            cp.wait()
        for buf, bf in zip(buf_refs, bf_refs):
            bf[...] = buf[slot].astype(BF16)


def _moe_up_kernel(te_ref, first_ref, slot_ref, nxt_ref, nu_ref, x_ref, wg_hbm, wu_hbm, h_ref,
                   wg_buf, wu_buf, wg_sem, wu_sem, wgb_ref, wub_ref, *, layer):
    t = pl.program_id(0)

    @pl.when(t < nu_ref[0])
    def _():
        _expert_weights(t, te_ref, first_ref, slot_ref, nxt_ref, layer, (wg_hbm, wu_hbm),
                        (wg_buf, wu_buf), (wg_sem, wu_sem), (wgb_ref, wub_ref))
        x = x_ref[...].astype(BF16)
        gate = _dot(x, wgb_ref[...])
        up = _dot(x, wub_ref[...])
        h_ref[...] = (gate * jax.nn.sigmoid(gate) * up).astype(h_ref.dtype)

    @pl.when(t >= nu_ref[0])
    def _():
        h_ref[...] = jnp.zeros(h_ref.shape, h_ref.dtype)


def _moe_down_kernel(te_ref, first_ref, slot_ref, nxt_ref, nu_ref, h_ref, wd_hbm, y_ref,
                     wd_buf, wd_sem, wdb_ref, *, layer):
    t = pl.program_id(0)

    @pl.when(t < nu_ref[0])
    def _():
        _expert_weights(t, te_ref, first_ref, slot_ref, nxt_ref, layer, (wd_hbm,), (wd_buf,),
                        (wd_sem,), (wdb_ref,))
        y_ref[...] = _dot(h_ref[...], wdb_ref[...])

    @pl.when(t >= nu_ref[0])
    def _():
        y_ref[...] = jnp.zeros(y_ref.shape, y_ref.dtype)


def _moe_experts(xs, plan, w_gate, w_up, w_down, layer):
    cap, d = xs.shape
    de = w_gate.shape[-1]
    n_tiles = cap // MOE_TILE
    n_plan = len(plan)

    def live(t, *refs):
        return (jnp.minimum(t, refs[n_plan - 1][0] - 1), 0)

    own = lambda t, *refs: (t, 0)
    hbm = pl.BlockSpec(memory_space=pl.ANY)
    two = lambda *shape: pltpu.VMEM((2,) + shape, F32)
    sem = pltpu.SemaphoreType.DMA((2,))

    hid = pl.pallas_call(
        functools.partial(_moe_up_kernel, layer=layer),
        grid_spec=pltpu.PrefetchScalarGridSpec(
            num_scalar_prefetch=n_plan,
            grid=(n_tiles,),
            in_specs=[pl.BlockSpec((MOE_TILE, d), live), hbm, hbm],
            out_specs=pl.BlockSpec((MOE_TILE, de), own),
            scratch_shapes=[two(d, de), two(d, de), sem, sem,
                            pltpu.VMEM((d, de), BF16), pltpu.VMEM((d, de), BF16)]),
        out_shape=jax.ShapeDtypeStruct((cap, de), BF16),
        compiler_params=_cparams("arbitrary"),
        name="moe_up",
    )(*plan, xs, w_gate, w_up)

    return pl.pallas_call(
        functools.partial(_moe_down_kernel, layer=layer),
        grid_spec=pltpu.PrefetchScalarGridSpec(
            num_scalar_prefetch=n_plan,
            grid=(n_tiles,),
            in_specs=[pl.BlockSpec((MOE_TILE, de), live), hbm],
            out_specs=pl.BlockSpec((MOE_TILE, d), own),
            scratch_shapes=[two(de, d), sem, pltpu.VMEM((de, d), BF16)]),
        out_shape=jax.ShapeDtypeStruct((cap, d), F32),
        compiler_params=_cparams("arbitrary"),
        name="moe_down",
    )(*plan, hid, w_down)


def _moe_ln_kernel(x_ref, y0_ref, y1_ref, r_ref, g_ref, b_ref, o_ref, *, alpha):
    r = r_ref[...]
    moe = r[:, TOP_K:TOP_K + 1] * y0_ref[...] + r[:, TOP_K + 1:TOP_K + 2] * y1_ref[...]
    o_ref[...] = _layer_norm_rows(alpha * x_ref[...] + moe, g_ref[...], b_ref[...])


def _moe_ln(x, y2, routed, g, b, alpha):
    m, d = x.shape
    tm = _tile(m, 512)
    nt = m // tm
    row = lambda i: (i, 0)
    fixed = lambda i: (0, 0)
    return pl.pallas_call(
        functools.partial(_moe_ln_kernel, alpha=alpha),
        grid=(nt,),
        in_specs=[pl.BlockSpec((tm, d), row), pl.BlockSpec((tm, d), row),
                  pl.BlockSpec((tm, d), lambda i: (nt + i, 0)), pl.BlockSpec((tm, LANES), row),
                  pl.BlockSpec((1, d), fixed), pl.BlockSpec((1, d), fixed)],
        out_specs=pl.BlockSpec((tm, d), row),
        out_shape=jax.ShapeDtypeStruct((m, d), F32),
        compiler_params=_cparams("parallel"),
        name="moe_ln",
    )(x, y2, y2, routed, g, b)


def _moe_layer(x, layer, w_rg, b_rg, w_re, b_re, w_gate, w_up, w_down, ln_g, ln_b, alpha):
    m, d = x.shape
    wr = jnp.concatenate([w_rg, w_re], axis=1)
    wr = jnp.pad(wr, ((0, 0), (0, LANES - wr.shape[1])))
    wr_hi = wr.astype(BF16)
    wr_lo = (wr - wr_hi.astype(F32)).astype(BF16)
    br = jnp.pad(jnp.concatenate([b_rg, b_re]), (0, LANES - N_GROUPS - N_EXPERTS))[None, :]
    routed = _router(x, wr_hi, wr_lo, br)
    ids = routed[:, 0:TOP_K].astype(jnp.int32)

    n_assign = m * TOP_K
    n_tiles = -(-n_assign // MOE_TILE) + N_EXPERTS
    cap = n_tiles * MOE_TILE
    flat_e = ids.reshape(-1)
    onehot = (flat_e[:, None] == jnp.arange(N_EXPERTS, dtype=jnp.int32)[None, :]).astype(jnp.int32)
    csum = jnp.cumsum(onehot, axis=0)
    counts = csum[-1]
    rank = jnp.take_along_axis(csum, flat_e[:, None], axis=1)[:, 0] - 1
    tiles_per = (counts + MOE_TILE - 1) // MOE_TILE
    tile_ends = jnp.cumsum(tiles_per)
    pad_starts = (tile_ends - tiles_per) * MOE_TILE
    dest = pad_starts[flat_e] + rank
    n_used = tile_ends[-1:].astype(jnp.int32)
    tile_idx = jnp.arange(n_tiles, dtype=jnp.int32)
    tile_e = jnp.sum((tile_ends[None, :] <= tile_idx[:, None]).astype(jnp.int32), axis=1)
    last_e = jnp.max(jnp.where(counts > 0, jnp.arange(N_EXPERTS), 0))
    tile_e = jnp.minimum(tile_e, last_e).astype(jnp.int32)
    prev_e = jnp.concatenate([jnp.full((1,), -1, jnp.int32), tile_e[:-1]])
    first = ((tile_e != prev_e) & (tile_idx < n_used[0])).astype(jnp.int32)
    slot = (jnp.cumsum(first) - 1) % 2
    next_tile = tile_ends[tile_e]
    nxt = jnp.where(next_tile < n_used[0], tile_e[jnp.minimum(next_tile, n_tiles - 1)], -1)
    plan = (tile_e, first, slot.astype(jnp.int32), nxt.astype(jnp.int32), n_used)
    tok = jnp.arange(n_assign, dtype=jnp.int32) // TOP_K
    slot_tok = (jnp.arange(cap, dtype=jnp.int32) % m).at[dest].set(tok)

    xs = x[slot_tok]
    y = _moe_experts(xs, plan, w_gate, w_up, w_down, layer)
    y2 = y[dest.reshape(m, TOP_K).T.reshape(-1)]
    return _moe_ln(x, y2, routed, ln_g[None, :], ln_b[None, :], alpha)


def _rope_tables(pos):
    half = ROPE_DIM // 2
    inv_freq = ROPE_THETA ** (-jnp.arange(half, dtype=F32) / half)
    ang = pos.astype(F32)[:, None] * inv_freq
    cos, sin = jnp.cos(ang), jnp.sin(ang)
    z = jnp.zeros_like(cos)
    pad = jnp.zeros((pos.shape[0], LANES - ROPE_DIM), F32)
    c = jnp.concatenate([cos, cos, pad], axis=1)
    sa = jnp.concatenate([-sin, z, pad], axis=1)
    sb = jnp.concatenate([z, sin, pad], axis=1)
    return c, sa, sb


def _hybrid_layer(x, dims, rope, conv_state, cache_lat, cache_pe, page_table, j,
                  w_in, conv_w, q_g, kv_g, w_q_up, w_uk, w_uv, w_out, ln_g, ln_b, alpha):
    bp, sp, bs, ts = dims
    n_p = bp * sp
    cw = conv_w.shape[1]
    heads = w_q_up.shape[1]
    n_main = 3 * cw + Q_LORA + KV_LORA

    w_pe = jnp.pad(w_in[j, :, n_main:], ((0, 0), (0, LANES - ROPE_DIM))).astype(BF16)
    h = _mm(x, w_in, layer=j, n=n_main, name="hyb_in")
    h_pe = _mm(x, w_pe, name="hyb_in_pe")

    a_p, conv_p = _conv_prompt(h, conv_w, bp, sp, cw)
    a_s, conv_s = _conv_sample(h, conv_state, conv_w, n_p, bs, ts, cw)

    w_q_cat = jnp.pad(w_q_up, ((0, 0), (0, 0), (0, HEAD_PAD - NOPE_DIM - ROPE_DIM)))
    w_q_cat = w_q_cat.reshape(Q_LORA, heads * HEAD_PAD).astype(BF16)
    q_cat = _q_proj(h, 3 * cw // Q_LORA, q_g[None, :], w_q_cat, *rope, heads)
    lat, kpe = _kv_norm(h, (3 * cw + Q_LORA) // KV_LORA, h_pe, kv_g[None, :], *rope)
    w_uk2 = w_uk.reshape(KV_LORA, heads * NOPE_DIM).astype(BF16)
    w_uv2 = w_uv.reshape(KV_LORA, heads * V_DIM).astype(BF16)
    w_uv_t = w_uv2.T

    k_cat, v_t = _kv_expand(lat, kpe, w_uk2, w_uv_t, n_p, heads)
    o_p = _attn_prompt(q_cat, k_cat, v_t, bp, sp, heads)

    n_s = bs * ts
    q_lat = _q_latent(q_cat, w_uk2, n_p, n_s, heads)
    q_lat = q_lat.reshape(heads, bs, ts, KV_LORA).transpose(1, 0, 2, 3).reshape(bs, heads * ts, KV_LORA)
    q_pe = q_cat[n_p:].reshape(bs, ts, heads, HEAD_PAD)[..., NOPE_DIM:NOPE_DIM + ROPE_DIM]
    q_pe = q_pe.transpose(0, 2, 1, 3).reshape(bs, heads * ts, ROPE_DIM)
    lat_s = lat[n_p:].reshape(bs, ts, KV_LORA)
    kpe_s = kpe[n_p:, :ROPE_DIM].reshape(bs, ts, ROPE_DIM)
    new_lat = jnp.pad(lat_s, ((0, 0), (0, ts), (0, 0)))
    new_pe = jnp.pad(kpe_s, ((0, 0), (0, ts), (0, 0)))
    o_lat = _attn_sample(page_table, q_lat, q_pe, cache_lat, cache_pe, j, new_lat, new_pe, ts)
    o_lat = o_lat.reshape(bs, heads, ts, KV_LORA).transpose(1, 0, 2, 3).reshape(heads, n_s, KV_LORA)
    o_s = _o_from_latent(o_lat, w_uv2, heads)

    y = _hyb_out(a_p, a_s, o_p, o_s, w_out.astype(BF16), x, ln_g[None, :], ln_b[None, :], alpha)
    outs = (lat[:n_p].reshape(bp, sp, KV_LORA), kpe[:n_p, :ROPE_DIM].reshape(bp, sp, ROPE_DIM),
            lat_s, kpe_s, conv_p, conv_s)
    return y, outs


def _rec_layer(x, dims, state, j, lb, w_in, norm_g, w_out, ln_g, ln_b, alpha):
    bp, sp, bs, ts = dims
    n_p = bp * sp
    width = w_in.shape[-1] // 4
    heads = width // REC_DK
    h = _mm(x, w_in, layer=j, name="rec_in")
    lb2 = lb[None, :]
    o_p, st_p = _gla_prompt(h, lb2, bp, sp, heads)
    o_s, st_s = _gla_sample(h, lb2, state, j, n_p, bs, ts, heads)
    y = _rec_out(o_p, o_s, h, 3, norm_g[None, :], w_out.astype(BF16), x, ln_g[None, :], ln_b[None, :], alpha)
    return y, (jnp.swapaxes(st_p, -1, -2), st_s)


def kernel(x_prompt, x_sample, cache_kv_latent, cache_k_rope, state_conv, state_hgrn, page_table,
           w_in_hyb, conv_w, q_norm_g, kv_norm_g, w_q_up, w_uk, w_uv, w_out_hyb,
           w_in_rec, lb_logits, rec_norm_g, w_out_rec,
           ln1_g, ln1_b, ln2_g, ln2_b,
           w_router_group, b_router_group, w_router_expert, b_router_expert,
           w_gate, w_up, w_down):
    bp, sp, d = x_prompt.shape
    bs, ts, _ = x_sample.shape
    depth = ln1_g.shape[0]
    past_len = page_table.shape[1] * cache_kv_latent.shape[2]
    dims = (bp, sp, bs, ts)
    n_p = bp * sp
    alpha = (2 * depth) ** 0.25

    lower = jnp.cumsum(jax.nn.softmax(lb_logits.astype(F32), axis=0), axis=0)
    lower = lower - lower[0]

    pos = jnp.concatenate([jnp.tile(jnp.arange(sp), bp), jnp.tile(past_len + jnp.arange(ts), bs)])
    rope = _rope_tables(pos)

    x = jnp.concatenate([x_prompt.reshape(n_p, d), x_sample.reshape(bs * ts, d)], axis=0)
    cache_pe_t = jnp.swapaxes(cache_k_rope, 2, 3)
    hyb_outs, rec_outs = [], []
    for layer in range(depth):
        j = layer // 2
        if layer % 2 == 0:
            x, outs = _hybrid_layer(x, dims, rope, state_conv[j], cache_kv_latent, cache_pe_t, page_table, j,
                                    w_in_hyb, conv_w[j], q_norm_g[j], kv_norm_g[j], w_q_up[j], w_uk[j],
                                    w_uv[j], w_out_hyb[j], ln1_g[layer], ln1_b[layer], alpha)
            hyb_outs.append(outs)
        else:
            x, outs = _rec_layer(x, dims, state_hgrn, j, lower[layer], w_in_rec, rec_norm_g[j],
                                 w_out_rec[j], ln1_g[layer], ln1_b[layer], alpha)
            rec_outs.append(outs)
        x = _moe_layer(x, layer, w_router_group[layer], b_router_group[layer], w_router_expert[layer],
                       b_router_expert[layer], w_gate, w_up, w_down, ln2_g[layer], ln2_b[layer], alpha)

    stack = lambda parts, i: jnp.stack([p[i] for p in parts])
    return (x[:n_p].reshape(bp, sp, d), x[n_p:].reshape(bs, ts, d),
            stack(hyb_outs, 0), stack(hyb_outs, 1), stack(hyb_outs, 2), stack(hyb_outs, 3),
            stack(hyb_outs, 4), stack(hyb_outs, 5), stack(rec_outs, 0), stack(rec_outs, 1))
```
